```python
import math
import jax
import jax.numpy as jnp
from jax import lax
import numpy as np

D_MODEL = 1024
BATCH = 8
SEQ = 2048
DEPTH = 4

SSM_WIDTH = D_MODEL // 2
SSM_GROUP = 16
SSM_GROUPS = SSM_WIDTH // SSM_GROUP
SSM_STATE = 64
DT_MIN = 0.001
DT_MAX = 0.1
DIFF_HEADS = 4
DIFF_HEAD_DIM = 64
DIFF_V_DIM = 2 * DIFF_HEAD_DIM
DIFF_WIDTH = DIFF_HEADS * DIFF_V_DIM
FOX_HEADS = 8
FOX_HEAD_DIM = 64
FOX_WIDTH = FOX_HEADS * FOX_HEAD_DIM
N_BRANCH = 3
BRANCH_WIDTH = SSM_WIDTH
Q_BLOCK = 128
ROPE_THETA = 10000.0
N_EXPERT_GROUPS = 4
EXPERTS_PER_GROUP = 4
N_EXPERTS = N_EXPERT_GROUPS * EXPERTS_PER_GROUP
D_EXPERT = 256
TOP_K = 2
ALPHA = (2 * DEPTH) ** 0.25
BETA = (8 * DEPTH) ** -0.25
LN_EPS = 1e-5
RMS_EPS = 1e-6
NEG_INF = -1e30

C_SSM = SSM_WIDTH
C_DQ = C_SSM + 2 * DIFF_HEADS * DIFF_HEAD_DIM
C_DK = C_DQ + 2 * DIFF_HEADS * DIFF_HEAD_DIM
C_DV = C_DK + DIFF_WIDTH
C_FQ = C_DV + FOX_WIDTH
C_FK = C_FQ + FOX_WIDTH
C_FV = C_FK + FOX_WIDTH
C_FF = C_FV + FOX_HEADS
D_IN = C_FF + N_BRANCH * D_MODEL
SPLIT_IDX = [C_SSM, C_DQ, C_DK, C_DV, C_FQ, C_FK, C_FV, C_FF]

kernel_name = 'hybrid_s5_diffattn_fox_hiermoe_deepnorm'


def layer_norm(x, g, b):
    x32 = x.astype(jnp.float32)
    mu = jnp.mean(x32, axis=-1, keepdims=True)
    var = jnp.mean(jnp.square(x32 - mu), axis=-1, keepdims=True)
    return ((x32 - mu) * lax.rsqrt(var + LN_EPS) * g.astype(jnp.float32) + b.astype(jnp.float32)).astype(x.dtype)


def apply_rope(t, cos, sin):
    t1, t2 = jnp.split(t, 2, axis=-1)
    return t * cos + jnp.concatenate([-t2, t1], axis=-1) * sin


def to_blocks(t):
    b, s = t.shape[0], t.shape[1]
    t = t.reshape((b, s // Q_BLOCK, Q_BLOCK) + t.shape[2:])
    return jnp.moveaxis(t, 1, 0)


def from_blocks(t):
    nb, b, q = t.shape[0], t.shape[1], t.shape[2]
    return jnp.moveaxis(t, 0, 1).reshape((b, nb * q) + t.shape[3:])


def ssm_combine(e1, e2):
    a1r, a1i, b1r, b1i = e1
    a2r, a2i, b2r, b2i = e2
    ar = a2r * a1r - a2i * a1i
    ai = a2r * a1i + a2i * a1r
    br = a2r * b1r - a2i * b1i + b2r
    bi = a2r * b1i + a2i * b1r + b2i
    return (ar, ai, br, bi)


def s5_branch(u, lam_re, lam_im, log_dt, b_re, b_im, c_re, c_im, d, w_glu):
    f32 = jnp.float32
    bsz, s, _ = u.shape
    u = u.astype(f32).reshape(bsz, s, SSM_GROUPS, SSM_GROUP)
    lr = lam_re.astype(f32)
    li = lam_im.astype(f32)
    dt = jnp.exp(log_dt.astype(f32))[:, None]
    mag = jnp.exp(lr * dt)
    ang = li * dt
    ar = mag * jnp.cos(ang)
    ai = mag * jnp.sin(ang)
    er = ar - 1.0
    ei = ai
    den = lr * lr + li * li
    qr = (er * lr + ei * li) / den
    qi = (ei * lr - er * li) / den
    br = b_re.astype(f32)
    bi = b_im.astype(f32)
    bbr = qr[:, :, None] * br - qi[:, :, None] * bi
    bbi = qr[:, :, None] * bi + qi[:, :, None] * br
    bur = jnp.einsum('bsgn,gpn->bsgp', u, bbr)
    bui = jnp.einsum('bsgn,gpn->bsgp', u, bbi)
    shape = bur.shape
    elems = (jnp.broadcast_to(ar, shape), jnp.broadcast_to(ai, shape), bur, bui)
    _, _, xr, xi = lax.associative_scan(ssm_combine, elems, axis=1)
    y = (jnp.einsum('gnp,bsgp->bsgn', c_re.astype(f32), xr)
         - jnp.einsum('gnp,bsgp->bsgn', c_im.astype(f32), xi)
         + d.astype(f32).reshape(SSM_GROUPS, SSM_GROUP) * u)
    y = jax.nn.gelu(y.reshape(bsz, s, SSM_WIDTH))
    val, gate = jnp.split(y @ w_glu.astype(f32), 2, axis=-1)
    return val * jax.nn.sigmoid(gate)


def diff_attention(q, k, v, lam, lam_init, norm_g, cos, sin):
    f32 = jnp.float32
    bsz, s, _ = q.shape
    q = q.reshape(bsz, s, DIFF_HEADS, 2, DIFF_HEAD_DIM)
    k = k.reshape(bsz, s, DIFF_HEADS, 2, DIFF_HEAD_DIM)
    v = v.reshape(bsz, s, DIFF_HEADS, DIFF_V_DIM)
    cs = cos[None, :, None, None, :].astype(q.dtype)
    sn = sin[None, :, None, None, :].astype(q.dtype)
    q = apply_rope(q, cs, sn)
    k = apply_rope(k, cs, sn)
    lam32 = lam.astype(f32)
    lam_full = (jnp.exp(jnp.sum(lam32[0] * lam32[1]))
                - jnp.exp(jnp.sum(lam32[2] * lam32[3])) + lam_init)
    scale = DIFF_HEAD_DIM ** -0.5
    kpos = jnp.arange(s)

    def block(args):
        qb, i = args
        sc = jnp.einsum('bqhcd,bkhcd->bhcqk', qb, k, preferred_element_type=f32) * scale
        qpos = i * Q_BLOCK + jnp.arange(Q_BLOCK)
        mask = kpos[None, :] <= qpos[:, None]
        p = jax.nn.softmax(jnp.where(mask, sc, NEG_INF), axis=-1)
        a = p[:, :, 0] - lam_full * p[:, :, 1]
        return jnp.einsum('bhqk,bkhe->bqhe', a.astype(v.dtype), v)

    o = from_blocks(lax.map(block, (to_blocks(q), jnp.arange(s // Q_BLOCK))))
    o32 = o.astype(f32)
    o32 = o32 * lax.rsqrt(jnp.mean(jnp.square(o32), axis=-1, keepdims=True) + RMS_EPS)
    o32 = o32 * norm_g.astype(f32).reshape(DIFF_HEADS, DIFF_V_DIM) * (1.0 - lam_init)
    return o32.reshape(bsz, s, DIFF_WIDTH)


def forgetting_attention(q, k, v, f_logit, f_bias):
    f32 = jnp.float32
    bsz, s, _ = q.shape
    q = q.reshape(bsz, s, FOX_HEADS, FOX_HEAD_DIM)
    k = k.reshape(bsz, s, FOX_HEADS, FOX_HEAD_DIM)
    v = v.reshape(bsz, s, FOX_HEADS, FOX_HEAD_DIM)
    logf = jax.nn.log_sigmoid(f_logit.astype(f32) + f_bias.astype(f32))
    c = jnp.cumsum(logf, axis=1)
    c_k = jnp.transpose(c, (0, 2, 1))
    scale = FOX_HEAD_DIM ** -0.5
    kpos = jnp.arange(s)

    def block(args):
        qb, cb, i = args
        sc = jnp.einsum('bqhd,bkhd->bhqk', qb, k, preferred_element_type=f32) * scale
        sc = sc + jnp.transpose(cb, (0, 2, 1))[:, :, :, None] - c_k[:, :, None, :]
        qpos = i * Q_BLOCK + jnp.arange(Q_BLOCK)
        mask = kpos[None, :] <= qpos[:, None]
        p = jax.nn.softmax(jnp.where(mask, sc, NEG_INF), axis=-1)
        return jnp.einsum('bhqk,bkhd->bqhd', p.astype(v.dtype), v)

    o = from_blocks(lax.map(block, (to_blocks(q), to_blocks(c), jnp.arange(s // Q_BLOCK))))
    return o.reshape(bsz, s, FOX_WIDTH)


def token_mixer(h, w_in, w_branch, w_out, lam_re, lam_im, log_dt, b_re, b_im, c_re, c_im, d,
                w_glu, diff_lam, diff_g, fox_b, lam_init, cos, sin):
    bsz, s, _ = h.shape
    proj = h @ w_in
    u, dq, dk, dv, fq, fk, fv, ff, gl = jnp.split(proj, SPLIT_IDX, axis=-1)
    y_ssm = s5_branch(u, lam_re, lam_im, log_dt, b_re, b_im, c_re, c_im, d, w_glu)
    y_diff = diff_attention(dq, dk, dv, diff_lam, lam_init, diff_g, cos, sin)
    y_fox = forgetting_attention(fq, fk, fv, ff, fox_b)
    branches = jnp.stack([y_ssm, y_diff, y_fox.astype(y_ssm.dtype)], axis=2)
    proj_b = jnp.einsum('bsnw,nwd->bsnd', branches, w_branch)
    gates = jax.nn.sigmoid(gl.reshape(bsz, s, N_BRANCH, D_MODEL).astype(jnp.float32))
    merged = jnp.sum(gates * proj_b, axis=2)
    return (merged @ w_out).astype(h.dtype)


def hier_moe(x, w_g, b_g, w_e, b_e, w_gate, w_up, w_down):
    f32 = jnp.float32
    bsz, s, dm = x.shape
    t = x.reshape(-1, dm)
    pg = jax.nn.softmax((t @ w_g).astype(f32) + b_g.astype(f32), axis=-1)
    gp, gi = lax.top_k(pg, 1)
    el = ((t @ w_e).astype(f32) + b_e.astype(f32)).reshape(-1, N_EXPERT_GROUPS, EXPERTS_PER_GROUP)
    el_sel = jnp.take_along_axis(el, gi[:, :, None], axis=1)[:, 0]
    ev, ei = lax.top_k(el_sel, TOP_K)
    w = gp * jax.nn.softmax(ev, axis=-1)
    eidx = gi * EXPERTS_PER_GROUP + ei
    gate = jnp.sum(jax.nn.one_hot(eidx, N_EXPERTS, dtype=f32) * w[..., None], axis=1)
    hg = jnp.einsum('td,edf->tef', t, w_gate)
    hu = jnp.einsum('td,edf->tef', t, w_up)
    hidden = jax.nn.silu(hg) * hu * gate[:, :, None].astype(hg.dtype)
    out = jnp.einsum('tef,efd->td', hidden, w_down)
    return out.reshape(bsz, s, dm).astype(x.dtype)


def setup_inputs(seed: int = 0) -> dict:
    key = jax.random.key(seed)
    ks = jax.random.split(key, 32)
    f32 = jnp.float32

    def nrm(k, shape, scale):
        return jax.random.normal(k, shape, f32) * scale

    G, P = SSM_GROUPS, SSM_STATE
    x = nrm(ks[0], (BATCH, SEQ, D_MODEL), 1.0)
    w_in = nrm(ks[1], (DEPTH, D_MODEL, D_IN), D_MODEL ** -0.5)
    w_branch = nrm(ks[2], (DEPTH, N_BRANCH, BRANCH_WIDTH, D_MODEL), BRANCH_WIDTH ** -0.5 * BETA)
    w_out = nrm(ks[3], (DEPTH, D_MODEL, D_MODEL), D_MODEL ** -0.5 * BETA)
    ssm_lambda_re = -0.5 + nrm(ks[4], (DEPTH, G, P), 0.01)
    ssm_lambda_im = math.pi * jnp.arange(P, dtype=f32) + nrm(ks[5], (DEPTH, G, P), 0.01)
    ssm_log_dt = jax.random.uniform(ks[6], (DEPTH, G), f32, math.log(DT_MIN), math.log(DT_MAX))
    ssm_b_re = nrm(ks[7], (DEPTH, G, P, SSM_GROUP), (2 * SSM_GROUP) ** -0.5)
    ssm_b_im = nrm(ks[8], (DEPTH, G, P, SSM_GROUP), (2 * SSM_GROUP) ** -0.5)
    ssm_c_re = nrm(ks[9], (DEPTH, G, SSM_GROUP, P), P ** -0.5)
    ssm_c_im = nrm(ks[10], (DEPTH, G, SSM_GROUP, P), P ** -0.5)
    ssm_d = nrm(ks[11], (DEPTH, SSM_WIDTH), 1.0)
    ssm_w_glu = nrm(ks[12], (DEPTH, SSM_WIDTH, 2 * SSM_WIDTH), SSM_WIDTH ** -0.5)
    diff_lambda = nrm(ks[13], (DEPTH, 4, DIFF_HEAD_DIM), 0.1)
    diff_norm_g = 1.0 + nrm(ks[14], (DEPTH, DIFF_WIDTH), 0.05)
    fox_f_bias = jax.random.uniform(ks[15], (DEPTH, FOX_HEADS), f32, 1.0, 4.0)
    ln1_g = 1.0 + nrm(ks[16], (DEPTH, D_MODEL), 0.05)
    ln1_b = nrm(ks[17], (DEPTH, D_MODEL), 0.02)
    moe_w_group = nrm(ks[18], (DEPTH, D_MODEL, N_EXPERT_GROUPS), D_MODEL ** -0.5)
    moe_b_group = nrm(ks[19], (DEPTH, N_EXPERT_GROUPS), 0.01)
    moe_w_expert = nrm(ks[20], (DEPTH, D_MODEL, N_EXPERTS), D_MODEL ** -0.5)
    moe_b_expert = nrm(ks[21], (DEPTH, N_EXPERTS), 0.01)
    moe_w_gate = nrm(ks[22], (DEPTH, N_EXPERTS, D_MODEL, D_EXPERT), D_MODEL ** -0.5)
    moe_w_up = nrm(ks[23], (DEPTH, N_EXPERTS, D_MODEL, D_EXPERT), D_MODEL ** -0.5)
    moe_w_down = nrm(ks[24], (DEPTH, N_EXPERTS, D_EXPERT, D_MODEL), D_EXPERT ** -0.5 * BETA)
    ln2_g = 1.0 + nrm(ks[25], (DEPTH, D_MODEL), 0.05)
    ln2_b = nrm(ks[26], (DEPTH, D_MODEL), 0.02)
    return {'x': x, 'w_in': w_in, 'w_branch': w_branch, 'w_out': w_out,
            'ssm_lambda_re': ssm_lambda_re, 'ssm_lambda_im': ssm_lambda_im, 'ssm_log_dt': ssm_log_dt,
            'ssm_b_re': ssm_b_re, 'ssm_b_im': ssm_b_im, 'ssm_c_re': ssm_c_re, 'ssm_c_im': ssm_c_im,
            'ssm_d': ssm_d, 'ssm_w_glu': ssm_w_glu, 'diff_lambda': diff_lambda, 'diff_norm_g': diff_norm_g,
            'fox_f_bias': fox_f_bias, 'ln1_g': ln1_g, 'ln1_b': ln1_b,
            'moe_w_group': moe_w_group, 'moe_b_group': moe_b_group,
            'moe_w_expert': moe_w_expert, 'moe_b_expert': moe_b_expert,
            'moe_w_gate': moe_w_gate, 'moe_w_up': moe_w_up, 'moe_w_down': moe_w_down,
            'ln2_g': ln2_g, 'ln2_b': ln2_b}


def reference(x, w_in, w_branch, w_out, ssm_lambda_re, ssm_lambda_im, ssm_log_dt, ssm_b_re, ssm_b_im,
              ssm_c_re, ssm_c_im, ssm_d, ssm_w_glu, diff_lambda, diff_norm_g, fox_f_bias, ln1_g, ln1_b,
              moe_w_group, moe_b_group, moe_w_expert, moe_b_expert, moe_w_gate, moe_w_up, moe_w_down,
              ln2_g, ln2_b):
    s = x.shape[1]
    pos = jnp.arange(s, dtype=jnp.float32)
    inv_freq = ROPE_THETA ** (-jnp.arange(0, DIFF_HEAD_DIM, 2, dtype=jnp.float32) / DIFF_HEAD_DIM)
    ang = pos[:, None] * inv_freq[None, :]
    emb = jnp.concatenate([ang, ang], axis=-1)
    cos = jnp.cos(emb)
    sin = jnp.sin(emb)
    for l in range(DEPTH):
        lam_init = 0.8 - 0.6 * math.exp(-0.3 * l)
        mix = token_mixer(x, w_in[l], w_branch[l], w_out[l], ssm_lambda_re[l], ssm_lambda_im[l],
                          ssm_log_dt[l], ssm_b_re[l], ssm_b_im[l], ssm_c_re[l], ssm_c_im[l], ssm_d[l],
                          ssm_w_glu[l], diff_lambda[l], diff_norm_g[l], fox_f_bias[l], lam_init, cos, sin)
        x = layer_norm(ALPHA * x + mix, ln1_g[l], ln1_b[l])
        ff = hier_moe(x, moe_w_group[l], moe_b_group[l], moe_w_expert[l], moe_b_expert[l],
                      moe_w_gate[l], moe_w_up[l], moe_w_down[l])
        x = layer_norm(ALPHA * x + ff, ln2_g[l], ln2_b[l])
    return x
```

```python
import functools
import math

import jax
import jax.numpy as jnp
from jax import lax
from jax.experimental import pallas as pl
from jax.experimental.pallas import tpu as pltpu

F32 = jnp.float32
BF16 = jnp.bfloat16

D_MODEL = 1024
BATCH = 8
SEQ = 2048
DEPTH = 4
TOKENS = BATCH * SEQ

SSM_WIDTH = 512
SSM_GROUP = 16
SSM_GROUPS = 32
SSM_STATE = 64
SSM_CHUNK = 16
SSM_NCHUNK = SEQ // SSM_CHUNK
SSM_ROWS = BATCH * SSM_NCHUNK
SSM_COLS = SSM_CHUNK * SSM_GROUP

DIFF_HEADS = 4
DIFF_HEAD_DIM = 64
FOX_HEADS = 8
FOX_HEAD_DIM = 64
BRANCH_WIDTH = 512
N_BRANCH = 3
ROPE_THETA = 10000.0

N_EXPERT_GROUPS = 4
EXPERTS_PER_GROUP = 4
N_EXPERTS = 16
D_EXPERT = 256

ALPHA = (2 * DEPTH) ** 0.25
LN_EPS = 1e-5
RMS_EPS = 1e-6
NEG_INF = -1e30

LANES = 128
N_SEG = 7
VMEM_LIMIT = 56 * 1024 * 1024

TM_PROJ = 512
TQ = 256
TM_POST = 512
TM_MERGE = 256
TM_ROUTE = 512
TM_MOE = 1024


def _params(*sem):
    return pltpu.CompilerParams(dimension_semantics=sem, vmem_limit_bytes=VMEM_LIMIT)


def _dot(a, b):
    return jnp.dot(a, b, preferred_element_type=F32)


def _dot_nt(a, b):
    return lax.dot_general(a, b, (((1,), (1,)), ((), ())), preferred_element_type=F32)


def _layer_norm(y, g, b):
    mu = jnp.mean(y, axis=-1, keepdims=True)
    d = y - mu
    var = jnp.mean(d * d, axis=-1, keepdims=True)
    return d * lax.rsqrt(var + LN_EPS) * g + b


def _inproj_kernel(x_ref, w_ref, wff_ref, cos_ref, sin_ref, o_ref, fft_ref):
    j = pl.program_id(1)
    xb = x_ref[...].astype(BF16)
    acc = _dot(xb, w_ref[...])
    is_rope = jnp.logical_or(j == 1, j == 2)

    @pl.when(is_rope)
    def _():
        cos = cos_ref[...]
        sin = sin_ref[...]
        lane = lax.broadcasted_iota(jnp.int32, cos.shape, 1)
        first_half = (lane & 32) == 0
        for c in range(SSM_WIDTH // LANES):
            t = acc[:, c * LANES:(c + 1) * LANES]
            rot = jnp.where(first_half, pltpu.roll(t, LANES - 32, 1), pltpu.roll(t, 32, 1))
            o_ref[:, c * LANES:(c + 1) * LANES] = (t * cos + rot * sin).astype(BF16)

    @pl.when(jnp.logical_not(is_rope))
    def _():
        o_ref[...] = acc.astype(BF16)

    @pl.when(j == 0)
    def _():
        fft_ref[...] = _dot_nt(wff_ref[...], xb)


def _inproj(x, w7, wff_t, cos, sin):
    tm = TM_PROJ
    nrope = SEQ // tm
    return pl.pallas_call(
        _inproj_kernel,
        grid=(TOKENS // tm, N_SEG),
        in_specs=[
            pl.BlockSpec((tm, D_MODEL), lambda i, j: (i, 0)),
            pl.BlockSpec((None, D_MODEL, 512), lambda i, j: (j, 0, 0)),
            pl.BlockSpec((FOX_HEADS, D_MODEL), lambda i, j: (0, 0)),
            pl.BlockSpec((tm, LANES), lambda i, j: (i % nrope, 0)),
            pl.BlockSpec((tm, LANES), lambda i, j: (i % nrope, 0)),
        ],
        out_specs=[
            pl.BlockSpec((None, tm, 512), lambda i, j: (j, i, 0)),
            pl.BlockSpec((FOX_HEADS, tm), lambda i, j: (0, i)),
        ],
        out_shape=[
            jax.ShapeDtypeStruct((N_SEG, TOKENS, 512), BF16),
            jax.ShapeDtypeStruct((FOX_HEADS, TOKENS), F32),
        ],
        compiler_params=_params("parallel", "arbitrary"),
        name="inproj",
    )(x, w7, wff_t, cos, sin)


def _fgate_kernel(fft_ref, bias_ref, ct_ref, c_ref):
    z = fft_ref[...] + bias_ref[...]
    c = jnp.minimum(z, 0.0) - jnp.log1p(jnp.exp(-jnp.abs(z)))
    lane = lax.broadcasted_iota(jnp.int32, c.shape, 1)
    shift = 1
    while shift < SEQ:
        c = c + jnp.where(lane >= shift, pltpu.roll(c, shift, 1), 0.0)
        shift *= 2
    ct_ref[...] = c
    padded = jnp.concatenate([c, jnp.zeros((LANES - FOX_HEADS, SEQ), F32)], axis=0)
    c_ref[...] = padded.T


def _fgate(fft, bias):
    return pl.pallas_call(
        _fgate_kernel,
        grid=(BATCH,),
        in_specs=[
            pl.BlockSpec((FOX_HEADS, SEQ), lambda b: (0, b)),
            pl.BlockSpec((FOX_HEADS, 1), lambda b: (0, 0)),
        ],
        out_specs=[
            pl.BlockSpec((FOX_HEADS, SEQ), lambda b: (0, b)),
            pl.BlockSpec((SEQ, LANES), lambda b: (b, 0)),
        ],
        out_shape=[
            jax.ShapeDtypeStruct((FOX_HEADS, TOKENS), F32),
            jax.ShapeDtypeStruct((TOKENS, LANES), F32),
        ],
        compiler_params=_params("parallel"),
        name="fgate",
    )(fft, bias)


def _softmax_step(s, m, l):
    m_new = jnp.maximum(m, jnp.max(s, axis=1, keepdims=True))
    p = jnp.exp(s - m_new)
    alpha = jnp.exp(m - m_new)
    l_new = alpha * l + jnp.sum(p, axis=1, keepdims=True)
    return p.astype(BF16), alpha, m_new, l_new


def _causal_mask(n):
    row = lax.broadcasted_iota(jnp.int32, (n, n), 0)
    col = lax.broadcasted_iota(jnp.int32, (n, n), 1)
    return col <= row


def _fox_kernel(q_ref, k_ref, v_ref, ct_ref, c_ref, o_ref):
    i = pl.program_id(1)
    tq = q_ref.shape[0]
    lane = lax.broadcasted_iota(jnp.int32, (tq, LANES), 1)
    lo = lane < FOX_HEAD_DIM
    mask = _causal_mask(tq)
    for hp in range(FOX_HEADS // 2):
        cols = slice(hp * LANES, (hp + 1) * LANES)
        q = q_ref[:, cols]
        zero = jnp.zeros_like(q)
        qa = jnp.where(lo, q, zero)
        qb = jnp.where(lo, zero, q)
        cqa = c_ref[:, 2 * hp:2 * hp + 1]
        cqb = c_ref[:, 2 * hp + 1:2 * hp + 2]

        def step(j, carry, masked, hp=hp, cols=cols, qa=qa, qb=qb, cqa=cqa, cqb=cqb):
            ma, la, mb, lb, acc = carry
            start = pl.multiple_of(j * tq, tq)
            kb = k_ref[pl.ds(start, tq), cols]
            vb = v_ref[pl.ds(start, tq), cols]
            cka = ct_ref[2 * hp:2 * hp + 1, pl.ds(start, tq)]
            ckb = ct_ref[2 * hp + 1:2 * hp + 2, pl.ds(start, tq)]
            sa = _dot_nt(qa, kb) + (cqa - cka)
            sb = _dot_nt(qb, kb) + (cqb - ckb)
            if masked:
                sa = jnp.where(mask, sa, NEG_INF)
                sb = jnp.where(mask, sb, NEG_INF)
            pa, alpha_a, ma, la = _softmax_step(sa, ma, la)
            pb, alpha_b, mb, lb = _softmax_step(sb, mb, lb)
            pv = jnp.where(lo, _dot(pa, vb), _dot(pb, vb))
            acc = acc * jnp.where(lo, alpha_a, alpha_b) + pv
            return ma, la, mb, lb, acc

        neg = jnp.full((tq, 1), NEG_INF, F32)
        zcol = jnp.zeros((tq, 1), F32)
        carry = (neg, zcol, neg, zcol, jnp.zeros((tq, LANES), F32))
        carry = lax.fori_loop(0, i, functools.partial(step, masked=False), carry)
        ma, la, mb, lb, acc = step(i, carry, True)
        o_ref[:, cols] = (acc / jnp.where(lo, la, lb)).astype(BF16)


def _fox(proj, ct, c):
    nq = SEQ // TQ
    return pl.pallas_call(
        _fox_kernel,
        grid=(BATCH, nq),
        in_specs=[
            pl.BlockSpec((None, TQ, 512), lambda b, i: (4, b * nq + i, 0)),
            pl.BlockSpec((None, SEQ, 512), lambda b, i: (5, b, 0)),
            pl.BlockSpec((None, SEQ, 512), lambda b, i: (6, b, 0)),
            pl.BlockSpec((FOX_HEADS, SEQ), lambda b, i: (0, b)),
            pl.BlockSpec((TQ, LANES), lambda b, i: (b * nq + i, 0)),
        ],
        out_specs=pl.BlockSpec((TQ, 512), lambda b, i: (b * nq + i, 0)),
        out_shape=jax.ShapeDtypeStruct((TOKENS, 512), BF16),
        compiler_params=_params("parallel", "arbitrary"),
        name="fox_attn",
    )(proj, proj, proj, ct, c)


def _diff_kernel(lam_init, q_ref, k_ref, v_ref, lam_ref, g_ref, o_ref):
    i = pl.program_id(1)
    tq = q_ref.shape[0]
    lane = lax.broadcasted_iota(jnp.int32, (tq, LANES), 1)
    lo = lane < DIFF_HEAD_DIM
    mask = _causal_mask(tq)
    lam_rows = lam_ref[...]
    lam = (jnp.exp(jnp.sum(lam_rows[0:1] * lam_rows[1:2], axis=1, keepdims=True))
           - jnp.exp(jnp.sum(lam_rows[2:3] * lam_rows[3:4], axis=1, keepdims=True)) + lam_init)
    for h in range(DIFF_HEADS):
        cols = slice(h * LANES, (h + 1) * LANES)
        q = q_ref[:, cols]
        zero = jnp.zeros_like(q)
        q1 = jnp.where(lo, q, zero)
        q2 = jnp.where(lo, zero, q)

        def step(j, carry, masked, cols=cols, q1=q1, q2=q2):
            m1, l1, m2, l2, acc1, acc2 = carry
            start = pl.multiple_of(j * tq, tq)
            kb = k_ref[pl.ds(start, tq), cols]
            vb = v_ref[pl.ds(start, tq), cols]
            s1 = _dot_nt(q1, kb)
            s2 = _dot_nt(q2, kb)
            if masked:
                s1 = jnp.where(mask, s1, NEG_INF)
                s2 = jnp.where(mask, s2, NEG_INF)
            p1, alpha1, m1, l1 = _softmax_step(s1, m1, l1)
            p2, alpha2, m2, l2 = _softmax_step(s2, m2, l2)
            acc1 = acc1 * alpha1 + _dot(p1, vb)
            acc2 = acc2 * alpha2 + _dot(p2, vb)
            return m1, l1, m2, l2, acc1, acc2

        neg = jnp.full((tq, 1), NEG_INF, F32)
        zcol = jnp.zeros((tq, 1), F32)
        zacc = jnp.zeros((tq, LANES), F32)
        carry = (neg, zcol, neg, zcol, zacc, zacc)
        carry = lax.fori_loop(0, i, functools.partial(step, masked=False), carry)
        m1, l1, m2, l2, acc1, acc2 = step(i, carry, True)
        o = acc1 / l1 - lam * (acc2 / l2)
        o = o * lax.rsqrt(jnp.mean(o * o, axis=1, keepdims=True) + RMS_EPS)
        o_ref[:, cols] = (o * g_ref[:, cols] * (1.0 - lam_init)).astype(BF16)


def _diff(proj, lam, g, lam_init):
    nq = SEQ // TQ
    return pl.pallas_call(
        functools.partial(_diff_kernel, lam_init),
        grid=(BATCH, nq),
        in_specs=[
            pl.BlockSpec((None, TQ, 512), lambda b, i: (1, b * nq + i, 0)),
            pl.BlockSpec((None, SEQ, 512), lambda b, i: (2, b, 0)),
            pl.BlockSpec((None, SEQ, 512), lambda b, i: (3, b, 0)),
            pl.BlockSpec((4, DIFF_HEAD_DIM), lambda b, i: (0, 0)),
            pl.BlockSpec((1, 512), lambda b, i: (0, 0)),
        ],
        out_specs=pl.BlockSpec((TQ, 512), lambda b, i: (b * nq + i, 0)),
        out_shape=jax.ShapeDtypeStruct((TOKENS, 512), BF16),
        compiler_params=_params("parallel", "arbitrary"),
        name="diff_attn",
    )(proj, proj, proj, lam, g)


def _s5_kernel(u_ref, m_ref, bre_ref, bim_ref, cre_ref, cim_ref, are_ref, aim_ref, y_ref,
               sre, sim, xre, xim):
    u = u_ref[...]
    sre[...] = _dot(u, bre_ref[...])
    sim[...] = _dot(u, bim_ref[...])
    ar = are_ref[...]
    ai = aim_ref[...]

    def body(c, carry):
        xr, xi = carry
        rows = pl.ds(c, BATCH, stride=SSM_NCHUNK)
        xre[rows, :] = xr
        xim[rows, :] = xi
        sr = sre[rows, :]
        si = sim[rows, :]
        return ar * xr - ai * xi + sr, ar * xi + ai * xr + si

    zero = jnp.zeros((BATCH, SSM_STATE), F32)
    lax.fori_loop(0, SSM_NCHUNK, body, (zero, zero))
    y_ref[...] = (_dot(u, m_ref[...])
                  + _dot(xre[...].astype(BF16), cre_ref[...])
                  + _dot(xim[...].astype(BF16), cim_ref[...]))


def _s5_scan(u_g, mats):
    m, bre, bim, cre, cim, are, aim = mats

    def spec(*shape):
        return pl.BlockSpec((None,) + shape, lambda g: (g,) + (0,) * len(shape))

    return pl.pallas_call(
        _s5_kernel,
        grid=(SSM_GROUPS,),
        in_specs=[spec(SSM_ROWS, SSM_COLS), spec(SSM_COLS, SSM_COLS),
                  spec(SSM_COLS, SSM_STATE), spec(SSM_COLS, SSM_STATE),
                  spec(SSM_STATE, SSM_COLS), spec(SSM_STATE, SSM_COLS),
                  spec(1, SSM_STATE), spec(1, SSM_STATE)],
        out_specs=spec(SSM_ROWS, SSM_COLS),
        out_shape=jax.ShapeDtypeStruct((SSM_GROUPS, SSM_ROWS, SSM_COLS), F32),
        scratch_shapes=[pltpu.VMEM((SSM_ROWS, SSM_STATE), F32) for _ in range(4)],
        compiler_params=_params("parallel"),
        name="s5_scan",
    )(u_g, m, bre, bim, cre, cim, are, aim)


def _s5_matrices(lam_re, lam_im, log_dt, b_re, b_im, c_re, c_im):
    hi = lax.Precision.HIGHEST
    L = SSM_CHUNK
    lr = lam_re.astype(F32)
    li = lam_im.astype(F32)
    dt = jnp.exp(log_dt.astype(F32))[:, None]
    mag = jnp.exp(lr * dt)
    ang = li * dt
    ar = mag * jnp.cos(ang)
    ai = mag * jnp.sin(ang)
    er = ar - 1.0
    ei = ai
    den = lr * lr + li * li
    qr = (er * lr + ei * li) / den
    qi = (ei * lr - er * li) / den
    br = b_re.astype(F32)
    bi = b_im.astype(F32)
    bbr = qr[:, :, None] * br - qi[:, :, None] * bi
    bbi = qr[:, :, None] * bi + qi[:, :, None] * br
    k = jnp.arange(L + 1, dtype=F32)[:, None, None]
    pmag = jnp.exp(lr[None] * dt[None] * k)
    pr = pmag * jnp.cos(ang[None] * k)
    pi = pmag * jnp.sin(ang[None] * k)
    cr = c_re.astype(F32)
    ci = c_im.astype(F32)
    e_r = cr[None] * pr[:, :, None, :] - ci[None] * pi[:, :, None, :]
    e_i = cr[None] * pi[:, :, None, :] + ci[None] * pr[:, :, None, :]
    kern = (jnp.einsum('kgnp,gpm->kgnm', e_r[:L], bbr, precision=hi)
            - jnp.einsum('kgnp,gpm->kgnm', e_i[:L], bbi, precision=hi))
    lag = jnp.arange(L)[None, :] - jnp.arange(L)[:, None]
    toep = jnp.where((lag >= 0)[:, :, None, None, None], kern[jnp.clip(lag, 0, L - 1)], 0.0)
    m = jnp.transpose(toep, (2, 0, 4, 1, 3)).reshape(SSM_GROUPS, SSM_COLS, SSM_COLS)
    rev_r = pr[:L][::-1]
    rev_i = pi[:L][::-1]
    sb_r = rev_r[:, :, :, None] * bbr[None] - rev_i[:, :, :, None] * bbi[None]
    sb_i = rev_r[:, :, :, None] * bbi[None] + rev_i[:, :, :, None] * bbr[None]
    bre = jnp.transpose(sb_r, (1, 0, 3, 2)).reshape(SSM_GROUPS, SSM_COLS, SSM_STATE)
    bim = jnp.transpose(sb_i, (1, 0, 3, 2)).reshape(SSM_GROUPS, SSM_COLS, SSM_STATE)
    cre = jnp.transpose(e_r[1:], (1, 3, 0, 2)).reshape(SSM_GROUPS, SSM_STATE, SSM_COLS)
    cim = -jnp.transpose(e_i[1:], (1, 3, 0, 2)).reshape(SSM_GROUPS, SSM_STATE, SSM_COLS)
    are = pr[L][:, None, :]
    aim = pi[L][:, None, :]
    return (m.astype(BF16), bre.astype(BF16), bim.astype(BF16), cre.astype(BF16), cim.astype(BF16),
            are, aim)


def _s5_post_kernel(y_ref, u_ref, d_ref, w_ref, o_ref):
    y = y_ref[...] + d_ref[...] * u_ref[...].astype(F32)
    y = jax.nn.gelu(y, approximate=True)
    z = _dot(y.astype(BF16), w_ref[...])
    o_ref[...] = (z[:, :SSM_WIDTH] * jax.nn.sigmoid(z[:, SSM_WIDTH:])).astype(BF16)


def _s5_post(y, proj, d, w_glu):
    tm = TM_POST
    return pl.pallas_call(
        _s5_post_kernel,
        grid=(TOKENS // tm,),
        in_specs=[
            pl.BlockSpec((tm, 512), lambda i: (i, 0)),
            pl.BlockSpec((None, tm, 512), lambda i: (0, i, 0)),
            pl.BlockSpec((1, 512), lambda i: (0, 0)),
            pl.BlockSpec((512, 1024), lambda i: (0, 0)),
        ],
        out_specs=pl.BlockSpec((tm, 512), lambda i: (i, 0)),
        out_shape=jax.ShapeDtypeStruct((TOKENS, 512), BF16),
        compiler_params=_params("parallel"),
        name="s5_post",
    )(y, proj, d, w_glu)


def _merge_kernel(x_ref, ys_ref, yd_ref, yf_ref, wgl_ref, wbr_ref, wout_ref, g_ref, b_ref, o_ref):
    x = x_ref[...]
    xb = x.astype(BF16)
    merged = None
    for n, y_ref in enumerate((ys_ref, yd_ref, yf_ref)):
        gate = jax.nn.sigmoid(_dot(xb, wgl_ref[:, n * D_MODEL:(n + 1) * D_MODEL]))
        term = gate * _dot(y_ref[...], wbr_ref[n])
        merged = term if merged is None else merged + term
    mix = _dot(merged.astype(BF16), wout_ref[...])
    o_ref[...] = _layer_norm(ALPHA * x + mix, g_ref[...], b_ref[...])


def _merge(x, ys, yd, yf, wgl, wbr, wout, g, b):
    tm = TM_MERGE
    row = lambda i: (i, 0)
    const2 = lambda i: (0, 0)
    return pl.pallas_call(
        _merge_kernel,
        grid=(TOKENS // tm,),
        in_specs=[
            pl.BlockSpec((tm, D_MODEL), row),
            pl.BlockSpec((tm, 512), row),
            pl.BlockSpec((tm, 512), row),
            pl.BlockSpec((tm, 512), row),
            pl.BlockSpec((D_MODEL, N_BRANCH * D_MODEL), const2),
            pl.BlockSpec((N_BRANCH, BRANCH_WIDTH, D_MODEL), lambda i: (0, 0, 0)),
            pl.BlockSpec((D_MODEL, D_MODEL), const2),
            pl.BlockSpec((1, D_MODEL), const2),
            pl.BlockSpec((1, D_MODEL), const2),
        ],
        out_specs=pl.BlockSpec((tm, D_MODEL), row),
        out_shape=jax.ShapeDtypeStruct((TOKENS, D_MODEL), F32),
        compiler_params=_params("parallel"),
        name="merge_ln",
    )(x, ys, yd, yf, wgl, wbr, wout, g, b)


def _router_kernel(x_ref, w_ref, b_ref, o_ref):
    logits = jnp.dot(x_ref[...], w_ref[...], preferred_element_type=F32,
                     precision=lax.Precision.HIGHEST) + b_ref[...]
    lane = lax.broadcasted_iota(jnp.int32, logits.shape, 1).astype(F32)
    big = jnp.float32(LANES)
    neg = jnp.float32(-jnp.inf)

    def first_argmax(v, vmax):
        return jnp.min(jnp.where(v == vmax, lane, big), axis=1, keepdims=True)

    gl = jnp.where(lane < N_EXPERT_GROUPS, logits, neg)
    ge = jnp.exp(gl - jnp.max(gl, axis=1, keepdims=True))
    pg = ge / jnp.sum(ge, axis=1, keepdims=True)
    gp = jnp.max(pg, axis=1, keepdims=True)
    gi = first_argmax(pg, gp)
    base = N_EXPERT_GROUPS + EXPERTS_PER_GROUP * gi
    sel = jnp.logical_and(lane >= base, lane < base + EXPERTS_PER_GROUP)
    el = jnp.where(sel, logits, neg)
    ev1 = jnp.max(el, axis=1, keepdims=True)
    i1 = first_argmax(el, ev1)
    el2 = jnp.where(lane == i1, neg, el)
    ev2 = jnp.max(el2, axis=1, keepdims=True)
    i2 = first_argmax(el2, ev2)
    e2 = jnp.exp(ev2 - ev1)
    den = 1.0 + e2
    w1 = gp * (1.0 / den)
    w2 = gp * (e2 / den)
    o_ref[...] = jnp.where(lane == i1, w1, 0.0) + jnp.where(lane == i2, w2, 0.0)


def _router(x, w, b):
    tm = TM_ROUTE
    return pl.pallas_call(
        _router_kernel,
        grid=(TOKENS // tm,),
        in_specs=[
            pl.BlockSpec((tm, D_MODEL), lambda i: (i, 0)),
            pl.BlockSpec((D_MODEL, LANES), lambda i: (0, 0)),
            pl.BlockSpec((1, LANES), lambda i: (0, 0)),
        ],
        out_specs=pl.BlockSpec((tm, LANES), lambda i: (i, 0)),
        out_shape=jax.ShapeDtypeStruct((TOKENS, LANES), F32),
        compiler_params=_params("parallel"),
        name="router",
    )(x, w, b)


def _moe_kernel(x_ref, gate_ref, wg_ref, wu_ref, wd_ref, g_ref, b_ref, o_ref, acc_ref):
    e = pl.program_id(1)
    xb = x_ref[...].astype(BF16)
    gates = gate_ref[...]
    lane = lax.broadcasted_iota(jnp.int32, gates.shape, 1)
    gate = jnp.sum(jnp.where(lane == e + N_EXPERT_GROUPS, gates, 0.0), axis=1, keepdims=True)
    hidden = jax.nn.silu(_dot(xb, wg_ref[...])) * _dot(xb, wu_ref[...]) * gate
    contrib = _dot(hidden.astype(BF16), wd_ref[...])

    @pl.when(e == 0)
    def _():
        acc_ref[...] = contrib

    @pl.when(e > 0)
    def _():
        acc_ref[...] += contrib

    @pl.when(e == N_EXPERTS - 1)
    def _():
        o_ref[...] = _layer_norm(ALPHA * x_ref[...] + acc_ref[...], g_ref[...], b_ref[...])


def _moe(x, gates, wg, wu, wd, g, b):
    tm = TM_MOE
    return pl.pallas_call(
        _moe_kernel,
        grid=(TOKENS // tm, N_EXPERTS),
        in_specs=[
            pl.BlockSpec((tm, D_MODEL), lambda i, e: (i, 0)),
            pl.BlockSpec((tm, LANES), lambda i, e: (i, 0)),
            pl.BlockSpec((None, D_MODEL, D_EXPERT), lambda i, e: (e, 0, 0)),
            pl.BlockSpec((None, D_MODEL, D_EXPERT), lambda i, e: (e, 0, 0)),
            pl.BlockSpec((None, D_EXPERT, D_MODEL), lambda i, e: (e, 0, 0)),
            pl.BlockSpec((1, D_MODEL), lambda i, e: (0, 0)),
            pl.BlockSpec((1, D_MODEL), lambda i, e: (0, 0)),
        ],
        out_specs=pl.BlockSpec((tm, D_MODEL), lambda i, e: (i, 0)),
        out_shape=jax.ShapeDtypeStruct((TOKENS, D_MODEL), F32),
        scratch_shapes=[pltpu.VMEM((tm, D_MODEL), F32)],
        compiler_params=_params("parallel", "arbitrary"),
        name="moe_ln",
    )(x, gates, wg, wu, wd, g, b)


def _rope_tables():
    pos = jnp.arange(SEQ, dtype=F32)
    inv_freq = ROPE_THETA ** (-jnp.arange(0, DIFF_HEAD_DIM, 2, dtype=F32) / DIFF_HEAD_DIM)
    ang = pos[:, None] * inv_freq[None, :]
    emb = jnp.concatenate([ang, ang], axis=-1)
    cos = jnp.cos(emb)
    sin = jnp.sin(emb)
    sign = jnp.where(jnp.arange(DIFF_HEAD_DIM) < DIFF_HEAD_DIM // 2, -1.0, 1.0).astype(F32)
    return jnp.tile(cos, (1, 2)), jnp.tile(sin * sign, (1, 2))


def _split_w_in(w_in_l):
    scale = DIFF_HEAD_DIM ** -0.5
    segs = []
    for s in range(N_SEG):
        w = w_in_l[:, s * 512:(s + 1) * 512]
        if s in (1, 4):
            w = w * scale
        segs.append(w)
    w7 = jnp.stack(segs).astype(BF16)
    off = N_SEG * 512
    wff_t = w_in_l[:, off:off + FOX_HEADS].T.astype(BF16)
    wgl = w_in_l[:, off + FOX_HEADS:].astype(BF16)
    return w7, wff_t, wgl


def kernel(x, w_in, w_branch, w_out, ssm_lambda_re, ssm_lambda_im, ssm_log_dt, ssm_b_re, ssm_b_im,
           ssm_c_re, ssm_c_im, ssm_d, ssm_w_glu, diff_lambda, diff_norm_g, fox_f_bias, ln1_g, ln1_b,
           moe_w_group, moe_b_group, moe_w_expert, moe_b_expert, moe_w_gate, moe_w_up, moe_w_down,
           ln2_g, ln2_b):
    cos, sin = _rope_tables()
    h = x.reshape(TOKENS, D_MODEL)
    for l in range(DEPTH):
        lam_init = 0.8 - 0.6 * math.exp(-0.3 * l)
        w7, wff_t, wgl = _split_w_in(w_in[l])
        proj, fft = _inproj(h, w7, wff_t, cos, sin)

        mats = _s5_matrices(ssm_lambda_re[l], ssm_lambda_im[l], ssm_log_dt[l], ssm_b_re[l],
                            ssm_b_im[l], ssm_c_re[l], ssm_c_im[l])
        u_g = proj[0].reshape(BATCH, SSM_NCHUNK, SSM_CHUNK, SSM_GROUPS, SSM_GROUP)
        u_g = jnp.transpose(u_g, (3, 0, 1, 2, 4)).reshape(SSM_GROUPS, SSM_ROWS, SSM_COLS)
        y_g = _s5_scan(u_g, mats)
        y_g = y_g.reshape(SSM_GROUPS, BATCH, SSM_NCHUNK, SSM_CHUNK, SSM_GROUP)
        y_scan = jnp.transpose(y_g, (1, 2, 3, 0, 4)).reshape(TOKENS, SSM_WIDTH)
        y_ssm = _s5_post(y_scan, proj, ssm_d[l].reshape(1, SSM_WIDTH).astype(F32),
                         ssm_w_glu[l].astype(BF16))

        y_diff = _diff(proj, diff_lambda[l].astype(F32), diff_norm_g[l].reshape(1, 512).astype(F32),
                       lam_init)
        ct, c = _fgate(fft, fox_f_bias[l].reshape(FOX_HEADS, 1).astype(F32))
        y_fox = _fox(proj, ct, c)

        h = _merge(h, y_ssm, y_diff, y_fox, wgl, w_branch[l].astype(BF16), w_out[l].astype(BF16),
                   ln1_g[l].reshape(1, D_MODEL), ln1_b[l].reshape(1, D_MODEL))

        w_r = jnp.concatenate([moe_w_group[l], moe_w_expert[l]], axis=1).astype(F32)
        w_r = jnp.pad(w_r, ((0, 0), (0, LANES - w_r.shape[1])))
        b_r = jnp.concatenate([moe_b_group[l], moe_b_expert[l]]).astype(F32)
        b_r = jnp.pad(b_r, (0, LANES - b_r.shape[0])).reshape(1, LANES)
        gates = _router(h, w_r, b_r)
        h = _moe(h, gates, moe_w_gate[l].astype(BF16), moe_w_up[l].astype(BF16),
                 moe_w_down[l].astype(BF16), ln2_g[l].reshape(1, D_MODEL), ln2_b[l].reshape(1, D_MODEL))
    return h.reshape(BATCH, SEQ, D_MODEL)
```

```python
import functools
import math

import jax
import jax.numpy as jnp
from jax import lax
from jax.experimental import pallas as pl
from jax.experimental.pallas import tpu as pltpu

F32 = jnp.float32
BF16 = jnp.bfloat16

D_MODEL = 1024
BATCH = 8
SEQ = 2048
DEPTH = 4
TOKENS = BATCH * SEQ

SSM_WIDTH = 512
SSM_GROUP = 16
SSM_GROUPS = 32
SSM_STATE = 64
SSM_CHUNK = 8
SSM_NCHUNK = SEQ // SSM_CHUNK
SSM_ROWS = BATCH * SSM_NCHUNK
SSM_GBLK = 8
SSM_NGBLK = SSM_GROUPS // SSM_GBLK
SSM_COLS = SSM_CHUNK * 128
SSM_SCOLS = SSM_GBLK * SSM_STATE

DIFF_HEADS = 4
DIFF_HEAD_DIM = 64
FOX_HEADS = 8
FOX_HEAD_DIM = 64
BRANCH_WIDTH = 512
N_BRANCH = 3
ROPE_THETA = 10000.0

N_EXPERT_GROUPS = 4
EXPERTS_PER_GROUP = 4
N_EXPERTS = 16
D_EXPERT = 256

ALPHA = (2 * DEPTH) ** 0.25
LN_EPS = 1e-5
RMS_EPS = 1e-6
NEG_INF = -1e30
LOG2E = math.log2(math.e)

LANES = 128
N_SEG = 6
SEG_DQ, SEG_DK, SEG_DV, SEG_FQ, SEG_FK, SEG_FV = range(N_SEG)
VMEM_LIMIT = 56 * 1024 * 1024

TM_PROJ = 512
TQ = 256
TM_POST = 512
TM_MERGE = 256
TM_ROUTE = 512
TM_MOE = 1024


def _params(*sem):
    return pltpu.CompilerParams(dimension_semantics=sem, vmem_limit_bytes=VMEM_LIMIT)


def _dot(a, b):
    return jnp.dot(a, b, preferred_element_type=F32)


def _dot_nt(a, b):
    return lax.dot_general(a, b, (((1,), (1,)), ((), ())), preferred_element_type=F32)


def _layer_norm(y, g, b):
    mu = jnp.mean(y, axis=-1, keepdims=True)
    d = y - mu
    var = jnp.mean(d * d, axis=-1, keepdims=True)
    return d * lax.rsqrt(var + LN_EPS) * g + b


def _inproj_kernel(x_ref, w_ref, wff_ref, cos_ref, sin_ref, o_ref, fft_ref):
    j = pl.program_id(1)
    xb = x_ref[...].astype(BF16)
    acc = _dot(xb, w_ref[...])
    is_rope = jnp.logical_or(j == SEG_DQ, j == SEG_DK)

    @pl.when(is_rope)
    def _():
        cos = cos_ref[...]
        sin = sin_ref[...]
        lane = lax.broadcasted_iota(jnp.int32, cos.shape, 1)
        first_half = (lane & 32) == 0
        for c in range(512 // LANES):
            t = acc[:, c * LANES:(c + 1) * LANES]
            rot = jnp.where(first_half, pltpu.roll(t, LANES - 32, 1), pltpu.roll(t, 32, 1))
            o_ref[:, c * LANES:(c + 1) * LANES] = (t * cos + rot * sin).astype(BF16)

    @pl.when(jnp.logical_not(is_rope))
    def _():
        o_ref[...] = acc.astype(BF16)

    @pl.when(j == 0)
    def _():
        fft_ref[...] = _dot_nt(wff_ref[...], xb)


def _inproj(x, w6, wff_t, cos, sin):
    tm = TM_PROJ
    nrope = SEQ // tm
    return pl.pallas_call(
        _inproj_kernel,
        grid=(TOKENS // tm, N_SEG),
        in_specs=[
            pl.BlockSpec((tm, D_MODEL), lambda i, j: (i, 0)),
            pl.BlockSpec((None, D_MODEL, 512), lambda i, j: (j, 0, 0)),
            pl.BlockSpec((FOX_HEADS, D_MODEL), lambda i, j: (0, 0)),
            pl.BlockSpec((tm, LANES), lambda i, j: (i % nrope, 0)),
            pl.BlockSpec((tm, LANES), lambda i, j: (i % nrope, 0)),
        ],
        out_specs=[
            pl.BlockSpec((None, tm, 512), lambda i, j: (j, i, 0)),
            pl.BlockSpec((FOX_HEADS, tm), lambda i, j: (0, i)),
        ],
        out_shape=[
            jax.ShapeDtypeStruct((N_SEG, TOKENS, 512), BF16),
            jax.ShapeDtypeStruct((FOX_HEADS, TOKENS), F32),
        ],
        compiler_params=_params("parallel", "arbitrary"),
        name="inproj",
    )(x, w6, wff_t, cos, sin)


def _inproj_u_kernel(x_ref, w_ref, o_ref):
    o_ref[...] = _dot(x_ref[...].astype(BF16), w_ref[...]).astype(BF16)


def _inproj_u(x_rows, w_u):
    tm = TM_PROJ
    return pl.pallas_call(
        _inproj_u_kernel,
        grid=(SSM_ROWS // tm, SSM_CHUNK),
        in_specs=[
            pl.BlockSpec((tm, D_MODEL), lambda i, s: (i, s)),
            pl.BlockSpec((D_MODEL, SSM_WIDTH), lambda i, s: (0, 0)),
        ],
        out_specs=pl.BlockSpec((None, tm, SSM_WIDTH), lambda i, s: (s, i, 0)),
        out_shape=jax.ShapeDtypeStruct((SSM_CHUNK, SSM_ROWS, SSM_WIDTH), BF16),
        compiler_params=_params("parallel", "parallel"),
        name="inproj_u",
    )(x_rows, w_u)


def _fgate_kernel(fft_ref, bias_ref, ct_ref, c_ref):
    z = fft_ref[...] + bias_ref[...]
    c = jnp.minimum(z, 0.0) - jnp.log1p(jnp.exp(-jnp.abs(z)))
    lane = lax.broadcasted_iota(jnp.int32, c.shape, 1)
    shift = 1
    while shift < SEQ:
        c = c + jnp.where(lane >= shift, pltpu.roll(c, shift, 1), 0.0)
        shift *= 2
    c = c * LOG2E
    ct_ref[...] = c
    padded = jnp.concatenate([c, jnp.zeros((LANES - FOX_HEADS, SEQ), F32)], axis=0)
    c_ref[...] = padded.T


def _fgate(fft, bias):
    return pl.pallas_call(
        _fgate_kernel,
        grid=(BATCH,),
        in_specs=[
            pl.BlockSpec((FOX_HEADS, SEQ), lambda b: (0, b)),
            pl.BlockSpec((FOX_HEADS, 1), lambda b: (0, 0)),
        ],
        out_specs=[
            pl.BlockSpec((FOX_HEADS, SEQ), lambda b: (0, b)),
            pl.BlockSpec((SEQ, LANES), lambda b: (b, 0)),
        ],
        out_shape=[
            jax.ShapeDtypeStruct((FOX_HEADS, TOKENS), F32),
            jax.ShapeDtypeStruct((TOKENS, LANES), F32),
        ],
        compiler_params=_params("parallel"),
        name="fgate",
    )(fft, bias)


def _online_softmax(s, m_ref, l_ref, idx):
    m_old = m_ref[idx]
    m_new = jnp.maximum(m_old, jnp.max(s, axis=1, keepdims=True))
    alpha = jnp.exp2(m_old - m_new)
    parts = [jnp.exp2(s[:, c * LANES:(c + 1) * LANES] - m_new) for c in range(s.shape[1] // LANES)]
    l_ref[idx] = alpha * l_ref[idx] + functools.reduce(lambda a, b: a + b, parts)
    m_ref[idx] = m_new
    return jnp.concatenate(parts, axis=1).astype(BF16), alpha


def _causal_mask(n):
    row = lax.broadcasted_iota(jnp.int32, (n, n), 0)
    col = lax.broadcasted_iota(jnp.int32, (n, n), 1)
    return col <= row


def _init_softmax_state(m_ref, l_ref, acc_ref):
    m_ref[...] = jnp.full(m_ref.shape, NEG_INF, F32)
    l_ref[...] = jnp.zeros(l_ref.shape, F32)
    acc_ref[...] = jnp.zeros(acc_ref.shape, F32)


def _fox_kernel(q_ref, k_ref, v_ref, ct_ref, c_ref, o_ref, m_ref, l_ref, acc_ref, cq_ref):
    i = pl.program_id(1)
    tq = q_ref.shape[0]
    lane = lax.broadcasted_iota(jnp.int32, (tq, LANES), 1)
    lo = lane < FOX_HEAD_DIM
    _init_softmax_state(m_ref, l_ref, acc_ref)
    for h in range(FOX_HEADS):
        cq_ref[h] = jnp.broadcast_to(c_ref[:, h:h + 1], (tq, LANES))

    def block(j, masked):
        start = pl.multiple_of(j * tq, tq)
        for hp in range(FOX_HEADS // 2):
            cols = slice(hp * LANES, (hp + 1) * LANES)
            q = q_ref[:, cols]
            zero = jnp.zeros_like(q)
            kb = k_ref[pl.ds(start, tq), cols]
            vb = v_ref[pl.ds(start, tq), cols]
            pv, alpha = [], []
            for h, qh in ((2 * hp, jnp.where(lo, q, zero)), (2 * hp + 1, jnp.where(lo, zero, q))):
                cq = cq_ref[h]
                ck = ct_ref[h:h + 1, pl.ds(start, tq)]
                raw = _dot_nt(qh, kb)
                s = jnp.concatenate(
                    [raw[:, c * LANES:(c + 1) * LANES] + (cq - ck[:, c * LANES:(c + 1) * LANES])
                     for c in range(tq // LANES)], axis=1)
                if masked:
                    s = jnp.where(_causal_mask(tq), s, NEG_INF)
                p, a = _online_softmax(s, m_ref, l_ref, h)
                pv.append(_dot(p, vb))
                alpha.append(a)
            acc_ref[hp] = acc_ref[hp] * jnp.where(lo, alpha[0], alpha[1]) + jnp.where(lo, pv[0], pv[1])

    def full_block(j, carry):
        block(j, False)
        return carry

    lax.fori_loop(0, i, full_block, 0)
    block(i, True)
    for hp in range(FOX_HEADS // 2):
        la = jnp.sum(l_ref[2 * hp], axis=1, keepdims=True)
        lb = jnp.sum(l_ref[2 * hp + 1], axis=1, keepdims=True)
        o_ref[:, hp * LANES:(hp + 1) * LANES] = (acc_ref[hp] / jnp.where(lo, la, lb)).astype(BF16)


def _fox(proj, ct, c):
    nq = SEQ // TQ
    return pl.pallas_call(
        _fox_kernel,
        grid=(BATCH, nq),
        in_specs=[
            pl.BlockSpec((None, TQ, 512), lambda b, i: (SEG_FQ, b * nq + i, 0)),
            pl.BlockSpec((None, SEQ, 512), lambda b, i: (SEG_FK, b, 0)),
            pl.BlockSpec((None, SEQ, 512), lambda b, i: (SEG_FV, b, 0)),
            pl.BlockSpec((FOX_HEADS, SEQ), lambda b, i: (0, b)),
            pl.BlockSpec((TQ, LANES), lambda b, i: (b * nq + i, 0)),
        ],
        out_specs=pl.BlockSpec((TQ, 512), lambda b, i: (b * nq + i, 0)),
        out_shape=jax.ShapeDtypeStruct((TOKENS, 512), BF16),
        scratch_shapes=[pltpu.VMEM((FOX_HEADS, TQ, LANES), F32),
                        pltpu.VMEM((FOX_HEADS, TQ, LANES), F32),
                        pltpu.VMEM((FOX_HEADS // 2, TQ, LANES), F32),
                        pltpu.VMEM((FOX_HEADS, TQ, LANES), F32)],
        compiler_params=_params("parallel", "arbitrary"),
        name="fox_attn",
    )(proj, proj, proj, ct, c)


def _diff_kernel(lam_init, q_ref, k_ref, v_ref, lam_ref, g_ref, o_ref, m_ref, l_ref, acc_ref):
    i = pl.program_id(1)
    tq = q_ref.shape[0]
    lane = lax.broadcasted_iota(jnp.int32, (tq, LANES), 1)
    lo = lane < DIFF_HEAD_DIM
    _init_softmax_state(m_ref, l_ref, acc_ref)

    def block(j, masked):
        start = pl.multiple_of(j * tq, tq)
        for h in range(DIFF_HEADS):
            cols = slice(h * LANES, (h + 1) * LANES)
            q = q_ref[:, cols]
            zero = jnp.zeros_like(q)
            kb = k_ref[pl.ds(start, tq), cols]
            vb = v_ref[pl.ds(start, tq), cols]
            for idx, qm in ((2 * h, jnp.where(lo, q, zero)), (2 * h + 1, jnp.where(lo, zero, q))):
                s = _dot_nt(qm, kb)
                if masked:
                    s = jnp.where(_causal_mask(tq), s, NEG_INF)
                p, a = _online_softmax(s, m_ref, l_ref, idx)
                acc_ref[idx] = acc_ref[idx] * a + _dot(p, vb)

    def full_block(j, carry):
        block(j, False)
        return carry

    lax.fori_loop(0, i, full_block, 0)
    block(i, True)
    lam_rows = lam_ref[...]
    lam = (jnp.exp(jnp.sum(lam_rows[0:1] * lam_rows[1:2], axis=1, keepdims=True))
           - jnp.exp(jnp.sum(lam_rows[2:3] * lam_rows[3:4], axis=1, keepdims=True)) + lam_init)
    for h in range(DIFF_HEADS):
        cols = slice(h * LANES, (h + 1) * LANES)
        l1 = jnp.sum(l_ref[2 * h], axis=1, keepdims=True)
        l2 = jnp.sum(l_ref[2 * h + 1], axis=1, keepdims=True)
        o = acc_ref[2 * h] / l1 - lam * (acc_ref[2 * h + 1] / l2)
        o = o * lax.rsqrt(jnp.mean(o * o, axis=1, keepdims=True) + RMS_EPS)
        o_ref[:, cols] = (o * g_ref[:, cols] * (1.0 - lam_init)).astype(BF16)


def _diff(proj, lam, g, lam_init):
    nq = SEQ // TQ
    return pl.pallas_call(
        functools.partial(_diff_kernel, lam_init),
        grid=(BATCH, nq),
        in_specs=[
            pl.BlockSpec((None, TQ, 512), lambda b, i: (SEG_DQ, b * nq + i, 0)),
            pl.BlockSpec((None, SEQ, 512), lambda b, i: (SEG_DK, b, 0)),
            pl.BlockSpec((None, SEQ, 512), lambda b, i: (SEG_DV, b, 0)),
            pl.BlockSpec((4, DIFF_HEAD_DIM), lambda b, i: (0, 0)),
            pl.BlockSpec((1, 512), lambda b, i: (0, 0)),
        ],
        out_specs=pl.BlockSpec((TQ, 512), lambda b, i: (b * nq + i, 0)),
        out_shape=jax.ShapeDtypeStruct((TOKENS, 512), BF16),
        scratch_shapes=[pltpu.VMEM((2 * DIFF_HEADS, TQ, LANES), F32),
                        pltpu.VMEM((2 * DIFF_HEADS, TQ, LANES), F32),
                        pltpu.VMEM((2 * DIFF_HEADS, TQ, LANES), F32)],
        compiler_params=_params("parallel", "arbitrary"),
        name="diff_attn",
    )(proj, proj, proj, lam, g)


def _s5_kernel(u_ref, m_ref, bre_ref, bim_ref, cre_ref, cim_ref, are_ref, aim_ref, y_ref, *state):
    nslab = SSM_SCOLS // LANES
    sre, sim = state[:nslab], state[nslab:]
    u = jnp.concatenate([u_ref[s] for s in range(SSM_CHUNK)], axis=1)
    for k in range(nslab):
        cols = slice(k * LANES, (k + 1) * LANES)
        sre[k][...] = _dot(u, bre_ref[:, cols])
        sim[k][...] = _dot(u, bim_ref[:, cols])
    ar = [are_ref[:, k * LANES:(k + 1) * LANES] for k in range(nslab)]
    ai = [aim_ref[:, k * LANES:(k + 1) * LANES] for k in range(nslab)]

    def body(c, carry):
        rows = pl.ds(c, BATCH, stride=SSM_NCHUNK)
        nxt = []
        for k in range(nslab):
            xr, xi = carry[k]
            sr = sre[k][rows, :]
            si = sim[k][rows, :]
            sre[k][rows, :] = xr
            sim[k][rows, :] = xi
            nxt.append((ar[k] * xr - ai[k] * xi + sr, ar[k] * xi + ai[k] * xr + si))
        return tuple(nxt)

    zero = jnp.zeros((BATCH, LANES), F32)
    lax.fori_loop(0, SSM_NCHUNK, body, tuple((zero, zero) for _ in range(nslab)))
    xr = jnp.concatenate([r[...] for r in sre], axis=1).astype(BF16)
    xi = jnp.concatenate([r[...] for r in sim], axis=1).astype(BF16)
    for c in range(SSM_COLS // (2 * LANES)):
        cols = slice(c * 2 * LANES, (c + 1) * 2 * LANES)
        y = _dot(u, m_ref[:, cols]) + _dot(xr, cre_ref[:, cols]) + _dot(xi, cim_ref[:, cols])
        y_ref[2 * c] = y[:, :LANES].astype(BF16)
        y_ref[2 * c + 1] = y[:, LANES:].astype(BF16)


def _s5_scan(u_stack, mats):
    m, bre, bim, cre, cim, are, aim = mats

    def spec(*shape):
        return pl.BlockSpec((None,) + shape, lambda g: (g,) + (0,) * len(shape))

    io_spec = pl.BlockSpec((SSM_CHUNK, SSM_ROWS, LANES), lambda g: (0, 0, g))
    return pl.pallas_call(
        _s5_kernel,
        grid=(SSM_NGBLK,),
        in_specs=[io_spec, spec(SSM_COLS, SSM_COLS),
                  spec(SSM_COLS, SSM_SCOLS), spec(SSM_COLS, SSM_SCOLS),
                  spec(SSM_SCOLS, SSM_COLS), spec(SSM_SCOLS, SSM_COLS),
                  spec(1, SSM_SCOLS), spec(1, SSM_SCOLS)],
        out_specs=io_spec,
        out_shape=jax.ShapeDtypeStruct((SSM_CHUNK, SSM_ROWS, SSM_WIDTH), BF16),
        scratch_shapes=[pltpu.VMEM((SSM_ROWS, LANES), F32) for _ in range(2 * SSM_SCOLS // LANES)],
        compiler_params=_params("parallel"),
        name="s5_scan",
    )(u_stack, m, bre, bim, cre, cim, are, aim)


def _s5_matrices(lam_re, lam_im, log_dt, b_re, b_im, c_re, c_im):
    hi = lax.Precision.HIGHEST
    L, NB, GB = SSM_CHUNK, SSM_NGBLK, SSM_GBLK
    lr = lam_re.astype(F32)
    li = lam_im.astype(F32)
    dt = jnp.exp(log_dt.astype(F32))[:, None]
    mag = jnp.exp(lr * dt)
    ang = li * dt
    ar = mag * jnp.cos(ang)
    ai = mag * jnp.sin(ang)
    er = ar - 1.0
    ei = ai
    den = lr * lr + li * li
    qr = (er * lr + ei * li) / den
    qi = (ei * lr - er * li) / den
    br = b_re.astype(F32)
    bi = b_im.astype(F32)
    bbr = qr[:, :, None] * br - qi[:, :, None] * bi
    bbi = qr[:, :, None] * bi + qi[:, :, None] * br
    k = jnp.arange(L + 1, dtype=F32)[:, None, None]
    pmag = jnp.exp(lr[None] * dt[None] * k)
    pr = pmag * jnp.cos(ang[None] * k)
    pi = pmag * jnp.sin(ang[None] * k)
    cr = c_re.astype(F32)
    ci = c_im.astype(F32)
    e_r = cr[None] * pr[:, :, None, :] - ci[None] * pi[:, :, None, :]
    e_i = cr[None] * pi[:, :, None, :] + ci[None] * pr[:, :, None, :]
    kern = (jnp.einsum('kgnp,gpm->kgnm', e_r[:L], bbr, precision=hi)
            - jnp.einsum('kgnp,gpm->kgnm', e_i[:L], bbi, precision=hi))
    lag = jnp.arange(L)[None, :] - jnp.arange(L)[:, None]
    toep = jnp.where((lag >= 0)[:, :, None, None, None], kern[jnp.clip(lag, 0, L - 1)], 0.0)
    eye = jnp.eye(GB, dtype=F32)

    toep = toep.reshape(L, L, NB, GB, SSM_GROUP, SSM_GROUP)
    m = jnp.einsum('stbgnm,gh->bsgmthn', toep, eye).reshape(NB, SSM_COLS, SSM_COLS)
    krev = (L - 1) - jnp.arange(L, dtype=F32)[:, None, None]
    rmag = jnp.exp(lr[None] * dt[None] * krev)
    rev_r = rmag * jnp.cos(ang[None] * krev)
    rev_i = rmag * jnp.sin(ang[None] * krev)
    sb_r = rev_r[:, :, :, None] * bbr[None] - rev_i[:, :, :, None] * bbi[None]
    sb_i = rev_r[:, :, :, None] * bbi[None] + rev_i[:, :, :, None] * bbr[None]
    sb_r = sb_r.reshape(L, NB, GB, SSM_STATE, SSM_GROUP)
    sb_i = sb_i.reshape(L, NB, GB, SSM_STATE, SSM_GROUP)
    bre = jnp.einsum('sbgpm,gh->bsgmhp', sb_r, eye).reshape(NB, SSM_COLS, SSM_SCOLS)
    bim = jnp.einsum('sbgpm,gh->bsgmhp', sb_i, eye).reshape(NB, SSM_COLS, SSM_SCOLS)
    ro_r = e_r[1:].reshape(L, NB, GB, SSM_GROUP, SSM_STATE)
    ro_i = e_i[1:].reshape(L, NB, GB, SSM_GROUP, SSM_STATE)
    cre = jnp.einsum('tbgnp,gh->bgpthn', ro_r, eye).reshape(NB, SSM_SCOLS, SSM_COLS)
    cim = -jnp.einsum('tbgnp,gh->bgpthn', ro_i, eye).reshape(NB, SSM_SCOLS, SSM_COLS)
    are = pr[L].reshape(NB, 1, SSM_SCOLS)
    aim = pi[L].reshape(NB, 1, SSM_SCOLS)
    return (m.astype(BF16), bre.astype(BF16), bim.astype(BF16), cre.astype(BF16), cim.astype(BF16),
            are, aim)


def _s5_post_kernel(y_ref, u_ref, d_ref, w_ref, o_ref):
    y = y_ref[...] + d_ref[...] * u_ref[...].astype(F32)
    y = jax.nn.gelu(y, approximate=True)
    z = _dot(y.astype(BF16), w_ref[...])
    o_ref[...] = (z[:, :SSM_WIDTH] * jax.nn.sigmoid(z[:, SSM_WIDTH:])).astype(BF16)


def _s5_post(y_stack, u_stack, d, w_glu):
    tm = TM_POST
    stack_spec = pl.BlockSpec((None, tm, SSM_WIDTH), lambda i, s: (s, i, 0))
    return pl.pallas_call(
        _s5_post_kernel,
        grid=(SSM_ROWS // tm, SSM_CHUNK),
        in_specs=[
            stack_spec,
            stack_spec,
            pl.BlockSpec((1, SSM_WIDTH), lambda i, s: (0, 0)),
            pl.BlockSpec((SSM_WIDTH, 2 * SSM_WIDTH), lambda i, s: (0, 0)),
        ],
        out_specs=pl.BlockSpec((tm, SSM_WIDTH), lambda i, s: (i, s)),
        out_shape=jax.ShapeDtypeStruct((SSM_ROWS, SSM_CHUNK * SSM_WIDTH), BF16),
        compiler_params=_params("parallel", "parallel"),
        name="s5_post",
    )(y_stack, u_stack, d, w_glu)


def _merge_kernel(x_ref, ys_ref, yd_ref, yf_ref, wgl_ref, wbr_ref, wout_ref, g_ref, b_ref, o_ref):
    x = x_ref[...]
    xb = x.astype(BF16)
    merged = None
    for n, y_ref in enumerate((ys_ref, yd_ref, yf_ref)):
        gate = jax.nn.sigmoid(_dot(xb, wgl_ref[:, n * D_MODEL:(n + 1) * D_MODEL]))
        term = gate * _dot(y_ref[...], wbr_ref[n])
        merged = term if merged is None else merged + term
    mix = _dot(merged.astype(BF16), wout_ref[...])
    o_ref[...] = _layer_norm(ALPHA * x + mix, g_ref[...], b_ref[...])


def _merge(x, ys, yd, yf, wgl, wbr, wout, g, b):
    tm = TM_MERGE
    row = lambda i: (i, 0)
    const2 = lambda i: (0, 0)
    return pl.pallas_call(
        _merge_kernel,
        grid=(TOKENS // tm,),
        in_specs=[
            pl.BlockSpec((tm, D_MODEL), row),
            pl.BlockSpec((tm, 512), row),
            pl.BlockSpec((tm, 512), row),
            pl.BlockSpec((tm, 512), row),
            pl.BlockSpec((D_MODEL, N_BRANCH * D_MODEL), const2),
            pl.BlockSpec((N_BRANCH, BRANCH_WIDTH, D_MODEL), lambda i: (0, 0, 0)),
            pl.BlockSpec((D_MODEL, D_MODEL), const2),
            pl.BlockSpec((1, D_MODEL), const2),
            pl.BlockSpec((1, D_MODEL), const2),
        ],
        out_specs=pl.BlockSpec((tm, D_MODEL), row),
        out_shape=jax.ShapeDtypeStruct((TOKENS, D_MODEL), F32),
        compiler_params=_params("parallel"),
        name="merge_ln",
    )(x, ys, yd, yf, wgl, wbr, wout, g, b)


def _router_kernel(x_ref, w_ref, b_ref, o_ref):
    logits = jnp.dot(x_ref[...], w_ref[...], preferred_element_type=F32,
                     precision=lax.Precision.HIGHEST) + b_ref[...]
    lane = lax.broadcasted_iota(jnp.int32, logits.shape, 1).astype(F32)
    big = jnp.float32(LANES)
    neg = jnp.float32(-jnp.inf)

    def first_argmax(v, vmax):
        return jnp.min(jnp.where(v == vmax, lane, big), axis=1, keepdims=True)

    gl = jnp.where(lane < N_EXPERT_GROUPS, logits, neg)
    ge = jnp.exp(gl - jnp.max(gl, axis=1, keepdims=True))
    pg = ge / jnp.sum(ge, axis=1, keepdims=True)
    gp = jnp.max(pg, axis=1, keepdims=True)
    gi = first_argmax(pg, gp)
    base = N_EXPERT_GROUPS + EXPERTS_PER_GROUP * gi
    sel = jnp.logical_and(lane >= base, lane < base + EXPERTS_PER_GROUP)
    el = jnp.where(sel, logits, neg)
    ev1 = jnp.max(el, axis=1, keepdims=True)
    i1 = first_argmax(el, ev1)
    el2 = jnp.where(lane == i1, neg, el)
    ev2 = jnp.max(el2, axis=1, keepdims=True)
    i2 = first_argmax(el2, ev2)
    e2 = jnp.exp(ev2 - ev1)
    den = 1.0 + e2
    w1 = gp * (1.0 / den)
    w2 = gp * (e2 / den)
    o_ref[...] = jnp.where(lane == i1, w1, 0.0) + jnp.where(lane == i2, w2, 0.0)


def _router(x, w, b):
    tm = TM_ROUTE
    return pl.pallas_call(
        _router_kernel,
        grid=(TOKENS // tm,),
        in_specs=[
            pl.BlockSpec((tm, D_MODEL), lambda i: (i, 0)),
            pl.BlockSpec((D_MODEL, LANES), lambda i: (0, 0)),
            pl.BlockSpec((1, LANES), lambda i: (0, 0)),
        ],
        out_specs=pl.BlockSpec((tm, LANES), lambda i: (i, 0)),
        out_shape=jax.ShapeDtypeStruct((TOKENS, LANES), F32),
        compiler_params=_params("parallel"),
        name="router",
    )(x, w, b)


def _moe_kernel(x_ref, gate_ref, wg_ref, wu_ref, wd_ref, g_ref, b_ref, o_ref, acc_ref):
    e = pl.program_id(1)
    xb = x_ref[...].astype(BF16)
    gates = gate_ref[...]
    lane = lax.broadcasted_iota(jnp.int32, gates.shape, 1)
    gate = jnp.sum(jnp.where(lane == e + N_EXPERT_GROUPS, gates, 0.0), axis=1, keepdims=True)
    hidden = jax.nn.silu(_dot(xb, wg_ref[...])) * _dot(xb, wu_ref[...]) * gate
    contrib = _dot(hidden.astype(BF16), wd_ref[...])

    @pl.when(e == 0)
    def _():
        acc_ref[...] = contrib

    @pl.when(e > 0)
    def _():
        acc_ref[...] += contrib

    @pl.when(e == N_EXPERTS - 1)
    def _():
        o_ref[...] = _layer_norm(ALPHA * x_ref[...] + acc_ref[...], g_ref[...], b_ref[...])


def _moe(x, gates, wg, wu, wd, g, b):
    tm = TM_MOE
    return pl.pallas_call(
        _moe_kernel,
        grid=(TOKENS // tm, N_EXPERTS),
        in_specs=[
            pl.BlockSpec((tm, D_MODEL), lambda i, e: (i, 0)),
            pl.BlockSpec((tm, LANES), lambda i, e: (i, 0)),
            pl.BlockSpec((None, D_MODEL, D_EXPERT), lambda i, e: (e, 0, 0)),
            pl.BlockSpec((None, D_MODEL, D_EXPERT), lambda i, e: (e, 0, 0)),
            pl.BlockSpec((None, D_EXPERT, D_MODEL), lambda i, e: (e, 0, 0)),
            pl.BlockSpec((1, D_MODEL), lambda i, e: (0, 0)),
            pl.BlockSpec((1, D_MODEL), lambda i, e: (0, 0)),
        ],
        out_specs=pl.BlockSpec((tm, D_MODEL), lambda i, e: (i, 0)),
        out_shape=jax.ShapeDtypeStruct((TOKENS, D_MODEL), F32),
        scratch_shapes=[pltpu.VMEM((tm, D_MODEL), F32)],
        compiler_params=_params("parallel", "arbitrary"),
        name="moe_ln",
    )(x, gates, wg, wu, wd, g, b)


def _rope_tables():
    pos = jnp.arange(SEQ, dtype=F32)
    inv_freq = ROPE_THETA ** (-jnp.arange(0, DIFF_HEAD_DIM, 2, dtype=F32) / DIFF_HEAD_DIM)
    ang = pos[:, None] * inv_freq[None, :]
    emb = jnp.concatenate([ang, ang], axis=-1)
    cos = jnp.cos(emb)
    sin = jnp.sin(emb)
    sign = jnp.where(jnp.arange(DIFF_HEAD_DIM) < DIFF_HEAD_DIM // 2, -1.0, 1.0).astype(F32)
    return jnp.tile(cos, (1, 2)), jnp.tile(sin * sign, (1, 2))


def _split_w_in(w_in_l):
    qscale = DIFF_HEAD_DIM ** -0.5 * LOG2E
    w_u = w_in_l[:, :SSM_WIDTH].astype(BF16)
    segs = []
    for s in range(N_SEG):
        w = w_in_l[:, (s + 1) * 512:(s + 2) * 512]
        if s in (SEG_DQ, SEG_FQ):
            w = w * qscale
        segs.append(w)
    w6 = jnp.stack(segs).astype(BF16)
    off = (N_SEG + 1) * 512
    wff_t = w_in_l[:, off:off + FOX_HEADS].T.astype(BF16)
    wgl = w_in_l[:, off + FOX_HEADS:].astype(BF16)
    return w_u, w6, wff_t, wgl


def kernel(x, w_in, w_branch, w_out, ssm_lambda_re, ssm_lambda_im, ssm_log_dt, ssm_b_re, ssm_b_im,
           ssm_c_re, ssm_c_im, ssm_d, ssm_w_glu, diff_lambda, diff_norm_g, fox_f_bias, ln1_g, ln1_b,
           moe_w_group, moe_b_group, moe_w_expert, moe_b_expert, moe_w_gate, moe_w_up, moe_w_down,
           ln2_g, ln2_b):
    cos, sin = _rope_tables()
    h = x.reshape(TOKENS, D_MODEL)
    for l in range(DEPTH):
        lam_init = 0.8 - 0.6 * math.exp(-0.3 * l)
        w_u, w6, wff_t, wgl = _split_w_in(w_in[l])
        proj, fft = _inproj(h, w6, wff_t, cos, sin)

        mats = _s5_matrices(ssm_lambda_re[l], ssm_lambda_im[l], ssm_log_dt[l], ssm_b_re[l],
                            ssm_b_im[l], ssm_c_re[l], ssm_c_im[l])
        u_stack = _inproj_u(h.reshape(SSM_ROWS, SSM_CHUNK * D_MODEL), w_u)
        y_stack = _s5_scan(u_stack, mats)
        y_ssm = _s5_post(y_stack, u_stack, ssm_d[l].reshape(1, SSM_WIDTH).astype(F32),
                         ssm_w_glu[l].astype(BF16)).reshape(TOKENS, SSM_WIDTH)

        y_diff = _diff(proj, diff_lambda[l].astype(F32), diff_norm_g[l].reshape(1, 512).astype(F32),
                       lam_init)
        ct, c = _fgate(fft, fox_f_bias[l].reshape(FOX_HEADS, 1).astype(F32))
        y_fox = _fox(proj, ct, c)

        h = _merge(h, y_ssm, y_diff, y_fox, wgl, w_branch[l].astype(BF16), w_out[l].astype(BF16),
                   ln1_g[l].reshape(1, D_MODEL), ln1_b[l].reshape(1, D_MODEL))

        w_r = jnp.concatenate([moe_w_group[l], moe_w_expert[l]], axis=1).astype(F32)
        w_r = jnp.pad(w_r, ((0, 0), (0, LANES - w_r.shape[1])))
        b_r = jnp.concatenate([moe_b_group[l], moe_b_expert[l]]).astype(F32)
        b_r = jnp.pad(b_r, (0, LANES - b_r.shape[0])).reshape(1, LANES)
        gates = _router(h, w_r, b_r)
        h = _moe(h, gates, moe_w_gate[l].astype(BF16), moe_w_up[l].astype(BF16),
                 moe_w_down[l].astype(BF16), ln2_g[l].reshape(1, D_MODEL), ln2_b[l].reshape(1, D_MODEL))
    return h.reshape(BATCH, SEQ, D_MODEL)
```

```python
import functools
import math

import jax
import jax.numpy as jnp
from jax import lax
from jax.experimental import pallas as pl
from jax.experimental.pallas import tpu as pltpu

F32 = jnp.float32
BF16 = jnp.bfloat16

D_MODEL = 1024
BATCH = 8
SEQ = 2048
DEPTH = 4
TOKENS = BATCH * SEQ

SSM_WIDTH = 512
SSM_GROUP = 16
SSM_GROUPS = 32
SSM_STATE = 64
SSM_CHUNK = 8
SSM_NCHUNK = SEQ // SSM_CHUNK
SSM_ROWS = BATCH * SSM_NCHUNK
SSM_GBLK = 8
SSM_NGBLK = SSM_GROUPS // SSM_GBLK
SSM_COLS = SSM_CHUNK * 128
SSM_SCOLS = SSM_GBLK * SSM_STATE

DIFF_HEADS = 4
DIFF_HEAD_DIM = 64
FOX_HEADS = 8
FOX_HEAD_DIM = 64
BRANCH_WIDTH = 512
N_BRANCH = 3
ROPE_THETA = 10000.0

N_EXPERT_GROUPS = 4
EXPERTS_PER_GROUP = 4
N_EXPERTS = 16
D_EXPERT = 256

ALPHA = (2 * DEPTH) ** 0.25
LN_EPS = 1e-5
RMS_EPS = 1e-6
NEG_INF = -1e30
LOG2E = math.log2(math.e)

LANES = 128
N_SEG = 6
SEG_DQ, SEG_DK, SEG_DV, SEG_FQ, SEG_FK, SEG_FV = range(N_SEG)
VMEM_LIMIT = 56 * 1024 * 1024

TM_PROJ = 512
TQ = 256
TM_POST = 512
TM_MERGE = 256
TM_ROUTE = 512
TM_MOE = 1024


def _params(*sem):
    return pltpu.CompilerParams(dimension_semantics=sem, vmem_limit_bytes=VMEM_LIMIT)


def _dot(a, b):
    return jnp.dot(a, b, preferred_element_type=F32)


def _dot_nt(a, b):
    return lax.dot_general(a, b, (((1,), (1,)), ((), ())), preferred_element_type=F32)


def _layer_norm(y, g, b):
    mu = jnp.mean(y, axis=-1, keepdims=True)
    d = y - mu
    var = jnp.mean(d * d, axis=-1, keepdims=True)
    return d * lax.rsqrt(var + LN_EPS) * g + b


def _inproj_kernel(x_ref, w_ref, wff_ref, cos_ref, sin_ref, o_ref, u_ref, fft_ref, tmp_ref):
    tm = x_ref.shape[0]
    xb = x_ref[...].astype(BF16)
    cos = cos_ref[...]
    sin = sin_ref[...]
    lane = lax.broadcasted_iota(jnp.int32, cos.shape, 1)
    first_half = (lane & 32) == 0
    for seg in range(N_SEG):
        acc = _dot(xb, w_ref[seg + 1])
        if seg in (SEG_DQ, SEG_DK):
            for c in range(512 // LANES):
                t = acc[:, c * LANES:(c + 1) * LANES]
                rot = jnp.where(first_half, pltpu.roll(t, LANES - 32, 1), pltpu.roll(t, 32, 1))
                o_ref[seg, :, c * LANES:(c + 1) * LANES] = (t * cos + rot * sin).astype(BF16)
        else:
            o_ref[seg] = acc.astype(BF16)
    fft_ref[...] = _dot_nt(wff_ref[...], xb)
    u = _dot(xb, w_ref[0])
    for c in range(SSM_WIDTH // LANES):
        tmp_ref[c] = u[:, c * LANES:(c + 1) * LANES]
    for s in range(SSM_CHUNK):
        for c in range(SSM_WIDTH // LANES):
            u_ref[s, :, c * LANES:(c + 1) * LANES] = (
                tmp_ref[c, pl.ds(s, tm // SSM_CHUNK, stride=SSM_CHUNK), :].astype(BF16))


def _inproj(x, w7, wff_t, cos, sin):
    tm = TM_PROJ
    nrope = SEQ // tm
    return pl.pallas_call(
        _inproj_kernel,
        grid=(TOKENS // tm,),
        in_specs=[
            pl.BlockSpec((tm, D_MODEL), lambda i: (i, 0)),
            pl.BlockSpec((N_SEG + 1, D_MODEL, 512), lambda i: (0, 0, 0)),
            pl.BlockSpec((FOX_HEADS, D_MODEL), lambda i: (0, 0)),
            pl.BlockSpec((tm, LANES), lambda i: (i % nrope, 0)),
            pl.BlockSpec((tm, LANES), lambda i: (i % nrope, 0)),
        ],
        out_specs=[
            pl.BlockSpec((N_SEG, tm, 512), lambda i: (0, i, 0)),
            pl.BlockSpec((SSM_CHUNK, tm // SSM_CHUNK, SSM_WIDTH), lambda i: (0, i, 0)),
            pl.BlockSpec((FOX_HEADS, tm), lambda i: (0, i)),
        ],
        out_shape=[
            jax.ShapeDtypeStruct((N_SEG, TOKENS, 512), BF16),
            jax.ShapeDtypeStruct((SSM_CHUNK, SSM_ROWS, SSM_WIDTH), BF16),
            jax.ShapeDtypeStruct((FOX_HEADS, TOKENS), F32),
        ],
        scratch_shapes=[pltpu.VMEM((SSM_WIDTH // LANES, tm, LANES), F32)],
        compiler_params=_params("parallel"),
        name="inproj",
    )(x, w7, wff_t, cos, sin)


def _fgate_kernel(fft_ref, bias_ref, ct_ref, c_ref):
    z = fft_ref[...] + bias_ref[...]
    c = jnp.minimum(z, 0.0) - jnp.log1p(jnp.exp(-jnp.abs(z)))
    lane = lax.broadcasted_iota(jnp.int32, c.shape, 1)
    shift = 1
    while shift < SEQ:
        c = c + jnp.where(lane >= shift, pltpu.roll(c, shift, 1), 0.0)
        shift *= 2
    c = c * LOG2E
    ct_ref[...] = c
    padded = jnp.concatenate([c, jnp.zeros((LANES - FOX_HEADS, SEQ), F32)], axis=0)
    c_ref[...] = padded.T


def _fgate(fft, bias):
    return pl.pallas_call(
        _fgate_kernel,
        grid=(BATCH,),
        in_specs=[
            pl.BlockSpec((FOX_HEADS, SEQ), lambda b: (0, b)),
            pl.BlockSpec((FOX_HEADS, 1), lambda b: (0, 0)),
        ],
        out_specs=[
            pl.BlockSpec((FOX_HEADS, SEQ), lambda b: (0, b)),
            pl.BlockSpec((SEQ, LANES), lambda b: (b, 0)),
        ],
        out_shape=[
            jax.ShapeDtypeStruct((FOX_HEADS, TOKENS), F32),
            jax.ShapeDtypeStruct((TOKENS, LANES), F32),
        ],
        compiler_params=_params("parallel"),
        name="fgate",
    )(fft, bias)


def _online_softmax(s, m_ref, l_ref, idx):
    m_old = m_ref[idx]
    m_new = jnp.maximum(m_old, jnp.max(s, axis=1, keepdims=True))
    alpha = jnp.exp2(m_old - m_new)
    parts = [jnp.exp2(s[:, c * LANES:(c + 1) * LANES] - m_new) for c in range(s.shape[1] // LANES)]
    l_ref[idx] = alpha * l_ref[idx] + functools.reduce(lambda a, b: a + b, parts)
    m_ref[idx] = m_new
    return jnp.concatenate(parts, axis=1).astype(BF16), alpha


def _causal_mask(n):
    row = lax.broadcasted_iota(jnp.int32, (n, n), 0)
    col = lax.broadcasted_iota(jnp.int32, (n, n), 1)
    return col <= row


def _init_softmax_state(m_ref, l_ref, acc_ref):
    m_ref[...] = jnp.full(m_ref.shape, NEG_INF, F32)
    l_ref[...] = jnp.zeros(l_ref.shape, F32)
    acc_ref[...] = jnp.zeros(acc_ref.shape, F32)


def _fox_kernel(q_ref, k_ref, v_ref, ct_ref, c_ref, o_ref, m_ref, l_ref, acc_ref, cq_ref):
    i = pl.program_id(1)
    tq = q_ref.shape[0]
    lane = lax.broadcasted_iota(jnp.int32, (tq, LANES), 1)
    lo = lane < FOX_HEAD_DIM
    _init_softmax_state(m_ref, l_ref, acc_ref)
    for h in range(FOX_HEADS):
        cq_ref[h] = jnp.broadcast_to(c_ref[:, h:h + 1], (tq, LANES))

    def block(j, masked):
        start = pl.multiple_of(j * tq, tq)
        for hp in range(FOX_HEADS // 2):
            cols = slice(hp * LANES, (hp + 1) * LANES)
            q = q_ref[:, cols]
            zero = jnp.zeros_like(q)
            kb = k_ref[pl.ds(start, tq), cols]
            vb = v_ref[pl.ds(start, tq), cols]
            pv, alpha = [], []
            for h, qh in ((2 * hp, jnp.where(lo, q, zero)), (2 * hp + 1, jnp.where(lo, zero, q))):
                cq = cq_ref[h]
                ck = ct_ref[h:h + 1, pl.ds(start, tq)]
                raw = _dot_nt(qh, kb)
                s = jnp.concatenate(
                    [raw[:, c * LANES:(c + 1) * LANES] + (cq - ck[:, c * LANES:(c + 1) * LANES])
                     for c in range(tq // LANES)], axis=1)
                if masked:
                    s = jnp.where(_causal_mask(tq), s, NEG_INF)
                p, a = _online_softmax(s, m_ref, l_ref, h)
                pv.append(_dot(p, vb))
                alpha.append(a)
            acc_ref[hp] = acc_ref[hp] * jnp.where(lo, alpha[0], alpha[1]) + jnp.where(lo, pv[0], pv[1])

    def full_block(j, carry):
        block(j, False)
        return carry

    lax.fori_loop(0, i, full_block, 0)
    block(i, True)
    for hp in range(FOX_HEADS // 2):
        la = jnp.sum(l_ref[2 * hp], axis=1, keepdims=True)
        lb = jnp.sum(l_ref[2 * hp + 1], axis=1, keepdims=True)
        o_ref[:, hp * LANES:(hp + 1) * LANES] = (acc_ref[hp] / jnp.where(lo, la, lb)).astype(BF16)


def _fox(proj, ct, c):
    nq = SEQ // TQ
    return pl.pallas_call(
        _fox_kernel,
        grid=(BATCH, nq),
        in_specs=[
            pl.BlockSpec((None, TQ, 512), lambda b, i: (SEG_FQ, b * nq + i, 0)),
            pl.BlockSpec((None, SEQ, 512), lambda b, i: (SEG_FK, b, 0)),
            pl.BlockSpec((None, SEQ, 512), lambda b, i: (SEG_FV, b, 0)),
            pl.BlockSpec((FOX_HEADS, SEQ), lambda b, i: (0, b)),
            pl.BlockSpec((TQ, LANES), lambda b, i: (b * nq + i, 0)),
        ],
        out_specs=pl.BlockSpec((TQ, 512), lambda b, i: (b * nq + i, 0)),
        out_shape=jax.ShapeDtypeStruct((TOKENS, 512), BF16),
        scratch_shapes=[pltpu.VMEM((FOX_HEADS, TQ, LANES), F32),
                        pltpu.VMEM((FOX_HEADS, TQ, LANES), F32),
                        pltpu.VMEM((FOX_HEADS // 2, TQ, LANES), F32),
                        pltpu.VMEM((FOX_HEADS, TQ, LANES), F32)],
        compiler_params=_params("parallel", "arbitrary"),
        name="fox_attn",
    )(proj, proj, proj, ct, c)


def _diff_kernel(lam_init, q_ref, k_ref, v_ref, lam_ref, g_ref, o_ref, m_ref, l_ref, acc_ref):
    i = pl.program_id(1)
    tq = q_ref.shape[0]
    lane = lax.broadcasted_iota(jnp.int32, (tq, LANES), 1)
    lo = lane < DIFF_HEAD_DIM
    _init_softmax_state(m_ref, l_ref, acc_ref)

    def block(j, masked):
        start = pl.multiple_of(j * tq, tq)
        for h in range(DIFF_HEADS):
            cols = slice(h * LANES, (h + 1) * LANES)
            q = q_ref[:, cols]
            zero = jnp.zeros_like(q)
            kb = k_ref[pl.ds(start, tq), cols]
            vb = v_ref[pl.ds(start, tq), cols]
            for idx, qm in ((2 * h, jnp.where(lo, q, zero)), (2 * h + 1, jnp.where(lo, zero, q))):
                s = _dot_nt(qm, kb)
                if masked:
                    s = jnp.where(_causal_mask(tq), s, NEG_INF)
                p, a = _online_softmax(s, m_ref, l_ref, idx)
                acc_ref[idx] = acc_ref[idx] * a + _dot(p, vb)

    def full_block(j, carry):
        block(j, False)
        return carry

    lax.fori_loop(0, i, full_block, 0)
    block(i, True)
    lam_rows = lam_ref[...]
    lam = (jnp.exp(jnp.sum(lam_rows[0:1] * lam_rows[1:2], axis=1, keepdims=True))
           - jnp.exp(jnp.sum(lam_rows[2:3] * lam_rows[3:4], axis=1, keepdims=True)) + lam_init)
    for h in range(DIFF_HEADS):
        cols = slice(h * LANES, (h + 1) * LANES)
        l1 = jnp.sum(l_ref[2 * h], axis=1, keepdims=True)
        l2 = jnp.sum(l_ref[2 * h + 1], axis=1, keepdims=True)
        o = acc_ref[2 * h] / l1 - lam * (acc_ref[2 * h + 1] / l2)
        o = o * lax.rsqrt(jnp.mean(o * o, axis=1, keepdims=True) + RMS_EPS)
        o_ref[:, cols] = (o * g_ref[:, cols] * (1.0 - lam_init)).astype(BF16)


def _diff(proj, lam, g, lam_init):
    nq = SEQ // TQ
    return pl.pallas_call(
        functools.partial(_diff_kernel, lam_init),
        grid=(BATCH, nq),
        in_specs=[
            pl.BlockSpec((None, TQ, 512), lambda b, i: (SEG_DQ, b * nq + i, 0)),
            pl.BlockSpec((None, SEQ, 512), lambda b, i: (SEG_DK, b, 0)),
            pl.BlockSpec((None, SEQ, 512), lambda b, i: (SEG_DV, b, 0)),
            pl.BlockSpec((4, DIFF_HEAD_DIM), lambda b, i: (0, 0)),
            pl.BlockSpec((1, 512), lambda b, i: (0, 0)),
        ],
        out_specs=pl.BlockSpec((TQ, 512), lambda b, i: (b * nq + i, 0)),
        out_shape=jax.ShapeDtypeStruct((TOKENS, 512), BF16),
        scratch_shapes=[pltpu.VMEM((2 * DIFF_HEADS, TQ, LANES), F32),
                        pltpu.VMEM((2 * DIFF_HEADS, TQ, LANES), F32),
                        pltpu.VMEM((2 * DIFF_HEADS, TQ, LANES), F32)],
        compiler_params=_params("parallel", "arbitrary"),
        name="diff_attn",
    )(proj, proj, proj, lam, g)


def _s5_kernel(u_ref, m_ref, bre_ref, bim_ref, cre_ref, cim_ref, are_ref, aim_ref, y_ref, *state):
    nslab = SSM_SCOLS // LANES
    sre, sim = state[:nslab], state[nslab:]
    u = jnp.concatenate([u_ref[s] for s in range(SSM_CHUNK)], axis=1)
    for k in range(0, nslab, 2):
        cols = slice(k * LANES, (k + 2) * LANES)
        for dst, b_ref in ((sre, bre_ref), (sim, bim_ref)):
            part = _dot(u, b_ref[:, cols])
            dst[k][...] = part[:, :LANES]
            dst[k + 1][...] = part[:, LANES:]
    ar = [are_ref[:, k * LANES:(k + 1) * LANES] for k in range(nslab)]
    ai = [aim_ref[:, k * LANES:(k + 1) * LANES] for k in range(nslab)]

    def body(c, carry):
        rows = pl.ds(c, BATCH, stride=SSM_NCHUNK)
        nxt = []
        for k in range(nslab):
            xr, xi = carry[k]
            sr = sre[k][rows, :]
            si = sim[k][rows, :]
            sre[k][rows, :] = xr
            sim[k][rows, :] = xi
            nxt.append((ar[k] * xr - ai[k] * xi + sr, ar[k] * xi + ai[k] * xr + si))
        return tuple(nxt)

    zero = jnp.zeros((BATCH, LANES), F32)
    lax.fori_loop(0, SSM_NCHUNK, body, tuple((zero, zero) for _ in range(nslab)))
    xr = jnp.concatenate([r[...] for r in sre], axis=1).astype(BF16)
    xi = jnp.concatenate([r[...] for r in sim], axis=1).astype(BF16)
    for c in range(SSM_COLS // (2 * LANES)):
        cols = slice(c * 2 * LANES, (c + 1) * 2 * LANES)
        y = _dot(u, m_ref[:, cols]) + _dot(xr, cre_ref[:, cols]) + _dot(xi, cim_ref[:, cols])
        y_ref[2 * c] = y[:, :LANES].astype(BF16)
        y_ref[2 * c + 1] = y[:, LANES:].astype(BF16)


def _s5_scan(u_stack, mats):
    m, bre, bim, cre, cim, are, aim = mats

    def spec(*shape):
        return pl.BlockSpec((None,) + shape, lambda g: (g,) + (0,) * len(shape))

    io_spec = pl.BlockSpec((SSM_CHUNK, SSM_ROWS, LANES), lambda g: (0, 0, g))
    return pl.pallas_call(
        _s5_kernel,
        grid=(SSM_NGBLK,),
        in_specs=[io_spec, spec(SSM_COLS, SSM_COLS),
                  spec(SSM_COLS, SSM_SCOLS), spec(SSM_COLS, SSM_SCOLS),
                  spec(SSM_SCOLS, SSM_COLS), spec(SSM_SCOLS, SSM_COLS),
                  spec(1, SSM_SCOLS), spec(1, SSM_SCOLS)],
        out_specs=io_spec,
        out_shape=jax.ShapeDtypeStruct((SSM_CHUNK, SSM_ROWS, SSM_WIDTH), BF16),
        scratch_shapes=[pltpu.VMEM((SSM_ROWS, LANES), F32) for _ in range(2 * SSM_SCOLS // LANES)],
        compiler_params=_params("parallel"),
        name="s5_scan",
    )(u_stack, m, bre, bim, cre, cim, are, aim)


def _s5_matrices(lam_re, lam_im, log_dt, b_re, b_im, c_re, c_im):
    hi = lax.Precision.HIGHEST
    L, NB, GB = SSM_CHUNK, SSM_NGBLK, SSM_GBLK
    lr = lam_re.astype(F32)
    li = lam_im.astype(F32)
    dt = jnp.exp(log_dt.astype(F32))[:, None]
    mag = jnp.exp(lr * dt)
    ang = li * dt
    ar = mag * jnp.cos(ang)
    ai = mag * jnp.sin(ang)
    er = ar - 1.0
    ei = ai
    den = lr * lr + li * li
    qr = (er * lr + ei * li) / den
    qi = (ei * lr - er * li) / den
    br = b_re.astype(F32)
    bi = b_im.astype(F32)
    bbr = qr[:, :, None] * br - qi[:, :, None] * bi
    bbi = qr[:, :, None] * bi + qi[:, :, None] * br
    k = jnp.arange(L + 1, dtype=F32)[:, None, None]
    pmag = jnp.exp(lr[None] * dt[None] * k)
    pr = pmag * jnp.cos(ang[None] * k)
    pi = pmag * jnp.sin(ang[None] * k)
    cr = c_re.astype(F32)
    ci = c_im.astype(F32)
    e_r = cr[None] * pr[:, :, None, :] - ci[None] * pi[:, :, None, :]
    e_i = cr[None] * pi[:, :, None, :] + ci[None] * pr[:, :, None, :]
    kern = (jnp.einsum('kgnp,gpm->kgnm', e_r[:L], bbr, precision=hi)
            - jnp.einsum('kgnp,gpm->kgnm', e_i[:L], bbi, precision=hi))
    lag = jnp.arange(L)[None, :] - jnp.arange(L)[:, None]
    toep = jnp.where((lag >= 0)[:, :, None, None, None], kern[jnp.clip(lag, 0, L - 1)], 0.0)
    eye = jnp.eye(GB, dtype=F32)

    toep = toep.reshape(L, L, NB, GB, SSM_GROUP, SSM_GROUP)
    m = jnp.einsum('stbgnm,gh->bsgmthn', toep, eye).reshape(NB, SSM_COLS, SSM_COLS)
    krev = (L - 1) - jnp.arange(L, dtype=F32)[:, None, None]
    rmag = jnp.exp(lr[None] * dt[None] * krev)
    rev_r = rmag * jnp.cos(ang[None] * krev)
    rev_i = rmag * jnp.sin(ang[None] * krev)
    sb_r = rev_r[:, :, :, None] * bbr[None] - rev_i[:, :, :, None] * bbi[None]
    sb_i = rev_r[:, :, :, None] * bbi[None] + rev_i[:, :, :, None] * bbr[None]
    sb_r = sb_r.reshape(L, NB, GB, SSM_STATE, SSM_GROUP)
    sb_i = sb_i.reshape(L, NB, GB, SSM_STATE, SSM_GROUP)
    bre = jnp.einsum('sbgpm,gh->bsgmhp', sb_r, eye).reshape(NB, SSM_COLS, SSM_SCOLS)
    bim = jnp.einsum('sbgpm,gh->bsgmhp', sb_i, eye).reshape(NB, SSM_COLS, SSM_SCOLS)
    ro_r = e_r[1:].reshape(L, NB, GB, SSM_GROUP, SSM_STATE)
    ro_i = e_i[1:].reshape(L, NB, GB, SSM_GROUP, SSM_STATE)
    cre = jnp.einsum('tbgnp,gh->bgpthn', ro_r, eye).reshape(NB, SSM_SCOLS, SSM_COLS)
    cim = -jnp.einsum('tbgnp,gh->bgpthn', ro_i, eye).reshape(NB, SSM_SCOLS, SSM_COLS)
    are = pr[L].reshape(NB, 1, SSM_SCOLS)
    aim = pi[L].reshape(NB, 1, SSM_SCOLS)
    return (m.astype(BF16), bre.astype(BF16), bim.astype(BF16), cre.astype(BF16), cim.astype(BF16),
            are, aim)


def _s5_post_kernel(y_ref, u_ref, d_ref, w_ref, o_ref, tmp_ref):
    nrow = y_ref.shape[1]
    ycat = jnp.concatenate([y_ref[s] for s in range(SSM_CHUNK)], axis=0).astype(F32)
    ucat = jnp.concatenate([u_ref[s] for s in range(SSM_CHUNK)], axis=0).astype(F32)
    y = jax.nn.gelu(ycat + d_ref[...] * ucat, approximate=True)
    z = _dot(y.astype(BF16), w_ref[...])
    out = z[:, :SSM_WIDTH] * jax.nn.sigmoid(z[:, SSM_WIDTH:])
    for s in range(SSM_CHUNK):
        for c in range(SSM_WIDTH // LANES):
            tmp_ref[c, pl.ds(s, nrow, stride=SSM_CHUNK), :] = (
                out[s * nrow:(s + 1) * nrow, c * LANES:(c + 1) * LANES])
    for c in range(SSM_WIDTH // LANES):
        o_ref[:, c * LANES:(c + 1) * LANES] = tmp_ref[c].astype(BF16)


def _s5_post(y_stack, u_stack, d, w_glu):
    tm = TM_POST
    stack_spec = pl.BlockSpec((SSM_CHUNK, tm // SSM_CHUNK, SSM_WIDTH), lambda i: (0, i, 0))
    return pl.pallas_call(
        _s5_post_kernel,
        grid=(TOKENS // tm,),
        in_specs=[
            stack_spec,
            stack_spec,
            pl.BlockSpec((1, SSM_WIDTH), lambda i: (0, 0)),
            pl.BlockSpec((SSM_WIDTH, 2 * SSM_WIDTH), lambda i: (0, 0)),
        ],
        out_specs=pl.BlockSpec((tm, SSM_WIDTH), lambda i: (i, 0)),
        out_shape=jax.ShapeDtypeStruct((TOKENS, SSM_WIDTH), BF16),
        scratch_shapes=[pltpu.VMEM((SSM_WIDTH // LANES, tm, LANES), F32)],
        compiler_params=_params("parallel"),
        name="s5_post",
    )(y_stack, u_stack, d, w_glu)


def _merge_kernel(x_ref, ys_ref, yd_ref, yf_ref, wgl_ref, wbr_ref, wout_ref, g_ref, b_ref, o_ref):
    x = x_ref[...]
    xb = x.astype(BF16)
    merged = None
    for n, y_ref in enumerate((ys_ref, yd_ref, yf_ref)):
        gate = jax.nn.sigmoid(_dot(xb, wgl_ref[:, n * D_MODEL:(n + 1) * D_MODEL]))
        term = gate * _dot(y_ref[...], wbr_ref[n])
        merged = term if merged is None else merged + term
    mix = _dot(merged.astype(BF16), wout_ref[...])
    o_ref[...] = _layer_norm(ALPHA * x + mix, g_ref[...], b_ref[...])


def _merge(x, ys, yd, yf, wgl, wbr, wout, g, b):
    tm = TM_MERGE
    row = lambda i: (i, 0)
    const2 = lambda i: (0, 0)
    return pl.pallas_call(
        _merge_kernel,
        grid=(TOKENS // tm,),
        in_specs=[
            pl.BlockSpec((tm, D_MODEL), row),
            pl.BlockSpec((tm, 512), row),
            pl.BlockSpec((tm, 512), row),
            pl.BlockSpec((tm, 512), row),
            pl.BlockSpec((D_MODEL, N_BRANCH * D_MODEL), const2),
            pl.BlockSpec((N_BRANCH, BRANCH_WIDTH, D_MODEL), lambda i: (0, 0, 0)),
            pl.BlockSpec((D_MODEL, D_MODEL), const2),
            pl.BlockSpec((1, D_MODEL), const2),
            pl.BlockSpec((1, D_MODEL), const2),
        ],
        out_specs=pl.BlockSpec((tm, D_MODEL), row),
        out_shape=jax.ShapeDtypeStruct((TOKENS, D_MODEL), F32),
        compiler_params=_params("parallel"),
        name="merge_ln",
    )(x, ys, yd, yf, wgl, wbr, wout, g, b)


def _router_kernel(x_ref, w_ref, b_ref, o_ref):
    x = x_ref[...]
    xh = x.astype(BF16)
    xl = (x - xh.astype(F32)).astype(BF16)
    hw = _dot(xh, w_ref[...])
    logits = hw[:, :LANES] + (hw[:, LANES:] + _dot(xl, w_ref[:, :LANES])) + b_ref[...]
    lane = lax.broadcasted_iota(jnp.int32, logits.shape, 1).astype(F32)
    big = jnp.float32(LANES)
    neg = jnp.float32(-jnp.inf)

    def first_argmax(v, vmax):
        return jnp.min(jnp.where(v == vmax, lane, big), axis=1, keepdims=True)

    gl = jnp.where(lane < N_EXPERT_GROUPS, logits, neg)
    ge = jnp.exp(gl - jnp.max(gl, axis=1, keepdims=True))
    pg = ge / jnp.sum(ge, axis=1, keepdims=True)
    gp = jnp.max(pg, axis=1, keepdims=True)
    gi = first_argmax(pg, gp)
    base = N_EXPERT_GROUPS + EXPERTS_PER_GROUP * gi
    sel = jnp.logical_and(lane >= base, lane < base + EXPERTS_PER_GROUP)
    el = jnp.where(sel, logits, neg)
    ev1 = jnp.max(el, axis=1, keepdims=True)
    i1 = first_argmax(el, ev1)
    el2 = jnp.where(lane == i1, neg, el)
    ev2 = jnp.max(el2, axis=1, keepdims=True)
    i2 = first_argmax(el2, ev2)
    e2 = jnp.exp(ev2 - ev1)
    den = 1.0 + e2
    w1 = gp * (1.0 / den)
    w2 = gp * (e2 / den)
    o_ref[...] = jnp.where(lane == i1, w1, 0.0) + jnp.where(lane == i2, w2, 0.0)


def _router(x, w, b):
    tm = TM_ROUTE
    return pl.pallas_call(
        _router_kernel,
        grid=(TOKENS // tm,),
        in_specs=[
            pl.BlockSpec((tm, D_MODEL), lambda i: (i, 0)),
            pl.BlockSpec((D_MODEL, 2 * LANES), lambda i: (0, 0)),
            pl.BlockSpec((1, LANES), lambda i: (0, 0)),
        ],
        out_specs=pl.BlockSpec((tm, LANES), lambda i: (i, 0)),
        out_shape=jax.ShapeDtypeStruct((TOKENS, LANES), F32),
        compiler_params=_params("parallel"),
        name="router",
    )(x, w, b)


def _moe_kernel(x_ref, gate_ref, wg_ref, wu_ref, wd_ref, g_ref, b_ref, o_ref, xb_ref, acc_ref):
    grp = pl.program_id(1)

    @pl.when(grp == 0)
    def _():
        xb_ref[...] = x_ref[...].astype(BF16)

    xb = xb_ref[...]
    gates = gate_ref[...]
    lane = lax.broadcasted_iota(jnp.int32, gates.shape, 1)
    first = N_EXPERT_GROUPS + EXPERTS_PER_GROUP * grp
    hidden = []
    for k in range(EXPERTS_PER_GROUP):
        gate = jnp.sum(jnp.where(lane == first + k, gates, 0.0), axis=1, keepdims=True)
        hk = jax.nn.silu(_dot(xb, wg_ref[k])) * _dot(xb, wu_ref[k]) * gate
        hidden.append(hk.astype(BF16))
    contrib = _dot(jnp.concatenate(hidden, axis=1), wd_ref[...])

    @pl.when(grp == 0)
    def _():
        acc_ref[...] = contrib

    @pl.when(jnp.logical_and(grp > 0, grp < N_EXPERT_GROUPS - 1))
    def _():
        acc_ref[...] += contrib

    @pl.when(grp == N_EXPERT_GROUPS - 1)
    def _():
        o_ref[...] = _layer_norm(ALPHA * x_ref[...] + (acc_ref[...] + contrib), g_ref[...], b_ref[...])


def _moe(x, gates, wg, wu, wd, g, b):
    tm = TM_MOE
    wd_grouped = wd.reshape(N_EXPERT_GROUPS, EXPERTS_PER_GROUP * D_EXPERT, D_MODEL)
    return pl.pallas_call(
        _moe_kernel,
        grid=(TOKENS // tm, N_EXPERT_GROUPS),
        in_specs=[
            pl.BlockSpec((tm, D_MODEL), lambda i, e: (i, 0)),
            pl.BlockSpec((tm, LANES), lambda i, e: (i, 0)),
            pl.BlockSpec((EXPERTS_PER_GROUP, D_MODEL, D_EXPERT), lambda i, e: (e, 0, 0)),
            pl.BlockSpec((EXPERTS_PER_GROUP, D_MODEL, D_EXPERT), lambda i, e: (e, 0, 0)),
            pl.BlockSpec((None, EXPERTS_PER_GROUP * D_EXPERT, D_MODEL), lambda i, e: (e, 0, 0)),
            pl.BlockSpec((1, D_MODEL), lambda i, e: (0, 0)),
            pl.BlockSpec((1, D_MODEL), lambda i, e: (0, 0)),
        ],
        out_specs=pl.BlockSpec((tm, D_MODEL), lambda i, e: (i, 0)),
        out_shape=jax.ShapeDtypeStruct((TOKENS, D_MODEL), F32),
        scratch_shapes=[pltpu.VMEM((tm, D_MODEL), BF16), pltpu.VMEM((tm, D_MODEL), F32)],
        compiler_params=_params("parallel", "arbitrary"),
        name="moe_ln",
    )(x, gates, wg, wu, wd_grouped, g, b)


def _rope_tables():
    pos = jnp.arange(SEQ, dtype=F32)
    inv_freq = ROPE_THETA ** (-jnp.arange(0, DIFF_HEAD_DIM, 2, dtype=F32) / DIFF_HEAD_DIM)
    ang = pos[:, None] * inv_freq[None, :]
    emb = jnp.concatenate([ang, ang], axis=-1)
    cos = jnp.cos(emb)
    sin = jnp.sin(emb)
    sign = jnp.where(jnp.arange(DIFF_HEAD_DIM) < DIFF_HEAD_DIM // 2, -1.0, 1.0).astype(F32)
    return jnp.tile(cos, (1, 2)), jnp.tile(sin * sign, (1, 2))


def _split_w_in(w_in_l):
    qscale = DIFF_HEAD_DIM ** -0.5 * LOG2E
    segs = [w_in_l[:, :SSM_WIDTH]]
    for s in range(N_SEG):
        w = w_in_l[:, (s + 1) * 512:(s + 2) * 512]
        if s in (SEG_DQ, SEG_FQ):
            w = w * qscale
        segs.append(w)
    w7 = jnp.stack(segs).astype(BF16)
    off = (N_SEG + 1) * 512
    wff_t = w_in_l[:, off:off + FOX_HEADS].T.astype(BF16)
    wgl = w_in_l[:, off + FOX_HEADS:].astype(BF16)
    return w7, wff_t, wgl


def kernel(x, w_in, w_branch, w_out, ssm_lambda_re, ssm_lambda_im, ssm_log_dt, ssm_b_re, ssm_b_im,
           ssm_c_re, ssm_c_im, ssm_d, ssm_w_glu, diff_lambda, diff_norm_g, fox_f_bias, ln1_g, ln1_b,
           moe_w_group, moe_b_group, moe_w_expert, moe_b_expert, moe_w_gate, moe_w_up, moe_w_down,
           ln2_g, ln2_b):
    cos, sin = _rope_tables()
    h = x.reshape(TOKENS, D_MODEL)
    for l in range(DEPTH):
        lam_init = 0.8 - 0.6 * math.exp(-0.3 * l)
        w7, wff_t, wgl = _split_w_in(w_in[l])
        proj, u_stack, fft = _inproj(h, w7, wff_t, cos, sin)

        mats = _s5_matrices(ssm_lambda_re[l], ssm_lambda_im[l], ssm_log_dt[l], ssm_b_re[l],
                            ssm_b_im[l], ssm_c_re[l], ssm_c_im[l])
        y_stack = _s5_scan(u_stack, mats)
        y_ssm = _s5_post(y_stack, u_stack, ssm_d[l].reshape(1, SSM_WIDTH).astype(F32),
                         ssm_w_glu[l].astype(BF16))

        y_diff = _diff(proj, diff_lambda[l].astype(F32), diff_norm_g[l].reshape(1, 512).astype(F32),
                       lam_init)
        ct, c = _fgate(fft, fox_f_bias[l].reshape(FOX_HEADS, 1).astype(F32))
        y_fox = _fox(proj, ct, c)

        h = _merge(h, y_ssm, y_diff, y_fox, wgl, w_branch[l].astype(BF16), w_out[l].astype(BF16),
                   ln1_g[l].reshape(1, D_MODEL), ln1_b[l].reshape(1, D_MODEL))

        w_r = jnp.concatenate([moe_w_group[l], moe_w_expert[l]], axis=1).astype(F32)
        w_r = jnp.pad(w_r, ((0, 0), (0, LANES - w_r.shape[1])))
        w_hi = w_r.astype(BF16)
        w_r = jnp.concatenate([w_hi, (w_r - w_hi.astype(F32)).astype(BF16)], axis=1)
        b_r = jnp.concatenate([moe_b_group[l], moe_b_expert[l]]).astype(F32)
        b_r = jnp.pad(b_r, (0, LANES - b_r.shape[0])).reshape(1, LANES)
        gates = _router(h, w_r, b_r)
        h = _moe(h, gates, moe_w_gate[l].astype(BF16), moe_w_up[l].astype(BF16),
                 moe_w_down[l].astype(BF16), ln2_g[l].reshape(1, D_MODEL), ln2_b[l].reshape(1, D_MODEL))
    return h.reshape(BATCH, SEQ, D_MODEL)
```

```python
import functools
import math

import jax
import jax.numpy as jnp
from jax import lax
from jax.experimental import pallas as pl
from jax.experimental.pallas import tpu as pltpu

F32 = jnp.float32
BF16 = jnp.bfloat16

D_MODEL = 1024
BATCH = 8
SEQ = 2048
DEPTH = 4
TOKENS = BATCH * SEQ

SSM_WIDTH = 512
SSM_GROUP = 16
SSM_GROUPS = 32
SSM_STATE = 64
SSM_CHUNK = 8
SSM_NCHUNK = SEQ // SSM_CHUNK
SSM_ROWS = BATCH * SSM_NCHUNK
SSM_GBLK = 8
SSM_NGBLK = SSM_GROUPS // SSM_GBLK
SSM_COLS = SSM_CHUNK * 128
SSM_SCOLS = SSM_GBLK * SSM_STATE
SSM_TN = SSM_CHUNK * SSM_GROUP

DIFF_HEADS = 4
DIFF_HEAD_DIM = 64
FOX_HEADS = 8
FOX_HEAD_DIM = 64
BRANCH_WIDTH = 512
N_BRANCH = 3
ROPE_THETA = 10000.0

N_EXPERT_GROUPS = 4
EXPERTS_PER_GROUP = 4
N_EXPERTS = 16
D_EXPERT = 256

ALPHA = (2 * DEPTH) ** 0.25
LN_EPS = 1e-5
RMS_EPS = 1e-6
NEG_INF = -1e30
LOG2E = math.log2(math.e)

LANES = 128
N_SEG = 4
SEG_DQ, SEG_DK, SEG_FQ, SEG_FK = range(N_SEG)
VT_DIFF, VT_FOX = 0, 1
VMEM_LIMIT = 56 * 1024 * 1024

TM_PROJ = 512
TQ = 256
TM_POST = 512
TM_MERGE = 256
TM_ROUTE = 512
TM_MOE = 1024


def _params(*sem):
    return pltpu.CompilerParams(dimension_semantics=sem, vmem_limit_bytes=VMEM_LIMIT)


def _dot(a, b):
    return jnp.dot(a, b, preferred_element_type=F32)


def _dot_nt(a, b):
    return lax.dot_general(a, b, (((1,), (1,)), ((), ())), preferred_element_type=F32)


def _layer_norm(y, g, b):
    mu = jnp.mean(y, axis=-1, keepdims=True)
    d = y - mu
    var = jnp.mean(d * d, axis=-1, keepdims=True)
    return d * lax.rsqrt(var + LN_EPS) * g + b


def _inproj_kernel(x_ref, w_ref, wvt_ref, wff_ref, cos_ref, sin_ref, o_ref, u_ref, vt_ref, fft_ref,
                   tmp_ref):
    tm = x_ref.shape[0]
    xb = x_ref[...].astype(BF16)
    cos = cos_ref[...]
    sin = sin_ref[...]
    lane = lax.broadcasted_iota(jnp.int32, cos.shape, 1)
    first_half = (lane & 32) == 0
    for seg in range(N_SEG):
        acc = _dot(xb, w_ref[seg + 1])
        if seg in (SEG_DQ, SEG_DK):
            for c in range(512 // LANES):
                t = acc[:, c * LANES:(c + 1) * LANES]
                rot = jnp.where(first_half, pltpu.roll(t, LANES - 32, 1), pltpu.roll(t, 32, 1))
                o_ref[seg, :, c * LANES:(c + 1) * LANES] = (t * cos + rot * sin).astype(BF16)
        else:
            o_ref[seg] = acc.astype(BF16)
    for v in range(2):
        vt_ref[v] = _dot_nt(wvt_ref[v], xb).astype(BF16)
    fft_ref[...] = _dot_nt(wff_ref[...], xb)
    u = _dot(xb, w_ref[0])
    for c in range(SSM_WIDTH // LANES):
        tmp_ref[c] = u[:, c * LANES:(c + 1) * LANES]
    for s in range(SSM_CHUNK):
        for c in range(SSM_WIDTH // LANES):
            u_ref[s, :, c * LANES:(c + 1) * LANES] = (
                tmp_ref[c, pl.ds(s, tm // SSM_CHUNK, stride=SSM_CHUNK), :].astype(BF16))


def _inproj(x, w_tok, w_vt, wff_t, cos, sin):
    tm = TM_PROJ
    nrope = SEQ // tm
    return pl.pallas_call(
        _inproj_kernel,
        grid=(TOKENS // tm,),
        in_specs=[
            pl.BlockSpec((tm, D_MODEL), lambda i: (i, 0)),
            pl.BlockSpec((N_SEG + 1, D_MODEL, 512), lambda i: (0, 0, 0)),
            pl.BlockSpec((2, 512, D_MODEL), lambda i: (0, 0, 0)),
            pl.BlockSpec((FOX_HEADS, D_MODEL), lambda i: (0, 0)),
            pl.BlockSpec((tm, LANES), lambda i: (i % nrope, 0)),
            pl.BlockSpec((tm, LANES), lambda i: (i % nrope, 0)),
        ],
        out_specs=[
            pl.BlockSpec((N_SEG, tm, 512), lambda i: (0, i, 0)),
            pl.BlockSpec((SSM_CHUNK, tm // SSM_CHUNK, SSM_WIDTH), lambda i: (0, i, 0)),
            pl.BlockSpec((2, 512, tm), lambda i: (0, 0, i)),
            pl.BlockSpec((FOX_HEADS, tm), lambda i: (0, i)),
        ],
        out_shape=[
            jax.ShapeDtypeStruct((N_SEG, TOKENS, 512), BF16),
            jax.ShapeDtypeStruct((SSM_CHUNK, SSM_ROWS, SSM_WIDTH), BF16),
            jax.ShapeDtypeStruct((2, 512, TOKENS), BF16),
            jax.ShapeDtypeStruct((FOX_HEADS, TOKENS), F32),
        ],
        scratch_shapes=[pltpu.VMEM((SSM_WIDTH // LANES, tm, LANES), F32)],
        compiler_params=_params("parallel"),
        name="inproj",
    )(x, w_tok, w_vt, wff_t, cos, sin)


def _fgate_kernel(fft_ref, bias_ref, ct_ref, c_ref):
    z = fft_ref[...] + bias_ref[...]
    c = jnp.minimum(z, 0.0) - jnp.log1p(jnp.exp(-jnp.abs(z)))
    lane = lax.broadcasted_iota(jnp.int32, c.shape, 1)
    shift = 1
    while shift < SEQ:
        c = c + jnp.where(lane >= shift, pltpu.roll(c, shift, 1), 0.0)
        shift *= 2
    c = c * LOG2E
    ct_ref[...] = c
    padded = jnp.concatenate([c, jnp.zeros((LANES - FOX_HEADS, SEQ), F32)], axis=0)
    c_ref[...] = padded.T


def _fgate(fft, bias):
    return pl.pallas_call(
        _fgate_kernel,
        grid=(BATCH,),
        in_specs=[
            pl.BlockSpec((FOX_HEADS, SEQ), lambda b: (0, b)),
            pl.BlockSpec((FOX_HEADS, 1), lambda b: (0, 0)),
        ],
        out_specs=[
            pl.BlockSpec((FOX_HEADS, SEQ), lambda b: (0, b)),
            pl.BlockSpec((SEQ, LANES), lambda b: (b, 0)),
        ],
        out_shape=[
            jax.ShapeDtypeStruct((FOX_HEADS, TOKENS), F32),
            jax.ShapeDtypeStruct((TOKENS, LANES), F32),
        ],
        compiler_params=_params("parallel"),
        name="fgate",
    )(fft, bias)


def _online_softmax(s, m_ref, l_ref, idx):
    m_old = m_ref[idx]
    m_new = jnp.maximum(m_old, jnp.max(s, axis=0, keepdims=True))
    alpha = jnp.exp2(m_old - m_new)
    p = jnp.exp2(s - m_new)
    l_ref[idx] = alpha * l_ref[idx] + jnp.sum(p, axis=0, keepdims=True)
    m_ref[idx] = m_new
    return p.astype(BF16), alpha


def _causal_mask_t(n):
    key = lax.broadcasted_iota(jnp.int32, (n, n), 0)
    query = lax.broadcasted_iota(jnp.int32, (n, n), 1)
    return key <= query


def _init_softmax_state(m_ref, l_ref, acc_ref):
    m_ref[...] = jnp.full(m_ref.shape, NEG_INF, F32)
    l_ref[...] = jnp.zeros(l_ref.shape, F32)
    acc_ref[...] = jnp.zeros(acc_ref.shape, F32)


def _fox_kernel(q_ref, k_ref, vt_ref, ct_ref, c_ref, o_ref, m_ref, l_ref, acc_ref, ck_ref):
    i = pl.program_id(1)
    tq = q_ref.shape[0]
    hd = FOX_HEAD_DIM
    lane = lax.broadcasted_iota(jnp.int32, (tq, LANES), 1)
    lo = lane < hd
    _init_softmax_state(m_ref, l_ref, acc_ref)

    @pl.when(i == 0)
    def _():
        def fill(j, carry):
            rows = pl.ds(pl.multiple_of(j * tq, tq), tq)
            for h in range(FOX_HEADS):
                ck_ref[h, rows, :] = jnp.broadcast_to(c_ref[rows, h:h + 1], (tq, LANES))
            return carry
        lax.fori_loop(0, SEQ // tq, fill, 0)

    qcols = pl.ds(pl.multiple_of(i * tq, tq), tq)

    def block(j, masked):
        start = pl.multiple_of(j * tq, tq)
        raw = []
        for hp in range(FOX_HEADS // 2):
            cols = slice(hp * LANES, (hp + 1) * LANES)
            q = q_ref[:, cols]
            zero = jnp.zeros_like(q)
            kb = k_ref[pl.ds(start, tq), cols]
            raw.append(_dot_nt(kb, jnp.where(lo, q, zero)))
            raw.append(_dot_nt(kb, jnp.where(lo, zero, q)))
        probs = []
        for h in range(FOX_HEADS):
            cq = ct_ref[h:h + 1, qcols]
            ck = ck_ref[h, pl.ds(start, tq), :]
            s = jnp.concatenate(
                [raw[h][:, c * LANES:(c + 1) * LANES] + (cq[:, c * LANES:(c + 1) * LANES] - ck)
                 for c in range(tq // LANES)], axis=1)
            if masked:
                s = jnp.where(_causal_mask_t(tq), s, NEG_INF)
            probs.append(_online_softmax(s, m_ref, l_ref, h))
        for h in range(FOX_HEADS):
            p, alpha = probs[h]
            rows = slice((h % 2) * hd, (h % 2 + 1) * hd)
            pv = _dot(vt_ref[h * hd:(h + 1) * hd, pl.ds(start, tq)], p)
            acc_ref[h // 2, rows, :] = acc_ref[h // 2, rows, :] * alpha + pv

    def full_block(j, carry):
        block(j, False)
        return carry

    lax.fori_loop(0, i, full_block, 0)
    block(i, True)
    for hp in range(FOX_HEADS // 2):
        oa = acc_ref[hp, :hd, :] / l_ref[2 * hp]
        ob = acc_ref[hp, hd:, :] / l_ref[2 * hp + 1]
        o_ref[:, hp * LANES:(hp + 1) * LANES] = jnp.concatenate([oa, ob], axis=0).T.astype(BF16)


def _fox(proj, vt, ct, c):
    nq = SEQ // TQ
    return pl.pallas_call(
        _fox_kernel,
        grid=(BATCH, nq),
        in_specs=[
            pl.BlockSpec((None, TQ, 512), lambda b, i: (SEG_FQ, b * nq + i, 0)),
            pl.BlockSpec((None, SEQ, 512), lambda b, i: (SEG_FK, b, 0)),
            pl.BlockSpec((None, 512, SEQ), lambda b, i: (VT_FOX, 0, b)),
            pl.BlockSpec((FOX_HEADS, SEQ), lambda b, i: (0, b)),
            pl.BlockSpec((SEQ, LANES), lambda b, i: (b, 0)),
        ],
        out_specs=pl.BlockSpec((TQ, 512), lambda b, i: (b * nq + i, 0)),
        out_shape=jax.ShapeDtypeStruct((TOKENS, 512), BF16),
        scratch_shapes=[pltpu.VMEM((FOX_HEADS, 1, TQ), F32),
                        pltpu.VMEM((FOX_HEADS, 1, TQ), F32),
                        pltpu.VMEM((FOX_HEADS // 2, LANES, TQ), F32),
                        pltpu.VMEM((FOX_HEADS, SEQ, LANES), F32)],
        compiler_params=_params("parallel", "arbitrary"),
        name="fox_attn",
    )(proj, proj, vt, ct, c)


def _diff_kernel(lam_init, q_ref, k_ref, vt_ref, lam_ref, g_ref, o_ref, m_ref, l_ref, acc_ref):
    i = pl.program_id(1)
    tq = q_ref.shape[0]
    lane = lax.broadcasted_iota(jnp.int32, (tq, LANES), 1)
    lo = lane < DIFF_HEAD_DIM
    _init_softmax_state(m_ref, l_ref, acc_ref)

    def block(j, masked):
        start = pl.multiple_of(j * tq, tq)
        scores = []
        for h in range(DIFF_HEADS):
            cols = slice(h * LANES, (h + 1) * LANES)
            q = q_ref[:, cols]
            zero = jnp.zeros_like(q)
            kb = k_ref[pl.ds(start, tq), cols]
            scores.append(_dot_nt(kb, jnp.where(lo, q, zero)))
            scores.append(_dot_nt(kb, jnp.where(lo, zero, q)))
        probs = []
        for idx in range(2 * DIFF_HEADS):
            s = scores[idx]
            if masked:
                s = jnp.where(_causal_mask_t(tq), s, NEG_INF)
            probs.append(_online_softmax(s, m_ref, l_ref, idx))
        for idx in range(2 * DIFF_HEADS):
            p, alpha = probs[idx]
            h = idx // 2
            vtb = vt_ref[h * LANES:(h + 1) * LANES, pl.ds(start, tq)]
            acc_ref[idx] = acc_ref[idx] * alpha + _dot(vtb, p)

    def full_block(j, carry):
        block(j, False)
        return carry

    lax.fori_loop(0, i, full_block, 0)
    block(i, True)
    lam_rows = lam_ref[...]
    lam = (jnp.exp(jnp.sum(lam_rows[0:1] * lam_rows[1:2], axis=1, keepdims=True))
           - jnp.exp(jnp.sum(lam_rows[2:3] * lam_rows[3:4], axis=1, keepdims=True)) + lam_init)
    for h in range(DIFF_HEADS):
        cols = slice(h * LANES, (h + 1) * LANES)
        o = acc_ref[2 * h] / l_ref[2 * h] - lam * (acc_ref[2 * h + 1] / l_ref[2 * h + 1])
        o = o * lax.rsqrt(jnp.mean(o * o, axis=0, keepdims=True) + RMS_EPS)
        o_ref[:, cols] = (o.T * g_ref[:, cols] * (1.0 - lam_init)).astype(BF16)


def _diff(proj, vt, lam, g, lam_init):
    nq = SEQ // TQ
    return pl.pallas_call(
        functools.partial(_diff_kernel, lam_init),
        grid=(BATCH, nq),
        in_specs=[
            pl.BlockSpec((None, TQ, 512), lambda b, i: (SEG_DQ, b * nq + i, 0)),
            pl.BlockSpec((None, SEQ, 512), lambda b, i: (SEG_DK, b, 0)),
            pl.BlockSpec((None, 512, SEQ), lambda b, i: (VT_DIFF, 0, b)),
            pl.BlockSpec((4, DIFF_HEAD_DIM), lambda b, i: (0, 0)),
            pl.BlockSpec((1, 512), lambda b, i: (0, 0)),
        ],
        out_specs=pl.BlockSpec((TQ, 512), lambda b, i: (b * nq + i, 0)),
        out_shape=jax.ShapeDtypeStruct((TOKENS, 512), BF16),
        scratch_shapes=[pltpu.VMEM((2 * DIFF_HEADS, 1, TQ), F32),
                        pltpu.VMEM((2 * DIFF_HEADS, 1, TQ), F32),
                        pltpu.VMEM((2 * DIFF_HEADS, LANES, TQ), F32)],
        compiler_params=_params("parallel", "arbitrary"),
        name="diff_attn",
    )(proj, proj, vt, lam, g)


def _spread(compact, select, row_group, col_group):
    full = _dot(compact, select)
    return jnp.where(row_group == col_group, full, 0.0).astype(BF16)


def _s5_kernel(u_ref, am_ref, abre_ref, abim_ref, acre_ref, acim_ref, are_ref, aim_ref, y_ref, *state):
    nslab = SSM_SCOLS // LANES
    sre, sim = state[:nslab], state[nslab:]

    def iota(shape, dim):
        return lax.broadcasted_iota(jnp.int32, shape, dim)

    i_tn, j_cols = iota((SSM_TN, 1), 0), iota((1, SSM_COLS), 1)
    sel_tn = jnp.where(jnp.logical_and((i_tn >> 4) == (j_cols >> 7), (i_tn & 15) == (j_cols & 15)),
                       1.0, 0.0).astype(BF16)
    i_p, j_sc = iota((SSM_STATE, 1), 0), iota((1, SSM_SCOLS), 1)
    sel_p = jnp.where(i_p == (j_sc & (SSM_STATE - 1)), 1.0, 0.0).astype(BF16)
    grp_rows_cols = (iota((SSM_COLS, 1), 0) >> 4) & (SSM_GBLK - 1)
    grp_cols_cols = (j_cols >> 4) & (SSM_GBLK - 1)
    grp_rows_state = iota((SSM_SCOLS, 1), 0) >> 6
    grp_cols_state = j_sc >> 6
    m = _spread(am_ref[...], sel_tn, grp_rows_cols, grp_cols_cols)
    bre = _spread(abre_ref[...], sel_p, grp_rows_cols, grp_cols_state)
    bim = _spread(abim_ref[...], sel_p, grp_rows_cols, grp_cols_state)
    cre = _spread(acre_ref[...], sel_tn, grp_rows_state, grp_cols_cols)
    cim = _spread(acim_ref[...], sel_tn, grp_rows_state, grp_cols_cols)

    u = jnp.concatenate([u_ref[s] for s in range(SSM_CHUNK)], axis=1)
    for k in range(0, nslab, 2):
        cols = slice(k * LANES, (k + 2) * LANES)
        for dst, b in ((sre, bre), (sim, bim)):
            part = _dot(u, b[:, cols])
            dst[k][...] = part[:, :LANES]
            dst[k + 1][...] = part[:, LANES:]
    ar = [are_ref[:, k * LANES:(k + 1) * LANES] for k in range(nslab)]
    ai = [aim_ref[:, k * LANES:(k + 1) * LANES] for k in range(nslab)]

    def body(c, carry):
        rows = pl.ds(c, BATCH, stride=SSM_NCHUNK)
        nxt = []
        for k in range(nslab):
            xr, xi = carry[k]
            sr = sre[k][rows, :]
            si = sim[k][rows, :]
            sre[k][rows, :] = xr
            sim[k][rows, :] = xi
            nxt.append((ar[k] * xr - ai[k] * xi + sr, ar[k] * xi + ai[k] * xr + si))
        return tuple(nxt)

    zero = jnp.zeros((BATCH, LANES), F32)
    lax.fori_loop(0, SSM_NCHUNK, body, tuple((zero, zero) for _ in range(nslab)))
    xr = jnp.concatenate([r[...] for r in sre], axis=1).astype(BF16)
    xi = jnp.concatenate([r[...] for r in sim], axis=1).astype(BF16)
    for c in range(SSM_COLS // (2 * LANES)):
        cols = slice(c * 2 * LANES, (c + 1) * 2 * LANES)
        y = _dot(u, m[:, cols]) + _dot(xr, cre[:, cols]) + _dot(xi, cim[:, cols])
        y_ref[2 * c] = y[:, :LANES].astype(BF16)
        y_ref[2 * c + 1] = y[:, LANES:].astype(BF16)


def _s5_scan(u_stack, mats):
    am, abre, abim, acre, acim, are, aim = mats

    def spec(*shape):
        return pl.BlockSpec((None,) + shape, lambda g: (g,) + (0,) * len(shape))

    io_spec = pl.BlockSpec((SSM_CHUNK, SSM_ROWS, LANES), lambda g: (0, 0, g))
    return pl.pallas_call(
        _s5_kernel,
        grid=(SSM_NGBLK,),
        in_specs=[io_spec, spec(SSM_COLS, SSM_TN),
                  spec(SSM_COLS, SSM_STATE), spec(SSM_COLS, SSM_STATE),
                  spec(SSM_SCOLS, SSM_TN), spec(SSM_SCOLS, SSM_TN),
                  spec(1, SSM_SCOLS), spec(1, SSM_SCOLS)],
        out_specs=io_spec,
        out_shape=jax.ShapeDtypeStruct((SSM_CHUNK, SSM_ROWS, SSM_WIDTH), BF16),
        scratch_shapes=[pltpu.VMEM((SSM_ROWS, LANES), F32) for _ in range(2 * SSM_SCOLS // LANES)],
        compiler_params=_params("parallel"),
        name="s5_scan",
    )(u_stack, am, abre, abim, acre, acim, are, aim)


def _s5_matrices(lam_re, lam_im, log_dt, b_re, b_im, c_re, c_im):
    hi = lax.Precision.HIGHEST
    L, NB, GB = SSM_CHUNK, SSM_NGBLK, SSM_GBLK
    lr = lam_re.astype(F32)
    li = lam_im.astype(F32)
    dt = jnp.exp(log_dt.astype(F32))[:, None]
    mag = jnp.exp(lr * dt)
    ang = li * dt
    ar = mag * jnp.cos(ang)
    ai = mag * jnp.sin(ang)
    er = ar - 1.0
    ei = ai
    den = lr * lr + li * li
    qr = (er * lr + ei * li) / den
    qi = (ei * lr - er * li) / den
    br = b_re.astype(F32)
    bi = b_im.astype(F32)
    bbr = qr[:, :, None] * br - qi[:, :, None] * bi
    bbi = qr[:, :, None] * bi + qi[:, :, None] * br
    k = jnp.arange(L + 1, dtype=F32)[:, None, None]
    pmag = jnp.exp(lr[None] * dt[None] * k)
    pr = pmag * jnp.cos(ang[None] * k)
    pi = pmag * jnp.sin(ang[None] * k)
    cr = c_re.astype(F32)
    ci = c_im.astype(F32)
    e_r = cr[None] * pr[:, :, None, :] - ci[None] * pi[:, :, None, :]
    e_i = cr[None] * pi[:, :, None, :] + ci[None] * pr[:, :, None, :]
    kern = (jnp.einsum('kgnp,gpm->kgnm', e_r[:L], bbr, precision=hi)
            - jnp.einsum('kgnp,gpm->kgnm', e_i[:L], bbi, precision=hi))
    lag = jnp.arange(L)[None, :] - jnp.arange(L)[:, None]
    toep = jnp.where((lag >= 0)[:, :, None, None, None], kern[jnp.clip(lag, 0, L - 1)], 0.0)
    toep = toep.reshape(L, L, NB, GB, SSM_GROUP, SSM_GROUP)
    am = jnp.transpose(toep, (2, 0, 3, 5, 1, 4)).reshape(NB, SSM_COLS, SSM_TN)
    krev = (L - 1) - jnp.arange(L, dtype=F32)[:, None, None]
    rmag = jnp.exp(lr[None] * dt[None] * krev)
    rev_r = rmag * jnp.cos(ang[None] * krev)
    rev_i = rmag * jnp.sin(ang[None] * krev)
    sb_r = rev_r[:, :, :, None] * bbr[None] - rev_i[:, :, :, None] * bbi[None]
    sb_i = rev_r[:, :, :, None] * bbi[None] + rev_i[:, :, :, None] * bbr[None]

    def state_in(t):
        t = t.reshape(L, NB, GB, SSM_STATE, SSM_GROUP)
        return jnp.transpose(t, (1, 0, 2, 4, 3)).reshape(NB, SSM_COLS, SSM_STATE)

    def state_out(t):
        t = t.reshape(L, NB, GB, SSM_GROUP, SSM_STATE)
        return jnp.transpose(t, (1, 2, 4, 0, 3)).reshape(NB, SSM_SCOLS, SSM_TN)

    are = pr[L].reshape(NB, 1, SSM_SCOLS)
    aim = pi[L].reshape(NB, 1, SSM_SCOLS)
    return (am.astype(BF16), state_in(sb_r).astype(BF16), state_in(sb_i).astype(BF16),
            state_out(e_r[1:]).astype(BF16), (-state_out(e_i[1:])).astype(BF16), are, aim)


def _s5_post_kernel(y_ref, u_ref, d_ref, w_ref, o_ref, tmp_ref):
    nrow = y_ref.shape[1]
    ycat = jnp.concatenate([y_ref[s] for s in range(SSM_CHUNK)], axis=0).astype(F32)
    ucat = jnp.concatenate([u_ref[s] for s in range(SSM_CHUNK)], axis=0).astype(F32)
    y = jax.nn.gelu(ycat + d_ref[...] * ucat, approximate=True)
    z = _dot(y.astype(BF16), w_ref[...])
    out = z[:, :SSM_WIDTH] * jax.nn.sigmoid(z[:, SSM_WIDTH:])
    for s in range(SSM_CHUNK):
        for c in range(SSM_WIDTH // LANES):
            tmp_ref[c, pl.ds(s, nrow, stride=SSM_CHUNK), :] = (
                out[s * nrow:(s + 1) * nrow, c * LANES:(c + 1) * LANES])
    for c in range(SSM_WIDTH // LANES):
        o_ref[:, c * LANES:(c + 1) * LANES] = tmp_ref[c].astype(BF16)


def _s5_post(y_stack, u_stack, d, w_glu):
    tm = TM_POST
    stack_spec = pl.BlockSpec((SSM_CHUNK, tm // SSM_CHUNK, SSM_WIDTH), lambda i: (0, i, 0))
    return pl.pallas_call(
        _s5_post_kernel,
        grid=(TOKENS // tm,),
        in_specs=[
            stack_spec,
            stack_spec,
            pl.BlockSpec((1, SSM_WIDTH), lambda i: (0, 0)),
            pl.BlockSpec((SSM_WIDTH, 2 * SSM_WIDTH), lambda i: (0, 0)),
        ],
        out_specs=pl.BlockSpec((tm, SSM_WIDTH), lambda i: (i, 0)),
        out_shape=jax.ShapeDtypeStruct((TOKENS, SSM_WIDTH), BF16),
        scratch_shapes=[pltpu.VMEM((SSM_WIDTH // LANES, tm, LANES), F32)],
        compiler_params=_params("parallel"),
        name="s5_post",
    )(y_stack, u_stack, d, w_glu)


def _merge_kernel(x_ref, ys_ref, yd_ref, yf_ref, wgl_ref, wbr_ref, wout_ref, g_ref, b_ref, o_ref):
    x = x_ref[...]
    xb = x.astype(BF16)
    merged = None
    for n, y_ref in enumerate((ys_ref, yd_ref, yf_ref)):
        gate = jax.nn.sigmoid(_dot(xb, wgl_ref[:, n * D_MODEL:(n + 1) * D_MODEL]))
        term = gate * _dot(y_ref[...], wbr_ref[n])
        merged = term if merged is None else merged + term
    mix = _dot(merged.astype(BF16), wout_ref[...])
    o_ref[...] = _layer_norm(ALPHA * x + mix, g_ref[...], b_ref[...])


def _merge(x, ys, yd, yf, wgl, wbr, wout, g, b):
    tm = TM_MERGE
    row = lambda i: (i, 0)
    const2 = lambda i: (0, 0)
    return pl.pallas_call(
        _merge_kernel,
        grid=(TOKENS // tm,),
        in_specs=[
            pl.BlockSpec((tm, D_MODEL), row),
            pl.BlockSpec((tm, 512), row),
            pl.BlockSpec((tm, 512), row),
            pl.BlockSpec((tm, 512), row),
            pl.BlockSpec((D_MODEL, N_BRANCH * D_MODEL), const2),
            pl.BlockSpec((N_BRANCH, BRANCH_WIDTH, D_MODEL), lambda i: (0, 0, 0)),
            pl.BlockSpec((D_MODEL, D_MODEL), const2),
            pl.BlockSpec((1, D_MODEL), const2),
            pl.BlockSpec((1, D_MODEL), const2),
        ],
        out_specs=pl.BlockSpec((tm, D_MODEL), row),
        out_shape=jax.ShapeDtypeStruct((TOKENS, D_MODEL), F32),
        compiler_params=_params("parallel"),
        name="merge_ln",
    )(x, ys, yd, yf, wgl, wbr, wout, g, b)


def _router_kernel(x_ref, w_ref, b_ref, o_ref):
    x = x_ref[...]
    xh = x.astype(BF16)
    xl = (x - xh.astype(F32)).astype(BF16)
    hw = _dot(xh, w_ref[...])
    logits = hw[:, :LANES] + (hw[:, LANES:] + _dot(xl, w_ref[:, :LANES])) + b_ref[...]
    lane = lax.broadcasted_iota(jnp.int32, logits.shape, 1).astype(F32)
    big = jnp.float32(LANES)
    neg = jnp.float32(-jnp.inf)

    def first_argmax(v, vmax):
        return jnp.min(jnp.where(v == vmax, lane, big), axis=1, keepdims=True)

    gl = jnp.where(lane < N_EXPERT_GROUPS, logits, neg)
    ge = jnp.exp(gl - jnp.max(gl, axis=1, keepdims=True))
    pg = ge / jnp.sum(ge, axis=1, keepdims=True)
    gp = jnp.max(pg, axis=1, keepdims=True)
    gi = first_argmax(pg, gp)
    base = N_EXPERT_GROUPS + EXPERTS_PER_GROUP * gi
    sel = jnp.logical_and(lane >= base, lane < base + EXPERTS_PER_GROUP)
    el = jnp.where(sel, logits, neg)
    ev1 = jnp.max(el, axis=1, keepdims=True)
    i1 = first_argmax(el, ev1)
    el2 = jnp.where(lane == i1, neg, el)
    ev2 = jnp.max(el2, axis=1, keepdims=True)
    i2 = first_argmax(el2, ev2)
    e2 = jnp.exp(ev2 - ev1)
    den = 1.0 + e2
    w1 = gp * (1.0 / den)
    w2 = gp * (e2 / den)
    o_ref[...] = jnp.where(lane == i1, w1, 0.0) + jnp.where(lane == i2, w2, 0.0)


def _router(x, w, b):
    tm = TM_ROUTE
    return pl.pallas_call(
        _router_kernel,
        grid=(TOKENS // tm,),
        in_specs=[
            pl.BlockSpec((tm, D_MODEL), lambda i: (i, 0)),
            pl.BlockSpec((D_MODEL, 2 * LANES), lambda i: (0, 0)),
            pl.BlockSpec((1, LANES), lambda i: (0, 0)),
        ],
        out_specs=pl.BlockSpec((tm, LANES), lambda i: (i, 0)),
        out_shape=jax.ShapeDtypeStruct((TOKENS, LANES), F32),
        compiler_params=_params("parallel"),
        name="router",
    )(x, w, b)


def _moe_kernel(x_ref, gate_ref, wg_ref, wu_ref, wd_ref, g_ref, b_ref, o_ref, xb_ref, acc_ref):
    grp = pl.program_id(1)

    @pl.when(grp == 0)
    def _():
        xb_ref[...] = x_ref[...].astype(BF16)

    xb = xb_ref[...]
    gates = gate_ref[...]
    lane = lax.broadcasted_iota(jnp.int32, gates.shape, 1)
    first = N_EXPERT_GROUPS + EXPERTS_PER_GROUP * grp
    hidden = []
    for k in range(EXPERTS_PER_GROUP):
        gate = jnp.sum(jnp.where(lane == first + k, gates, 0.0), axis=1, keepdims=True)
        hk = jax.nn.silu(_dot(xb, wg_ref[k])) * _dot(xb, wu_ref[k]) * gate
        hidden.append(hk.astype(BF16))
    contrib = _dot(jnp.concatenate(hidden, axis=1), wd_ref[...])

    @pl.when(grp == 0)
    def _():
        acc_ref[...] = contrib

    @pl.when(jnp.logical_and(grp > 0, grp < N_EXPERT_GROUPS - 1))
    def _():
        acc_ref[...] += contrib

    @pl.when(grp == N_EXPERT_GROUPS - 1)
    def _():
        o_ref[...] = _layer_norm(ALPHA * x_ref[...] + (acc_ref[...] + contrib), g_ref[...], b_ref[...])


def _moe(x, gates, wg, wu, wd, g, b):
    tm = TM_MOE
    wd_grouped = wd.reshape(N_EXPERT_GROUPS, EXPERTS_PER_GROUP * D_EXPERT, D_MODEL)
    return pl.pallas_call(
        _moe_kernel,
        grid=(TOKENS // tm, N_EXPERT_GROUPS),
        in_specs=[
            pl.BlockSpec((tm, D_MODEL), lambda i, e: (i, 0)),
            pl.BlockSpec((tm, LANES), lambda i, e: (i, 0)),
            pl.BlockSpec((EXPERTS_PER_GROUP, D_MODEL, D_EXPERT), lambda i, e: (e, 0, 0)),
            pl.BlockSpec((EXPERTS_PER_GROUP, D_MODEL, D_EXPERT), lambda i, e: (e, 0, 0)),
            pl.BlockSpec((None, EXPERTS_PER_GROUP * D_EXPERT, D_MODEL), lambda i, e: (e, 0, 0)),
            pl.BlockSpec((1, D_MODEL), lambda i, e: (0, 0)),
            pl.BlockSpec((1, D_MODEL), lambda i, e: (0, 0)),
        ],
        out_specs=pl.BlockSpec((tm, D_MODEL), lambda i, e: (i, 0)),
        out_shape=jax.ShapeDtypeStruct((TOKENS, D_MODEL), F32),
        scratch_shapes=[pltpu.VMEM((tm, D_MODEL), BF16), pltpu.VMEM((tm, D_MODEL), F32)],
        compiler_params=_params("parallel", "arbitrary"),
        name="moe_ln",
    )(x, gates, wg, wu, wd_grouped, g, b)


def _rope_tables():
    pos = jnp.arange(SEQ, dtype=F32)
    inv_freq = ROPE_THETA ** (-jnp.arange(0, DIFF_HEAD_DIM, 2, dtype=F32) / DIFF_HEAD_DIM)
    ang = pos[:, None] * inv_freq[None, :]
    emb = jnp.concatenate([ang, ang], axis=-1)
    cos = jnp.cos(emb)
    sin = jnp.sin(emb)
    sign = jnp.where(jnp.arange(DIFF_HEAD_DIM) < DIFF_HEAD_DIM // 2, -1.0, 1.0).astype(F32)
    return jnp.tile(cos, (1, 2)), jnp.tile(sin * sign, (1, 2))


def _split_w_in(w_in):
    qscale = DIFF_HEAD_DIM ** -0.5 * LOG2E

    def seg(k):
        return w_in[:, :, k * 512:(k + 1) * 512]

    w_tok = jnp.stack([seg(0), seg(1) * qscale, seg(2), seg(4) * qscale, seg(5)], axis=1).astype(BF16)
    w_vt = jnp.stack([jnp.swapaxes(seg(3), 1, 2), jnp.swapaxes(seg(6), 1, 2)], axis=1).astype(BF16)
    off = 7 * 512
    wff_t = jnp.swapaxes(w_in[:, :, off:off + FOX_HEADS], 1, 2).astype(BF16)
    wgl = w_in[:, :, off + FOX_HEADS:].astype(BF16)
    return w_tok, w_vt, wff_t, wgl


def kernel(x, w_in, w_branch, w_out, ssm_lambda_re, ssm_lambda_im, ssm_log_dt, ssm_b_re, ssm_b_im,
           ssm_c_re, ssm_c_im, ssm_d, ssm_w_glu, diff_lambda, diff_norm_g, fox_f_bias, ln1_g, ln1_b,
           moe_w_group, moe_b_group, moe_w_expert, moe_b_expert, moe_w_gate, moe_w_up, moe_w_down,
           ln2_g, ln2_b):
    cos, sin = _rope_tables()
    w_tok, w_vt, wff_t, wgl = _split_w_in(w_in)
    s5_mats = jax.vmap(_s5_matrices)(ssm_lambda_re, ssm_lambda_im, ssm_log_dt, ssm_b_re, ssm_b_im,
                                     ssm_c_re, ssm_c_im)
    w_glu = ssm_w_glu.astype(BF16)
    w_br = w_branch.astype(BF16)
    w_o = w_out.astype(BF16)
    w_gate, w_up, w_down = moe_w_gate.astype(BF16), moe_w_up.astype(BF16), moe_w_down.astype(BF16)
    w_r = jnp.concatenate([moe_w_group, moe_w_expert], axis=2).astype(F32)
    w_r = jnp.pad(w_r, ((0, 0), (0, 0), (0, LANES - w_r.shape[2])))
    w_r_hi = w_r.astype(BF16)
    w_r = jnp.concatenate([w_r_hi, (w_r - w_r_hi.astype(F32)).astype(BF16)], axis=2)
    b_r = jnp.concatenate([moe_b_group, moe_b_expert], axis=1).astype(F32)
    b_r = jnp.pad(b_r, ((0, 0), (0, LANES - b_r.shape[1]))).reshape(DEPTH, 1, LANES)

    h = x.reshape(TOKENS, D_MODEL)
    for l in range(DEPTH):
        lam_init = 0.8 - 0.6 * math.exp(-0.3 * l)
        proj, u_stack, vt, fft = _inproj(h, w_tok[l], w_vt[l], wff_t[l], cos, sin)

        y_stack = _s5_scan(u_stack, tuple(t[l] for t in s5_mats))
        y_ssm = _s5_post(y_stack, u_stack, ssm_d[l].reshape(1, SSM_WIDTH).astype(F32), w_glu[l])

        y_diff = _diff(proj, vt, diff_lambda[l].astype(F32),
                       diff_norm_g[l].reshape(1, 512).astype(F32), lam_init)
        ct, c = _fgate(fft, fox_f_bias[l].reshape(FOX_HEADS, 1).astype(F32))
        y_fox = _fox(proj, vt, ct, c)

        h = _merge(h, y_ssm, y_diff, y_fox, wgl[l], w_br[l], w_o[l],
                   ln1_g[l].reshape(1, D_MODEL), ln1_b[l].reshape(1, D_MODEL))

        gates = _router(h, w_r[l], b_r[l])
        h = _moe(h, gates, w_gate[l], w_up[l], w_down[l],
                 ln2_g[l].reshape(1, D_MODEL), ln2_b[l].reshape(1, D_MODEL))
    return h.reshape(BATCH, SEQ, D_MODEL)
```

```python
import functools
import math

import jax
import jax.numpy as jnp
from jax import lax
from jax.experimental import pallas as pl
from jax.experimental.pallas import tpu as pltpu

F32 = jnp.float32
BF16 = jnp.bfloat16

D_MODEL = 1024
BATCH = 8
SEQ = 2048
DEPTH = 4
TOKENS = BATCH * SEQ

SSM_WIDTH = 512
SSM_GROUP = 16
SSM_GROUPS = 32
SSM_STATE = 64
SSM_CHUNK = 8
SSM_NCHUNK = SEQ // SSM_CHUNK
SSM_ROWS = BATCH * SSM_NCHUNK
SSM_GBLK = 8
SSM_NGBLK = SSM_GROUPS // SSM_GBLK
SSM_COLS = SSM_CHUNK * 128
SSM_SCOLS = SSM_GBLK * SSM_STATE
SSM_TN = SSM_CHUNK * SSM_GROUP

DIFF_HEADS = 4
DIFF_HEAD_DIM = 64
FOX_HEADS = 8
FOX_HEAD_DIM = 64
BRANCH_WIDTH = 512
N_BRANCH = 3
ROPE_THETA = 10000.0

N_EXPERT_GROUPS = 4
EXPERTS_PER_GROUP = 4
N_EXPERTS = 16
D_EXPERT = 256

ALPHA = (2 * DEPTH) ** 0.25
LN_EPS = 1e-5
RMS_EPS = 1e-6
NEG_INF = -1e30
LOG2E = math.log2(math.e)

LANES = 128
N_SEG = 4
SEG_DQ, SEG_DK, SEG_FQ, SEG_FK = range(N_SEG)
VT_DIFF, VT_FOX = 0, 1
VMEM_LIMIT = 56 * 1024 * 1024

TM_PROJ = 512
TQ = 256
TM_POST = 512
TM_MERGE = 256
TM_ROUTE = 512
TM_MOE = 1024


def _params(*sem):
    return pltpu.CompilerParams(dimension_semantics=sem, vmem_limit_bytes=VMEM_LIMIT)


def _dot(a, b):
    return jnp.dot(a, b, preferred_element_type=F32)


def _dot_nt(a, b):
    return lax.dot_general(a, b, (((1,), (1,)), ((), ())), preferred_element_type=F32)


def _layer_norm(y, g, b):
    mu = jnp.mean(y, axis=-1, keepdims=True)
    d = y - mu
    var = jnp.mean(d * d, axis=-1, keepdims=True)
    return d * lax.rsqrt(var + LN_EPS) * g + b


def _inproj_kernel(x_ref, w_ref, wvt_ref, wff_ref, cos_ref, sin_ref, o_ref, u_ref, vt_ref, fft_ref,
                   tmp_ref):
    tm = x_ref.shape[0]
    xb = x_ref[...].astype(BF16)
    cos = cos_ref[...]
    sin = sin_ref[...]
    lane = lax.broadcasted_iota(jnp.int32, cos.shape, 1)
    first_half = (lane & 32) == 0
    for seg in range(N_SEG):
        acc = _dot(xb, w_ref[seg + 1])
        if seg in (SEG_DQ, SEG_DK):
            for c in range(512 // LANES):
                t = acc[:, c * LANES:(c + 1) * LANES]
                rot = jnp.where(first_half, pltpu.roll(t, LANES - 32, 1), pltpu.roll(t, 32, 1))
                o_ref[seg, :, c * LANES:(c + 1) * LANES] = (t * cos + rot * sin).astype(BF16)
        else:
            o_ref[seg] = acc.astype(BF16)
    for v in range(2):
        vt_ref[v] = _dot_nt(wvt_ref[v], xb).astype(BF16)
    fft_ref[...] = _dot_nt(wff_ref[...], xb)
    u = _dot(xb, w_ref[0])
    for c in range(SSM_WIDTH // LANES):
        tmp_ref[c] = u[:, c * LANES:(c + 1) * LANES]
    for s in range(SSM_CHUNK):
        for c in range(SSM_WIDTH // LANES):
            u_ref[s, :, c * LANES:(c + 1) * LANES] = (
                tmp_ref[c, pl.ds(s, tm // SSM_CHUNK, stride=SSM_CHUNK), :].astype(BF16))


def _inproj(x, w_tok, w_vt, wff_t, cos, sin):
    tm = TM_PROJ
    nrope = SEQ // tm
    return pl.pallas_call(
        _inproj_kernel,
        grid=(TOKENS // tm,),
        in_specs=[
            pl.BlockSpec((tm, D_MODEL), lambda i: (i, 0)),
            pl.BlockSpec((N_SEG + 1, D_MODEL, 512), lambda i: (0, 0, 0)),
            pl.BlockSpec((2, 512, D_MODEL), lambda i: (0, 0, 0)),
            pl.BlockSpec((FOX_HEADS, D_MODEL), lambda i: (0, 0)),
            pl.BlockSpec((tm, LANES), lambda i: (i % nrope, 0)),
            pl.BlockSpec((tm, LANES), lambda i: (i % nrope, 0)),
        ],
        out_specs=[
            pl.BlockSpec((N_SEG, tm, 512), lambda i: (0, i, 0)),
            pl.BlockSpec((SSM_CHUNK, tm // SSM_CHUNK, SSM_WIDTH), lambda i: (0, i, 0)),
            pl.BlockSpec((2, 512, tm), lambda i: (0, 0, i)),
            pl.BlockSpec((FOX_HEADS, tm), lambda i: (0, i)),
        ],
        out_shape=[
            jax.ShapeDtypeStruct((N_SEG, TOKENS, 512), BF16),
            jax.ShapeDtypeStruct((SSM_CHUNK, SSM_ROWS, SSM_WIDTH), BF16),
            jax.ShapeDtypeStruct((2, 512, TOKENS), BF16),
            jax.ShapeDtypeStruct((FOX_HEADS, TOKENS), F32),
        ],
        scratch_shapes=[pltpu.VMEM((SSM_WIDTH // LANES, tm, LANES), F32)],
        compiler_params=_params("parallel"),
        name="inproj",
    )(x, w_tok, w_vt, wff_t, cos, sin)


def _fgate_kernel(fft_ref, bias_ref, c_ref):
    z = fft_ref[...] + bias_ref[...]
    c = jnp.minimum(z, 0.0) - jnp.log1p(jnp.exp(-jnp.abs(z)))
    lane = lax.broadcasted_iota(jnp.int32, c.shape, 1)
    shift = 1
    while shift < SEQ:
        c = c + jnp.where(lane >= shift, pltpu.roll(c, shift, 1), 0.0)
        shift *= 2
    c = c * LOG2E
    padded = jnp.concatenate([c, jnp.zeros((LANES - FOX_HEADS, SEQ), F32)], axis=0)
    c_ref[...] = padded.T


def _fgate(fft, bias):
    return pl.pallas_call(
        _fgate_kernel,
        grid=(BATCH,),
        in_specs=[
            pl.BlockSpec((FOX_HEADS, SEQ), lambda b: (0, b)),
            pl.BlockSpec((FOX_HEADS, 1), lambda b: (0, 0)),
        ],
        out_specs=pl.BlockSpec((SEQ, LANES), lambda b: (b, 0)),
        out_shape=jax.ShapeDtypeStruct((TOKENS, LANES), F32),
        compiler_params=_params("parallel"),
        name="fgate",
    )(fft, bias)


ONES_ROWS = 16


def _online_softmax(s, m_ref, idx):
    m_old = m_ref[idx]
    m_new = jnp.maximum(m_old, jnp.max(s, axis=0, keepdims=True))
    m_ref[idx] = m_new
    return jnp.exp2(s - m_new).astype(BF16), jnp.exp2(m_old - m_new)


def _causal_mask_t(n):
    key = lax.broadcasted_iota(jnp.int32, (n, n), 0)
    query = lax.broadcasted_iota(jnp.int32, (n, n), 1)
    return key <= query


def _init_softmax_state(m_ref, acc_ref):
    m_ref[...] = jnp.full(m_ref.shape, NEG_INF, F32)
    acc_ref[...] = jnp.zeros(acc_ref.shape, F32)


def _with_ones(vt):
    return jnp.concatenate([vt, jnp.ones((ONES_ROWS, vt.shape[1]), BF16)], axis=0)


def _causal_sweep(i, tq, scores_of, consume):
    def keys(j):
        return pl.ds(pl.multiple_of(j * tq, tq), tq)

    def step(blocks):
        scores = [scores_of(k) for k, _ in blocks]
        for (k, masked), s in zip(blocks, scores):
            consume(s, k, masked)

    def pair(j, carry):
        step([(keys(2 * j), False), (keys(2 * j + 1), False)])
        return carry

    lax.fori_loop(0, i >> 1, pair, 0)

    @pl.when((i & 1) == 1)
    def _():
        step([(keys(i - 1), False), (keys(i), True)])

    @pl.when((i & 1) == 0)
    def _():
        step([(keys(i), True)])


BIAS_LANES = 8


def _split3(c):
    hi = c.astype(BF16)
    rest = c - hi.astype(F32)
    mid = rest.astype(BF16)
    lo = (rest - mid.astype(F32)).astype(BF16)
    return jnp.concatenate([hi, mid, lo], axis=1)


def _bias_select(pair, key_side):
    r = lax.broadcasted_iota(jnp.int32, (3 * LANES, 1), 0)
    d = lax.broadcasted_iota(jnp.int32, (1, LANES), 1)
    piece, src = r >> 7, r & (LANES - 1)
    shift = 3 if key_side else 0
    first = jnp.logical_and(src == 2 * pair, d == piece + shift)
    second = jnp.logical_and(src == 2 * pair + 1, d == piece + shift + BIAS_LANES)
    value = -1.0 if key_side else 1.0
    return jnp.where(jnp.logical_or(first, second), value, 0.0).astype(BF16)


def _bias_ones(key_side):
    d = lax.broadcasted_iota(jnp.int32, (1, LANES), 1)
    within = d & (BIAS_LANES - 1)
    hit = (within < 3) if key_side else jnp.logical_and(within >= 3, within < 6)
    return jnp.where(jnp.logical_and(hit, d < 2 * BIAS_LANES), 1.0, 0.0)


def _fox_kernel(q_ref, k_ref, vt_ref, c_ref, o_ref, m_ref, acc_ref, kb_ref):
    i = pl.program_id(1)
    tq = q_ref.shape[0]
    hd = FOX_HEAD_DIM
    npair = FOX_HEADS // 2
    lane = lax.broadcasted_iota(jnp.int32, (tq, LANES), 1)
    lo = lane < hd
    lo_bias = lane < BIAS_LANES
    _init_softmax_state(m_ref, acc_ref)

    @pl.when(i == 0)
    def _():
        pieces = _split3(c_ref[...])
        for hp in range(npair):
            kb_ref[hp] = (_dot(pieces, _bias_select(hp, True)) + _bias_ones(True)).astype(BF16)

    q_pieces = _split3(c_ref[pl.ds(pl.multiple_of(i * tq, tq), tq), :])
    q_ops = []
    for hp in range(npair):
        q = q_ref[:, hp * LANES:(hp + 1) * LANES]
        zero = jnp.zeros_like(q)
        qb = (_dot(q_pieces, _bias_select(hp, False)) + _bias_ones(False)).astype(BF16)
        q_ops.append(jnp.concatenate([jnp.where(lo, q, zero), jnp.where(lo_bias, qb, zero)], axis=1))
        q_ops.append(jnp.concatenate([jnp.where(lo, zero, q), jnp.where(lo_bias, zero, qb)], axis=1))

    def scores_of(keys):
        out = []
        for hp in range(npair):
            k_op = jnp.concatenate([k_ref[keys, hp * LANES:(hp + 1) * LANES], kb_ref[hp, keys, :]], axis=1)
            out.append(_dot_nt(k_op, q_ops[2 * hp]))
            out.append(_dot_nt(k_op, q_ops[2 * hp + 1]))
        return out

    def consume(scores, keys, masked):
        probs = []
        for h in range(FOX_HEADS):
            s = scores[h]
            if masked:
                s = jnp.where(_causal_mask_t(tq), s, NEG_INF)
            probs.append(_online_softmax(s, m_ref, h))
        for h in range(FOX_HEADS):
            p, alpha = probs[h]
            pv = _dot(_with_ones(vt_ref[h * hd:(h + 1) * hd, keys]), p)
            acc_ref[h] = acc_ref[h] * alpha + pv

    _causal_sweep(i, tq, scores_of, consume)
    for hp in range(npair):
        heads = [acc_ref[h, :hd, :] / acc_ref[h, hd:hd + 1, :] for h in (2 * hp, 2 * hp + 1)]
        o_ref[:, hp * LANES:(hp + 1) * LANES] = jnp.concatenate(heads, axis=0).T.astype(BF16)


def _fox(proj, vt, c):
    nq = SEQ // TQ
    return pl.pallas_call(
        _fox_kernel,
        grid=(BATCH, nq),
        in_specs=[
            pl.BlockSpec((None, TQ, 512), lambda b, i: (SEG_FQ, b * nq + i, 0)),
            pl.BlockSpec((None, SEQ, 512), lambda b, i: (SEG_FK, b, 0)),
            pl.BlockSpec((None, 512, SEQ), lambda b, i: (VT_FOX, 0, b)),
            pl.BlockSpec((SEQ, LANES), lambda b, i: (b, 0)),
        ],
        out_specs=pl.BlockSpec((TQ, 512), lambda b, i: (b * nq + i, 0)),
        out_shape=jax.ShapeDtypeStruct((TOKENS, 512), BF16),
        scratch_shapes=[pltpu.VMEM((FOX_HEADS, 1, TQ), F32),
                        pltpu.VMEM((FOX_HEADS, FOX_HEAD_DIM + ONES_ROWS, TQ), F32),
                        pltpu.VMEM((FOX_HEADS // 2, SEQ, LANES), BF16)],
        compiler_params=_params("parallel", "arbitrary"),
        name="fox_attn",
    )(proj, proj, vt, c)


def _diff_kernel(lam_init, q_ref, k_ref, vt_ref, lam_ref, g_ref, o_ref, m_ref, acc_ref):
    i = pl.program_id(1)
    tq = q_ref.shape[0]
    dv = LANES
    lane = lax.broadcasted_iota(jnp.int32, (tq, LANES), 1)
    lo = lane < DIFF_HEAD_DIM
    _init_softmax_state(m_ref, acc_ref)

    def scores_of(keys):
        out = []
        for h in range(DIFF_HEADS):
            cols = slice(h * LANES, (h + 1) * LANES)
            q = q_ref[:, cols]
            zero = jnp.zeros_like(q)
            kb = k_ref[keys, cols]
            out.append(_dot_nt(kb, jnp.where(lo, q, zero)))
            out.append(_dot_nt(kb, jnp.where(lo, zero, q)))
        return out

    def consume(scores, keys, masked):
        probs = []
        for idx in range(2 * DIFF_HEADS):
            s = scores[idx]
            if masked:
                s = jnp.where(_causal_mask_t(tq), s, NEG_INF)
            probs.append(_online_softmax(s, m_ref, idx))
        for h in range(DIFF_HEADS):
            vtb = _with_ones(vt_ref[h * dv:(h + 1) * dv, keys])
            for idx in (2 * h, 2 * h + 1):
                p, alpha = probs[idx]
                acc_ref[idx] = acc_ref[idx] * alpha + _dot(vtb, p)

    _causal_sweep(i, tq, scores_of, consume)
    lam_rows = lam_ref[...]
    lam = (jnp.exp(jnp.sum(lam_rows[0:1] * lam_rows[1:2], axis=1, keepdims=True))
           - jnp.exp(jnp.sum(lam_rows[2:3] * lam_rows[3:4], axis=1, keepdims=True)) + lam_init)
    for h in range(DIFF_HEADS):
        cols = slice(h * LANES, (h + 1) * LANES)
        o1 = acc_ref[2 * h, :dv, :] / acc_ref[2 * h, dv:dv + 1, :]
        o2 = acc_ref[2 * h + 1, :dv, :] / acc_ref[2 * h + 1, dv:dv + 1, :]
        o = o1 - lam * o2
        o = o * lax.rsqrt(jnp.mean(o * o, axis=0, keepdims=True) + RMS_EPS)
        o_ref[:, cols] = (o.T * g_ref[:, cols] * (1.0 - lam_init)).astype(BF16)


def _diff(proj, vt, lam, g, lam_init):
    nq = SEQ // TQ
    return pl.pallas_call(
        functools.partial(_diff_kernel, lam_init),
        grid=(BATCH, nq),
        in_specs=[
            pl.BlockSpec((None, TQ, 512), lambda b, i: (SEG_DQ, b * nq + i, 0)),
            pl.BlockSpec((None, SEQ, 512), lambda b, i: (SEG_DK, b, 0)),
            pl.BlockSpec((None, 512, SEQ), lambda b, i: (VT_DIFF, 0, b)),
            pl.BlockSpec((4, DIFF_HEAD_DIM), lambda b, i: (0, 0)),
            pl.BlockSpec((1, 512), lambda b, i: (0, 0)),
        ],
        out_specs=pl.BlockSpec((TQ, 512), lambda b, i: (b * nq + i, 0)),
        out_shape=jax.ShapeDtypeStruct((TOKENS, 512), BF16),
        scratch_shapes=[pltpu.VMEM((2 * DIFF_HEADS, 1, TQ), F32),
                        pltpu.VMEM((2 * DIFF_HEADS, LANES + ONES_ROWS, TQ), F32)],
        compiler_params=_params("parallel", "arbitrary"),
        name="diff_attn",
    )(proj, proj, vt, lam, g)


def _spread(compact, select, row_group, col_group):
    full = _dot(compact, select)
    return jnp.where(row_group == col_group, full, 0.0).astype(BF16)


def _s5_kernel(u_ref, am_ref, abre_ref, abim_ref, acre_ref, acim_ref, are_ref, aim_ref, y_ref, *state):
    nslab = SSM_SCOLS // LANES
    sre, sim = state[:nslab], state[nslab:]

    def iota(shape, dim):
        return lax.broadcasted_iota(jnp.int32, shape, dim)

    i_tn, j_cols = iota((SSM_TN, 1), 0), iota((1, SSM_COLS), 1)
    sel_tn = jnp.where(jnp.logical_and((i_tn >> 4) == (j_cols >> 7), (i_tn & 15) == (j_cols & 15)),
                       1.0, 0.0).astype(BF16)
    i_p, j_sc = iota((SSM_STATE, 1), 0), iota((1, SSM_SCOLS), 1)
    sel_p = jnp.where(i_p == (j_sc & (SSM_STATE - 1)), 1.0, 0.0).astype(BF16)
    grp_rows_cols = (iota((SSM_COLS, 1), 0) >> 4) & (SSM_GBLK - 1)
    grp_cols_cols = (j_cols >> 4) & (SSM_GBLK - 1)
    grp_rows_state = iota((SSM_SCOLS, 1), 0) >> 6
    grp_cols_state = j_sc >> 6
    m = _spread(am_ref[...], sel_tn, grp_rows_cols, grp_cols_cols)
    bre = _spread(abre_ref[...], sel_p, grp_rows_cols, grp_cols_state)
    bim = _spread(abim_ref[...], sel_p, grp_rows_cols, grp_cols_state)
    cre = _spread(acre_ref[...], sel_tn, grp_rows_state, grp_cols_cols)
    cim = _spread(acim_ref[...], sel_tn, grp_rows_state, grp_cols_cols)

    u = jnp.concatenate([u_ref[s] for s in range(SSM_CHUNK)], axis=1)
    for k in range(0, nslab, 2):
        cols = slice(k * LANES, (k + 2) * LANES)
        for dst, b in ((sre, bre), (sim, bim)):
            part = _dot(u, b[:, cols])
            dst[k][...] = part[:, :LANES]
            dst[k + 1][...] = part[:, LANES:]
    ar = [are_ref[:, k * LANES:(k + 1) * LANES] for k in range(nslab)]
    ai = [aim_ref[:, k * LANES:(k + 1) * LANES] for k in range(nslab)]

    def body(c, carry):
        rows = pl.ds(c, BATCH, stride=SSM_NCHUNK)
        nxt = []
        for k in range(nslab):
            xr, xi = carry[k]
            sr = sre[k][rows, :]
            si = sim[k][rows, :]
            sre[k][rows, :] = xr
            sim[k][rows, :] = xi
            nxt.append((ar[k] * xr - ai[k] * xi + sr, ar[k] * xi + ai[k] * xr + si))
        return tuple(nxt)

    zero = jnp.zeros((BATCH, LANES), F32)
    lax.fori_loop(0, SSM_NCHUNK, body, tuple((zero, zero) for _ in range(nslab)))
    xr = jnp.concatenate([r[...] for r in sre], axis=1).astype(BF16)
    xi = jnp.concatenate([r[...] for r in sim], axis=1).astype(BF16)
    for c in range(SSM_COLS // (2 * LANES)):
        cols = slice(c * 2 * LANES, (c + 1) * 2 * LANES)
        y = _dot(u, m[:, cols]) + _dot(xr, cre[:, cols]) + _dot(xi, cim[:, cols])
        y_ref[2 * c] = y[:, :LANES].astype(BF16)
        y_ref[2 * c + 1] = y[:, LANES:].astype(BF16)


def _s5_scan(u_stack, mats):
    am, abre, abim, acre, acim, are, aim = mats

    def spec(*shape):
        return pl.BlockSpec((None,) + shape, lambda g: (g,) + (0,) * len(shape))

    io_spec = pl.BlockSpec((SSM_CHUNK, SSM_ROWS, LANES), lambda g: (0, 0, g))
    return pl.pallas_call(
        _s5_kernel,
        grid=(SSM_NGBLK,),
        in_specs=[io_spec, spec(SSM_COLS, SSM_TN),
                  spec(SSM_COLS, SSM_STATE), spec(SSM_COLS, SSM_STATE),
                  spec(SSM_SCOLS, SSM_TN), spec(SSM_SCOLS, SSM_TN),
                  spec(1, SSM_SCOLS), spec(1, SSM_SCOLS)],
        out_specs=io_spec,
        out_shape=jax.ShapeDtypeStruct((SSM_CHUNK, SSM_ROWS, SSM_WIDTH), BF16),
        scratch_shapes=[pltpu.VMEM((SSM_ROWS, LANES), F32) for _ in range(2 * SSM_SCOLS // LANES)],
        compiler_params=_params("parallel"),
        name="s5_scan",
    )(u_stack, am, abre, abim, acre, acim, are, aim)


def _s5_matrices(lam_re, lam_im, log_dt, b_re, b_im, c_re, c_im):
    hi = lax.Precision.HIGHEST
    L, NB, GB = SSM_CHUNK, SSM_NGBLK, SSM_GBLK
    lr = lam_re.astype(F32)
    li = lam_im.astype(F32)
    dt = jnp.exp(log_dt.astype(F32))[:, None]
    mag = jnp.exp(lr * dt)
    ang = li * dt
    ar = mag * jnp.cos(ang)
    ai = mag * jnp.sin(ang)
    er = ar - 1.0
    ei = ai
    den = lr * lr + li * li
    qr = (er * lr + ei * li) / den
    qi = (ei * lr - er * li) / den
    br = b_re.astype(F32)
    bi = b_im.astype(F32)
    bbr = qr[:, :, None] * br - qi[:, :, None] * bi
    bbi = qr[:, :, None] * bi + qi[:, :, None] * br
    k = jnp.arange(L + 1, dtype=F32)[:, None, None]
    pmag = jnp.exp(lr[None] * dt[None] * k)
    pr = pmag * jnp.cos(ang[None] * k)
    pi = pmag * jnp.sin(ang[None] * k)
    cr = c_re.astype(F32)
    ci = c_im.astype(F32)
    e_r = cr[None] * pr[:, :, None, :] - ci[None] * pi[:, :, None, :]
    e_i = cr[None] * pi[:, :, None, :] + ci[None] * pr[:, :, None, :]
    kern = (jnp.einsum('kgnp,gpm->kgnm', e_r[:L], bbr, precision=hi)
            - jnp.einsum('kgnp,gpm->kgnm', e_i[:L], bbi, precision=hi))
    lag = jnp.arange(L)[None, :] - jnp.arange(L)[:, None]
    toep = jnp.where((lag >= 0)[:, :, None, None, None], kern[jnp.clip(lag, 0, L - 1)], 0.0)
    toep = toep.reshape(L, L, NB, GB, SSM_GROUP, SSM_GROUP)
    am = jnp.transpose(toep, (2, 0, 3, 5, 1, 4)).reshape(NB, SSM_COLS, SSM_TN)
    krev = (L - 1) - jnp.arange(L, dtype=F32)[:, None, None]
    rmag = jnp.exp(lr[None] * dt[None] * krev)
    rev_r = rmag * jnp.cos(ang[None] * krev)
    rev_i = rmag * jnp.sin(ang[None] * krev)
    sb_r = rev_r[:, :, :, None] * bbr[None] - rev_i[:, :, :, None] * bbi[None]
    sb_i = rev_r[:, :, :, None] * bbi[None] + rev_i[:, :, :, None] * bbr[None]

    def state_in(t):
        t = t.reshape(L, NB, GB, SSM_STATE, SSM_GROUP)
        return jnp.transpose(t, (1, 0, 2, 4, 3)).reshape(NB, SSM_COLS, SSM_STATE)

    def state_out(t):
        t = t.reshape(L, NB, GB, SSM_GROUP, SSM_STATE)
        return jnp.transpose(t, (1, 2, 4, 0, 3)).reshape(NB, SSM_SCOLS, SSM_TN)

    are = pr[L].reshape(NB, 1, SSM_SCOLS)
    aim = pi[L].reshape(NB, 1, SSM_SCOLS)
    return (am.astype(BF16), state_in(sb_r).astype(BF16), state_in(sb_i).astype(BF16),
            state_out(e_r[1:]).astype(BF16), (-state_out(e_i[1:])).astype(BF16), are, aim)


def _s5_post_kernel(y_ref, u_ref, d_ref, w_ref, o_ref, tmp_ref):
    nrow = y_ref.shape[1]
    ycat = jnp.concatenate([y_ref[s] for s in range(SSM_CHUNK)], axis=0).astype(F32)
    ucat = jnp.concatenate([u_ref[s] for s in range(SSM_CHUNK)], axis=0).astype(F32)
    y = jax.nn.gelu(ycat + d_ref[...] * ucat, approximate=True)
    z = _dot(y.astype(BF16), w_ref[...])
    out = z[:, :SSM_WIDTH] * jax.nn.sigmoid(z[:, SSM_WIDTH:])
    for s in range(SSM_CHUNK):
        for c in range(SSM_WIDTH // LANES):
            tmp_ref[c, pl.ds(s, nrow, stride=SSM_CHUNK), :] = (
                out[s * nrow:(s + 1) * nrow, c * LANES:(c + 1) * LANES])
    for c in range(SSM_WIDTH // LANES):
        o_ref[:, c * LANES:(c + 1) * LANES] = tmp_ref[c].astype(BF16)


def _s5_post(y_stack, u_stack, d, w_glu):
    tm = TM_POST
    stack_spec = pl.BlockSpec((SSM_CHUNK, tm // SSM_CHUNK, SSM_WIDTH), lambda i: (0, i, 0))
    return pl.pallas_call(
        _s5_post_kernel,
        grid=(TOKENS // tm,),
        in_specs=[
            stack_spec,
            stack_spec,
            pl.BlockSpec((1, SSM_WIDTH), lambda i: (0, 0)),
            pl.BlockSpec((SSM_WIDTH, 2 * SSM_WIDTH), lambda i: (0, 0)),
        ],
        out_specs=pl.BlockSpec((tm, SSM_WIDTH), lambda i: (i, 0)),
        out_shape=jax.ShapeDtypeStruct((TOKENS, SSM_WIDTH), BF16),
        scratch_shapes=[pltpu.VMEM((SSM_WIDTH // LANES, tm, LANES), F32)],
        compiler_params=_params("parallel"),
        name="s5_post",
    )(y_stack, u_stack, d, w_glu)


def _merge_kernel(x_ref, ys_ref, yd_ref, yf_ref, wgl_ref, wbr_ref, wout_ref, g_ref, b_ref, o_ref):
    x = x_ref[...]
    xb = x.astype(BF16)
    merged = None
    for n, y_ref in enumerate((ys_ref, yd_ref, yf_ref)):
        gate = jax.nn.sigmoid(_dot(xb, wgl_ref[:, n * D_MODEL:(n + 1) * D_MODEL]))
        term = gate * _dot(y_ref[...], wbr_ref[n])
        merged = term if merged is None else merged + term
    mix = _dot(merged.astype(BF16), wout_ref[...])
    o_ref[...] = _layer_norm(ALPHA * x + mix, g_ref[...], b_ref[...])


def _merge(x, ys, yd, yf, wgl, wbr, wout, g, b):
    tm = TM_MERGE
    row = lambda i: (i, 0)
    const2 = lambda i: (0, 0)
    return pl.pallas_call(
        _merge_kernel,
        grid=(TOKENS // tm,),
        in_specs=[
            pl.BlockSpec((tm, D_MODEL), row),
            pl.BlockSpec((tm, 512), row),
            pl.BlockSpec((tm, 512), row),
            pl.BlockSpec((tm, 512), row),
            pl.BlockSpec((D_MODEL, N_BRANCH * D_MODEL), const2),
            pl.BlockSpec((N_BRANCH, BRANCH_WIDTH, D_MODEL), lambda i: (0, 0, 0)),
            pl.BlockSpec((D_MODEL, D_MODEL), const2),
            pl.BlockSpec((1, D_MODEL), const2),
            pl.BlockSpec((1, D_MODEL), const2),
        ],
        out_specs=pl.BlockSpec((tm, D_MODEL), row),
        out_shape=jax.ShapeDtypeStruct((TOKENS, D_MODEL), F32),
        compiler_params=_params("parallel"),
        name="merge_ln",
    )(x, ys, yd, yf, wgl, wbr, wout, g, b)


def _router_kernel(x_ref, w_ref, b_ref, o_ref):
    x = x_ref[...]
    xh = x.astype(BF16)
    xl = (x - xh.astype(F32)).astype(BF16)
    hw = _dot(xh, w_ref[...])
    logits = hw[:, :LANES] + (hw[:, LANES:] + _dot(xl, w_ref[:, :LANES])) + b_ref[...]
    lane = lax.broadcasted_iota(jnp.int32, logits.shape, 1).astype(F32)
    big = jnp.float32(LANES)
    neg = jnp.float32(-jnp.inf)

    def first_argmax(v, vmax):
        return jnp.min(jnp.where(v == vmax, lane, big), axis=1, keepdims=True)

    gl = jnp.where(lane < N_EXPERT_GROUPS, logits, neg)
    ge = jnp.exp(gl - jnp.max(gl, axis=1, keepdims=True))
    pg = ge / jnp.sum(ge, axis=1, keepdims=True)
    gp = jnp.max(pg, axis=1, keepdims=True)
    gi = first_argmax(pg, gp)
    base = N_EXPERT_GROUPS + EXPERTS_PER_GROUP * gi
    sel = jnp.logical_and(lane >= base, lane < base + EXPERTS_PER_GROUP)
    el = jnp.where(sel, logits, neg)
    ev1 = jnp.max(el, axis=1, keepdims=True)
    i1 = first_argmax(el, ev1)
    el2 = jnp.where(lane == i1, neg, el)
    ev2 = jnp.max(el2, axis=1, keepdims=True)
    i2 = first_argmax(el2, ev2)
    e2 = jnp.exp(ev2 - ev1)
    den = 1.0 + e2
    w1 = gp * (1.0 / den)
    w2 = gp * (e2 / den)
    o_ref[...] = (jnp.where(lane == i1, w1, 0.0) + jnp.where(lane == i2, w2, 0.0)
                  + jnp.where(lane == 0.0, gi, 0.0))


def _router(x, w, b):
    tm = TM_ROUTE
    return pl.pallas_call(
        _router_kernel,
        grid=(TOKENS // tm,),
        in_specs=[
            pl.BlockSpec((tm, D_MODEL), lambda i: (i, 0)),
            pl.BlockSpec((D_MODEL, 2 * LANES), lambda i: (0, 0)),
            pl.BlockSpec((1, LANES), lambda i: (0, 0)),
        ],
        out_specs=pl.BlockSpec((tm, LANES), lambda i: (i, 0)),
        out_shape=jax.ShapeDtypeStruct((TOKENS, LANES), F32),
        compiler_params=_params("parallel"),
        name="router",
    )(x, w, b)


MOE_SUB = 256


def _one_hot(cond):
    return jnp.where(cond, 1.0, 0.0).astype(BF16)


def _moe_sort(x_ref, gate_ref, xs_ref, gs_ref, ys_ref, pos_ref, off_ref):
    tm = x_ref.shape[0]
    gates = gate_ref[...]
    lane = lax.broadcasted_iota(jnp.int32, (1, LANES), 1)
    row8 = lax.broadcasted_iota(jnp.int32, (8, 1), 0)
    t_col = lax.broadcasted_iota(jnp.int32, (tm, 1), 0)
    t_row = lax.broadcasted_iota(jnp.int32, (1, tm), 1)
    member = jnp.where(gates[:, 0:1] == lane.astype(F32), 1.0, 0.0)
    member_b = member.astype(BF16)
    counts = jnp.sum(member, axis=0, keepdims=True)
    cnt = [jnp.sum(jnp.where(lane == k, counts, 0.0), axis=1, keepdims=True)
           for k in range(N_EXPERT_GROUPS - 1)]
    start = [cnt[0], cnt[0] + cnt[1], cnt[0] + cnt[1] + cnt[2]]
    off_ref[0] = 0
    for k in range(N_EXPERT_GROUPS - 1):
        off_ref[k + 1] = start[k][0, 0].astype(jnp.int32)
    off_ref[N_EXPERT_GROUPS] = tm

    def offsets(index):
        return sum(jnp.where(index > k, cnt[k], 0.0) for k in range(N_EXPERT_GROUPS - 1))

    earlier = _one_hot(t_row < t_col)
    rank_col = _dot(earlier, member_b)
    pos_col = jnp.sum(member * (offsets(lane) + rank_col), axis=1, keepdims=True)
    pos_ref[...] = jnp.broadcast_to(pos_col, (tm, LANES))
    pick = _one_hot(lax.broadcasted_iota(jnp.int32, (8, LANES), 0)
                    == lax.broadcasted_iota(jnp.int32, (8, LANES), 1))
    member_t = _dot_nt(pick, member_b)
    rank_row = _dot(member_t.astype(BF16), _one_hot(t_col < t_row))
    pos_row = jnp.sum(member_t * (offsets(row8) + rank_row), axis=0, keepdims=True)
    perm = _one_hot(t_col.astype(F32) == pos_row)
    xs_ref[...] = _dot(perm, x_ref[...].astype(BF16)).astype(BF16)
    g_hi = gates.astype(BF16)
    rest = gates - g_hi.astype(F32)
    g_mid = rest.astype(BF16)
    g_lo = (rest - g_mid.astype(F32)).astype(BF16)
    moved = _dot(perm, jnp.concatenate([g_hi, g_mid, g_lo], axis=1))
    gs_ref[...] = moved[:, :LANES] + moved[:, LANES:2 * LANES] + moved[:, 2 * LANES:]
    ys_ref[...] = jnp.zeros(ys_ref.shape, F32)


def _moe_kernel(x_ref, gate_ref, wg_ref, wu_ref, wd_ref, g_ref, b_ref, o_ref,
                xs_ref, gs_ref, ys_ref, pos_ref, off_ref):
    grp = pl.program_id(1)
    tm = x_ref.shape[0]

    @pl.when(grp == 0)
    def _():
        _moe_sort(x_ref, gate_ref, xs_ref, gs_ref, ys_ref, pos_ref, off_ref)

    first = N_EXPERT_GROUPS + EXPERTS_PER_GROUP * grp
    shift = MOE_SUB.bit_length() - 1

    def sub_tile(k, carry):
        rows = pl.ds(pl.multiple_of(k * MOE_SUB, MOE_SUB), MOE_SUB)
        xb = xs_ref[rows, :]
        gates = gs_ref[rows, :]
        lane = lax.broadcasted_iota(jnp.int32, gates.shape, 1)
        hidden = []
        for e in range(EXPERTS_PER_GROUP):
            gate = jnp.sum(jnp.where(lane == first + e, gates, 0.0), axis=1, keepdims=True)
            he = jax.nn.silu(_dot(xb, wg_ref[e])) * _dot(xb, wu_ref[e]) * gate
            hidden.append(he.astype(BF16))
        ys_ref[rows, :] += _dot(jnp.concatenate(hidden, axis=1), wd_ref[...])
        return carry

    lo = off_ref[grp] >> shift
    hi = (off_ref[grp + 1] + (MOE_SUB - 1)) >> shift
    lax.fori_loop(lo, hi, sub_tile, 0)

    @pl.when(grp == N_EXPERT_GROUPS - 1)
    def _():
        slot = lax.broadcasted_iota(jnp.int32, (1, tm), 1).astype(F32)
        unsort = _one_hot(pos_ref[:, 0:1] == slot)
        y = _dot(unsort, ys_ref[...].astype(BF16))
        o_ref[...] = _layer_norm(ALPHA * x_ref[...] + y, g_ref[...], b_ref[...])


def _moe(x, gates, wg, wu, wd, g, b):
    tm = TM_MOE
    wd_grouped = wd.reshape(N_EXPERT_GROUPS, EXPERTS_PER_GROUP * D_EXPERT, D_MODEL)
    return pl.pallas_call(
        _moe_kernel,
        grid=(TOKENS // tm, N_EXPERT_GROUPS),
        in_specs=[
            pl.BlockSpec((tm, D_MODEL), lambda i, e: (i, 0)),
            pl.BlockSpec((tm, LANES), lambda i, e: (i, 0)),
            pl.BlockSpec((EXPERTS_PER_GROUP, D_MODEL, D_EXPERT), lambda i, e: (e, 0, 0)),
            pl.BlockSpec((EXPERTS_PER_GROUP, D_MODEL, D_EXPERT), lambda i, e: (e, 0, 0)),
            pl.BlockSpec((None, EXPERTS_PER_GROUP * D_EXPERT, D_MODEL), lambda i, e: (e, 0, 0)),
            pl.BlockSpec((1, D_MODEL), lambda i, e: (0, 0)),
            pl.BlockSpec((1, D_MODEL), lambda i, e: (0, 0)),
        ],
        out_specs=pl.BlockSpec((tm, D_MODEL), lambda i, e: (i, 0)),
        out_shape=jax.ShapeDtypeStruct((TOKENS, D_MODEL), F32),
        scratch_shapes=[pltpu.VMEM((tm, D_MODEL), BF16),
                        pltpu.VMEM((tm, LANES), F32),
                        pltpu.VMEM((tm, D_MODEL), F32),
                        pltpu.VMEM((tm, LANES), F32),
                        pltpu.SMEM((8,), jnp.int32)],
        compiler_params=_params("parallel", "arbitrary"),
        name="moe_ln",
    )(x, gates, wg, wu, wd_grouped, g, b)


def _rope_tables():
    pos = jnp.arange(SEQ, dtype=F32)
    inv_freq = ROPE_THETA ** (-jnp.arange(0, DIFF_HEAD_DIM, 2, dtype=F32) / DIFF_HEAD_DIM)
    ang = pos[:, None] * inv_freq[None, :]
    emb = jnp.concatenate([ang, ang], axis=-1)
    cos = jnp.cos(emb)
    sin = jnp.sin(emb)
    sign = jnp.where(jnp.arange(DIFF_HEAD_DIM) < DIFF_HEAD_DIM // 2, -1.0, 1.0).astype(F32)
    return jnp.tile(cos, (1, 2)), jnp.tile(sin * sign, (1, 2))


def _split_w_in(w_in):
    qscale = DIFF_HEAD_DIM ** -0.5 * LOG2E

    def seg(k):
        return w_in[:, :, k * 512:(k + 1) * 512]

    w_tok = jnp.stack([seg(0), seg(1) * qscale, seg(2), seg(4) * qscale, seg(5)], axis=1).astype(BF16)
    w_vt = jnp.stack([jnp.swapaxes(seg(3), 1, 2), jnp.swapaxes(seg(6), 1, 2)], axis=1).astype(BF16)
    off = 7 * 512
    wff_t = jnp.swapaxes(w_in[:, :, off:off + FOX_HEADS], 1, 2).astype(BF16)
    wgl = w_in[:, :, off + FOX_HEADS:].astype(BF16)
    return w_tok, w_vt, wff_t, wgl


def kernel(x, w_in, w_branch, w_out, ssm_lambda_re, ssm_lambda_im, ssm_log_dt, ssm_b_re, ssm_b_im,
           ssm_c_re, ssm_c_im, ssm_d, ssm_w_glu, diff_lambda, diff_norm_g, fox_f_bias, ln1_g, ln1_b,
           moe_w_group, moe_b_group, moe_w_expert, moe_b_expert, moe_w_gate, moe_w_up, moe_w_down,
           ln2_g, ln2_b):
    cos, sin = _rope_tables()
    w_tok, w_vt, wff_t, wgl = _split_w_in(w_in)
    s5_mats = jax.vmap(_s5_matrices)(ssm_lambda_re, ssm_lambda_im, ssm_log_dt, ssm_b_re, ssm_b_im,
                                     ssm_c_re, ssm_c_im)
    w_glu = ssm_w_glu.astype(BF16)
    w_br = w_branch.astype(BF16)
    w_o = w_out.astype(BF16)
    w_gate, w_up, w_down = moe_w_gate.astype(BF16), moe_w_up.astype(BF16), moe_w_down.astype(BF16)
    w_r = jnp.concatenate([moe_w_group, moe_w_expert], axis=2).astype(F32)
    w_r = jnp.pad(w_r, ((0, 0), (0, 0), (0, LANES - w_r.shape[2])))
    w_r_hi = w_r.astype(BF16)
    w_r = jnp.concatenate([w_r_hi, (w_r - w_r_hi.astype(F32)).astype(BF16)], axis=2)
    b_r = jnp.concatenate([moe_b_group, moe_b_expert], axis=1).astype(F32)
    b_r = jnp.pad(b_r, ((0, 0), (0, LANES - b_r.shape[1]))).reshape(DEPTH, 1, LANES)

    h = x.reshape(TOKENS, D_MODEL)
    for l in range(DEPTH):
        lam_init = 0.8 - 0.6 * math.exp(-0.3 * l)
        proj, u_stack, vt, fft = _inproj(h, w_tok[l], w_vt[l], wff_t[l], cos, sin)

        y_stack = _s5_scan(u_stack, tuple(t[l] for t in s5_mats))
        y_ssm = _s5_post(y_stack, u_stack, ssm_d[l].reshape(1, SSM_WIDTH).astype(F32), w_glu[l])

        y_diff = _diff(proj, vt, diff_lambda[l].astype(F32),
                       diff_norm_g[l].reshape(1, 512).astype(F32), lam_init)
        c = _fgate(fft, fox_f_bias[l].reshape(FOX_HEADS, 1).astype(F32))
        y_fox = _fox(proj, vt, c)

        h = _merge(h, y_ssm, y_diff, y_fox, wgl[l], w_br[l], w_o[l],
                   ln1_g[l].reshape(1, D_MODEL), ln1_b[l].reshape(1, D_MODEL))

        gates = _router(h, w_r[l], b_r[l])
        h = _moe(h, gates, w_gate[l], w_up[l], w_down[l],
                 ln2_g[l].reshape(1, D_MODEL), ln2_b[l].reshape(1, D_MODEL))
    return h.reshape(BATCH, SEQ, D_MODEL)
```

```python
import functools
import math

import jax
import jax.numpy as jnp
from jax import lax
from jax.experimental import pallas as pl
from jax.experimental.pallas import tpu as pltpu

F32 = jnp.float32
BF16 = jnp.bfloat16

D_MODEL = 1024
BATCH = 8
SEQ = 2048
DEPTH = 4
TOKENS = BATCH * SEQ

SSM_WIDTH = 512
SSM_GROUP = 16
SSM_GROUPS = 32
SSM_STATE = 64
SSM_CHUNK = 8
SSM_NCHUNK = SEQ // SSM_CHUNK
SSM_ROWS = BATCH * SSM_NCHUNK
SSM_GBLK = 8
SSM_NGBLK = SSM_GROUPS // SSM_GBLK
SSM_COLS = SSM_CHUNK * 128
SSM_SCOLS = SSM_GBLK * SSM_STATE
SSM_TN = SSM_CHUNK * SSM_GROUP

DIFF_HEADS = 4
DIFF_HEAD_DIM = 64
FOX_HEADS = 8
FOX_HEAD_DIM = 64
BRANCH_WIDTH = 512
N_BRANCH = 3
ROPE_THETA = 10000.0

N_EXPERT_GROUPS = 4
EXPERTS_PER_GROUP = 4
N_EXPERTS = 16
D_EXPERT = 256

ALPHA = (2 * DEPTH) ** 0.25
LN_EPS = 1e-5
RMS_EPS = 1e-6
NEG_INF = -1e30
LOG2E = math.log2(math.e)

LANES = 128
N_SEG = 4
SEG_DQ, SEG_DK, SEG_FQ, SEG_FK = range(N_SEG)
VT_DIFF, VT_FOX = 0, 1
VMEM_LIMIT = 56 * 1024 * 1024

TM_PROJ = 512
TQ = 256
TM_POST = 512
TM_MERGE = 256
TM_ROUTE = 512
TM_MOE = 1024


def _params(*sem):
    return pltpu.CompilerParams(dimension_semantics=sem, vmem_limit_bytes=VMEM_LIMIT)


def _dot(a, b):
    return jnp.dot(a, b, preferred_element_type=F32)


def _dot_nt(a, b):
    return lax.dot_general(a, b, (((1,), (1,)), ((), ())), preferred_element_type=F32)


def _layer_norm(y, g, b):
    mu = jnp.mean(y, axis=-1, keepdims=True)
    d = y - mu
    var = jnp.mean(d * d, axis=-1, keepdims=True)
    return d * lax.rsqrt(var + LN_EPS) * g + b


def _inproj_kernel(x_ref, w_ref, wvt_ref, wff_ref, cos_ref, sin_ref, o_ref, u_ref, vt_ref, fft_ref,
                   tmp_ref):
    tm = x_ref.shape[0]
    xb = x_ref[...].astype(BF16)
    cos = cos_ref[...]
    sin = sin_ref[...]
    lane = lax.broadcasted_iota(jnp.int32, cos.shape, 1)
    first_half = (lane & 32) == 0
    for seg in range(N_SEG):
        acc = _dot(xb, w_ref[seg + 1])
        if seg in (SEG_DQ, SEG_DK):
            for c in range(512 // LANES):
                t = acc[:, c * LANES:(c + 1) * LANES]
                rot = jnp.where(first_half, pltpu.roll(t, LANES - 32, 1), pltpu.roll(t, 32, 1))
                o_ref[seg, :, c * LANES:(c + 1) * LANES] = (t * cos + rot * sin).astype(BF16)
        else:
            o_ref[seg] = acc.astype(BF16)
    for v in range(2):
        vt_ref[v] = _dot_nt(wvt_ref[v], xb).astype(BF16)
    fft_ref[...] = _dot_nt(wff_ref[...], xb)
    u = _dot(xb, w_ref[0])
    for c in range(SSM_WIDTH // LANES):
        tmp_ref[c] = u[:, c * LANES:(c + 1) * LANES]
    for s in range(SSM_CHUNK):
        for c in range(SSM_WIDTH // LANES):
            u_ref[s, :, c * LANES:(c + 1) * LANES] = (
                tmp_ref[c, pl.ds(s, tm // SSM_CHUNK, stride=SSM_CHUNK), :].astype(BF16))


def _inproj(x, w_tok, w_vt, wff_t, cos, sin, layer):
    tm = TM_PROJ
    nrope = SEQ // tm
    return pl.pallas_call(
        _inproj_kernel,
        grid=(TOKENS // tm,),
        in_specs=[
            pl.BlockSpec((tm, D_MODEL), lambda i: (i, 0)),
            pl.BlockSpec((None, N_SEG + 1, D_MODEL, 512), lambda i: (layer, 0, 0, 0)),
            pl.BlockSpec((None, 2, 512, D_MODEL), lambda i: (layer, 0, 0, 0)),
            pl.BlockSpec((None, FOX_HEADS, D_MODEL), lambda i: (layer, 0, 0)),
            pl.BlockSpec((tm, LANES), lambda i: (i % nrope, 0)),
            pl.BlockSpec((tm, LANES), lambda i: (i % nrope, 0)),
        ],
        out_specs=[
            pl.BlockSpec((N_SEG, tm, 512), lambda i: (0, i, 0)),
            pl.BlockSpec((SSM_CHUNK, tm // SSM_CHUNK, SSM_WIDTH), lambda i: (0, i, 0)),
            pl.BlockSpec((2, 512, tm), lambda i: (0, 0, i)),
            pl.BlockSpec((FOX_HEADS, tm), lambda i: (0, i)),
        ],
        out_shape=[
            jax.ShapeDtypeStruct((N_SEG, TOKENS, 512), BF16),
            jax.ShapeDtypeStruct((SSM_CHUNK, SSM_ROWS, SSM_WIDTH), BF16),
            jax.ShapeDtypeStruct((2, 512, TOKENS), BF16),
            jax.ShapeDtypeStruct((FOX_HEADS, TOKENS), F32),
        ],
        scratch_shapes=[pltpu.VMEM((SSM_WIDTH // LANES, tm, LANES), F32)],
        compiler_params=_params("parallel"),
        name="inproj",
    )(x, w_tok, w_vt, wff_t, cos, sin)


def _fgate_kernel(fft_ref, bias_ref, c_ref):
    z = fft_ref[...] + bias_ref[...]
    c = jnp.minimum(z, 0.0) - jnp.log1p(jnp.exp(-jnp.abs(z)))
    lane = lax.broadcasted_iota(jnp.int32, c.shape, 1)
    shift = 1
    while shift < SEQ:
        c = c + jnp.where(lane >= shift, pltpu.roll(c, shift, 1), 0.0)
        shift *= 2
    c = c * LOG2E
    padded = jnp.concatenate([c, jnp.zeros((LANES - FOX_HEADS, SEQ), F32)], axis=0)
    c_ref[...] = padded.T


def _fgate(fft, bias):
    return pl.pallas_call(
        _fgate_kernel,
        grid=(BATCH,),
        in_specs=[
            pl.BlockSpec((FOX_HEADS, SEQ), lambda b: (0, b)),
            pl.BlockSpec((FOX_HEADS, 1), lambda b: (0, 0)),
        ],
        out_specs=pl.BlockSpec((SEQ, LANES), lambda b: (b, 0)),
        out_shape=jax.ShapeDtypeStruct((TOKENS, LANES), F32),
        compiler_params=_params("parallel"),
        name="fgate",
    )(fft, bias)


ONES_ROWS = 16


def _online_softmax(s, m_ref, idx):
    m_old = m_ref[idx]
    m_new = jnp.maximum(m_old, jnp.max(s, axis=0, keepdims=True))
    m_ref[idx] = m_new
    return jnp.exp2(s - m_new).astype(BF16), jnp.exp2(m_old - m_new)


def _causal_mask_t(n):
    key = lax.broadcasted_iota(jnp.int32, (n, n), 0)
    query = lax.broadcasted_iota(jnp.int32, (n, n), 1)
    return key <= query


def _init_softmax_state(m_ref, acc_ref):
    m_ref[...] = jnp.full(m_ref.shape, NEG_INF, F32)
    acc_ref[...] = jnp.zeros(acc_ref.shape, F32)


def _with_ones(vt):
    return jnp.concatenate([vt, jnp.ones((ONES_ROWS, vt.shape[1]), BF16)], axis=0)


def _causal_sweep(i, tq, scores_of, consume):
    def keys(j):
        return pl.ds(pl.multiple_of(j * tq, tq), tq)

    def step(blocks):
        scores = [scores_of(k) for k, _ in blocks]
        for (k, masked), s in zip(blocks, scores):
            consume(s, k, masked)

    def pair(j, carry):
        step([(keys(2 * j), False), (keys(2 * j + 1), False)])
        return carry

    lax.fori_loop(0, i >> 1, pair, 0)

    @pl.when((i & 1) == 1)
    def _():
        step([(keys(i - 1), False), (keys(i), True)])

    @pl.when((i & 1) == 0)
    def _():
        step([(keys(i), True)])


BIAS_LANES = 8


def _split3(c):
    hi = c.astype(BF16)
    rest = c - hi.astype(F32)
    mid = rest.astype(BF16)
    lo = (rest - mid.astype(F32)).astype(BF16)
    return jnp.concatenate([hi, mid, lo], axis=1)


def _bias_select(pair, key_side):
    r = lax.broadcasted_iota(jnp.int32, (3 * LANES, 1), 0)
    d = lax.broadcasted_iota(jnp.int32, (1, LANES), 1)
    piece, src = r >> 7, r & (LANES - 1)
    shift = 3 if key_side else 0
    first = jnp.logical_and(src == 2 * pair, d == piece + shift)
    second = jnp.logical_and(src == 2 * pair + 1, d == piece + shift + BIAS_LANES)
    value = -1.0 if key_side else 1.0
    return jnp.where(jnp.logical_or(first, second), value, 0.0).astype(BF16)


def _bias_ones(key_side):
    d = lax.broadcasted_iota(jnp.int32, (1, LANES), 1)
    within = d & (BIAS_LANES - 1)
    hit = (within < 3) if key_side else jnp.logical_and(within >= 3, within < 6)
    return jnp.where(jnp.logical_and(hit, d < 2 * BIAS_LANES), 1.0, 0.0)


def _fox_kernel(q_ref, k_ref, vt_ref, c_ref, o_ref, m_ref, acc_ref, kb_ref):
    i = pl.program_id(1)
    tq = q_ref.shape[0]
    hd = FOX_HEAD_DIM
    npair = FOX_HEADS // 2
    lane = lax.broadcasted_iota(jnp.int32, (tq, LANES), 1)
    lo = lane < hd
    lo_bias = lane < BIAS_LANES
    _init_softmax_state(m_ref, acc_ref)

    @pl.when(i == 0)
    def _():
        pieces = _split3(c_ref[...])
        for hp in range(npair):
            kb_ref[hp] = (_dot(pieces, _bias_select(hp, True)) + _bias_ones(True)).astype(BF16)

    q_pieces = _split3(c_ref[pl.ds(pl.multiple_of(i * tq, tq), tq), :])
    q_ops = []
    for hp in range(npair):
        q = q_ref[:, hp * LANES:(hp + 1) * LANES]
        zero = jnp.zeros_like(q)
        qb = (_dot(q_pieces, _bias_select(hp, False)) + _bias_ones(False)).astype(BF16)
        q_ops.append(jnp.concatenate([jnp.where(lo, q, zero), jnp.where(lo_bias, qb, zero)], axis=1))
        q_ops.append(jnp.concatenate([jnp.where(lo, zero, q), jnp.where(lo_bias, zero, qb)], axis=1))

    def scores_of(keys):
        out = []
        for hp in range(npair):
            k_op = jnp.concatenate([k_ref[keys, hp * LANES:(hp + 1) * LANES], kb_ref[hp, keys, :]], axis=1)
            out.append(_dot_nt(k_op, q_ops[2 * hp]))
            out.append(_dot_nt(k_op, q_ops[2 * hp + 1]))
        return out

    def consume(scores, keys, masked):
        probs = []
        for h in range(FOX_HEADS):
            s = scores[h]
            if masked:
                s = jnp.where(_causal_mask_t(tq), s, NEG_INF)
            probs.append(_online_softmax(s, m_ref, h))
        for h in range(FOX_HEADS):
            p, alpha = probs[h]
            pv = _dot(_with_ones(vt_ref[h * hd:(h + 1) * hd, keys]), p)
            acc_ref[h] = acc_ref[h] * alpha + pv

    _causal_sweep(i, tq, scores_of, consume)
    for hp in range(npair):
        heads = [acc_ref[h, :hd, :] / acc_ref[h, hd:hd + 1, :] for h in (2 * hp, 2 * hp + 1)]
        o_ref[:, hp * LANES:(hp + 1) * LANES] = jnp.concatenate(heads, axis=0).T.astype(BF16)


def _fox(proj, vt, c):
    nq = SEQ // TQ
    return pl.pallas_call(
        _fox_kernel,
        grid=(BATCH, nq),
        in_specs=[
            pl.BlockSpec((None, TQ, 512), lambda b, i: (SEG_FQ, b * nq + i, 0)),
            pl.BlockSpec((None, SEQ, 512), lambda b, i: (SEG_FK, b, 0)),
            pl.BlockSpec((None, 512, SEQ), lambda b, i: (VT_FOX, 0, b)),
            pl.BlockSpec((SEQ, LANES), lambda b, i: (b, 0)),
        ],
        out_specs=pl.BlockSpec((TQ, 512), lambda b, i: (b * nq + i, 0)),
        out_shape=jax.ShapeDtypeStruct((TOKENS, 512), BF16),
        scratch_shapes=[pltpu.VMEM((FOX_HEADS, 1, TQ), F32),
                        pltpu.VMEM((FOX_HEADS, FOX_HEAD_DIM + ONES_ROWS, TQ), F32),
                        pltpu.VMEM((FOX_HEADS // 2, SEQ, LANES), BF16)],
        compiler_params=_params("parallel", "arbitrary"),
        name="fox_attn",
    )(proj, proj, vt, c)


def _diff_kernel(lam_init, q_ref, k_ref, vt_ref, lam_ref, g_ref, o_ref, m_ref, acc_ref):
    i = pl.program_id(1)
    tq = q_ref.shape[0]
    dv = LANES
    lane = lax.broadcasted_iota(jnp.int32, (tq, LANES), 1)
    lo = lane < DIFF_HEAD_DIM
    _init_softmax_state(m_ref, acc_ref)

    def scores_of(keys):
        out = []
        for h in range(DIFF_HEADS):
            cols = slice(h * LANES, (h + 1) * LANES)
            q = q_ref[:, cols]
            zero = jnp.zeros_like(q)
            kb = k_ref[keys, cols]
            out.append(_dot_nt(kb, jnp.where(lo, q, zero)))
            out.append(_dot_nt(kb, jnp.where(lo, zero, q)))
        return out

    def consume(scores, keys, masked):
        probs = []
        for idx in range(2 * DIFF_HEADS):
            s = scores[idx]
            if masked:
                s = jnp.where(_causal_mask_t(tq), s, NEG_INF)
            probs.append(_online_softmax(s, m_ref, idx))
        for h in range(DIFF_HEADS):
            vtb = _with_ones(vt_ref[h * dv:(h + 1) * dv, keys])
            for idx in (2 * h, 2 * h + 1):
                p, alpha = probs[idx]
                acc_ref[idx] = acc_ref[idx] * alpha + _dot(vtb, p)

    _causal_sweep(i, tq, scores_of, consume)
    lam_rows = lam_ref[...]
    lam = (jnp.exp(jnp.sum(lam_rows[0:1] * lam_rows[1:2], axis=1, keepdims=True))
           - jnp.exp(jnp.sum(lam_rows[2:3] * lam_rows[3:4], axis=1, keepdims=True)) + lam_init)
    for h in range(DIFF_HEADS):
        cols = slice(h * LANES, (h + 1) * LANES)
        o1 = acc_ref[2 * h, :dv, :] / acc_ref[2 * h, dv:dv + 1, :]
        o2 = acc_ref[2 * h + 1, :dv, :] / acc_ref[2 * h + 1, dv:dv + 1, :]
        o = o1 - lam * o2
        o = o * lax.rsqrt(jnp.mean(o * o, axis=0, keepdims=True) + RMS_EPS)
        o_ref[:, cols] = (o.T * g_ref[:, cols] * (1.0 - lam_init)).astype(BF16)


def _diff(proj, vt, lam, g, lam_init):
    nq = SEQ // TQ
    return pl.pallas_call(
        functools.partial(_diff_kernel, lam_init),
        grid=(BATCH, nq),
        in_specs=[
            pl.BlockSpec((None, TQ, 512), lambda b, i: (SEG_DQ, b * nq + i, 0)),
            pl.BlockSpec((None, SEQ, 512), lambda b, i: (SEG_DK, b, 0)),
            pl.BlockSpec((None, 512, SEQ), lambda b, i: (VT_DIFF, 0, b)),
            pl.BlockSpec((4, DIFF_HEAD_DIM), lambda b, i: (0, 0)),
            pl.BlockSpec((1, 512), lambda b, i: (0, 0)),
        ],
        out_specs=pl.BlockSpec((TQ, 512), lambda b, i: (b * nq + i, 0)),
        out_shape=jax.ShapeDtypeStruct((TOKENS, 512), BF16),
        scratch_shapes=[pltpu.VMEM((2 * DIFF_HEADS, 1, TQ), F32),
                        pltpu.VMEM((2 * DIFF_HEADS, LANES + ONES_ROWS, TQ), F32)],
        compiler_params=_params("parallel", "arbitrary"),
        name="diff_attn",
    )(proj, proj, vt, lam, g)


def _spread(compact, select, row_group, col_group):
    full = _dot(compact, select)
    return jnp.where(row_group == col_group, full, 0.0).astype(BF16)


def _s5_kernel(u_ref, am_ref, abre_ref, abim_ref, acre_ref, acim_ref, are_ref, aim_ref, y_ref, *state):
    nslab = SSM_SCOLS // LANES
    sre, sim = state[:nslab], state[nslab:]

    def iota(shape, dim):
        return lax.broadcasted_iota(jnp.int32, shape, dim)

    i_tn, j_cols = iota((SSM_TN, 1), 0), iota((1, SSM_COLS), 1)
    sel_tn = jnp.where(jnp.logical_and((i_tn >> 4) == (j_cols >> 7), (i_tn & 15) == (j_cols & 15)),
                       1.0, 0.0).astype(BF16)
    i_p, j_sc = iota((SSM_STATE, 1), 0), iota((1, SSM_SCOLS), 1)
    sel_p = jnp.where(i_p == (j_sc & (SSM_STATE - 1)), 1.0, 0.0).astype(BF16)
    grp_rows_cols = (iota((SSM_COLS, 1), 0) >> 4) & (SSM_GBLK - 1)
    grp_cols_cols = (j_cols >> 4) & (SSM_GBLK - 1)
    grp_rows_state = iota((SSM_SCOLS, 1), 0) >> 6
    grp_cols_state = j_sc >> 6
    m = _spread(am_ref[...], sel_tn, grp_rows_cols, grp_cols_cols)
    bre = _spread(abre_ref[...], sel_p, grp_rows_cols, grp_cols_state)
    bim = _spread(abim_ref[...], sel_p, grp_rows_cols, grp_cols_state)
    cre = _spread(acre_ref[...], sel_tn, grp_rows_state, grp_cols_cols)
    cim = _spread(acim_ref[...], sel_tn, grp_rows_state, grp_cols_cols)

    u = jnp.concatenate([u_ref[s] for s in range(SSM_CHUNK)], axis=1)
    for k in range(0, nslab, 2):
        cols = slice(k * LANES, (k + 2) * LANES)
        for dst, b in ((sre, bre), (sim, bim)):
            part = _dot(u, b[:, cols])
            dst[k][...] = part[:, :LANES]
            dst[k + 1][...] = part[:, LANES:]
    ar = [are_ref[:, k * LANES:(k + 1) * LANES] for k in range(nslab)]
    ai = [aim_ref[:, k * LANES:(k + 1) * LANES] for k in range(nslab)]

    def body(c, carry):
        rows = pl.ds(c, BATCH, stride=SSM_NCHUNK)
        nxt = []
        for k in range(nslab):
            xr, xi = carry[k]
            sr = sre[k][rows, :]
            si = sim[k][rows, :]
            sre[k][rows, :] = xr
            sim[k][rows, :] = xi
            nxt.append((ar[k] * xr - ai[k] * xi + sr, ar[k] * xi + ai[k] * xr + si))
        return tuple(nxt)

    zero = jnp.zeros((BATCH, LANES), F32)
    lax.fori_loop(0, SSM_NCHUNK, body, tuple((zero, zero) for _ in range(nslab)))
    xr = jnp.concatenate([r[...] for r in sre], axis=1).astype(BF16)
    xi = jnp.concatenate([r[...] for r in sim], axis=1).astype(BF16)
    for c in range(SSM_COLS // (2 * LANES)):
        cols = slice(c * 2 * LANES, (c + 1) * 2 * LANES)
        y = _dot(u, m[:, cols]) + _dot(xr, cre[:, cols]) + _dot(xi, cim[:, cols])
        y_ref[2 * c] = y[:, :LANES].astype(BF16)
        y_ref[2 * c + 1] = y[:, LANES:].astype(BF16)


def _s5_scan(u_stack, mats, layer):
    am, abre, abim, acre, acim, are, aim = mats

    def spec(*shape):
        return pl.BlockSpec((None, None) + shape, lambda g: (layer, g) + (0,) * len(shape))

    io_spec = pl.BlockSpec((SSM_CHUNK, SSM_ROWS, LANES), lambda g: (0, 0, g))
    return pl.pallas_call(
        _s5_kernel,
        grid=(SSM_NGBLK,),
        in_specs=[io_spec, spec(SSM_COLS, SSM_TN),
                  spec(SSM_COLS, SSM_STATE), spec(SSM_COLS, SSM_STATE),
                  spec(SSM_SCOLS, SSM_TN), spec(SSM_SCOLS, SSM_TN),
                  spec(1, SSM_SCOLS), spec(1, SSM_SCOLS)],
        out_specs=io_spec,
        out_shape=jax.ShapeDtypeStruct((SSM_CHUNK, SSM_ROWS, SSM_WIDTH), BF16),
        scratch_shapes=[pltpu.VMEM((SSM_ROWS, LANES), F32) for _ in range(2 * SSM_SCOLS // LANES)],
        compiler_params=_params("parallel"),
        name="s5_scan",
    )(u_stack, am, abre, abim, acre, acim, are, aim)


def _s5_matrices(lam_re, lam_im, log_dt, b_re, b_im, c_re, c_im):
    hi = lax.Precision.HIGHEST
    L, NB, GB = SSM_CHUNK, SSM_NGBLK, SSM_GBLK
    lr = lam_re.astype(F32)
    li = lam_im.astype(F32)
    dt = jnp.exp(log_dt.astype(F32))[:, None]
    mag = jnp.exp(lr * dt)
    ang = li * dt
    ar = mag * jnp.cos(ang)
    ai = mag * jnp.sin(ang)
    er = ar - 1.0
    ei = ai
    den = lr * lr + li * li
    qr = (er * lr + ei * li) / den
    qi = (ei * lr - er * li) / den
    br = b_re.astype(F32)
    bi = b_im.astype(F32)
    bbr = qr[:, :, None] * br - qi[:, :, None] * bi
    bbi = qr[:, :, None] * bi + qi[:, :, None] * br
    k = jnp.arange(L + 1, dtype=F32)[:, None, None]
    pmag = jnp.exp(lr[None] * dt[None] * k)
    pr = pmag * jnp.cos(ang[None] * k)
    pi = pmag * jnp.sin(ang[None] * k)
    cr = c_re.astype(F32)
    ci = c_im.astype(F32)
    e_r = cr[None] * pr[:, :, None, :] - ci[None] * pi[:, :, None, :]
    e_i = cr[None] * pi[:, :, None, :] + ci[None] * pr[:, :, None, :]
    kern = (jnp.einsum('kgnp,gpm->kgnm', e_r[:L], bbr, precision=hi)
            - jnp.einsum('kgnp,gpm->kgnm', e_i[:L], bbi, precision=hi))
    lag = jnp.arange(L)[None, :] - jnp.arange(L)[:, None]
    toep = jnp.where((lag >= 0)[:, :, None, None, None], kern[jnp.clip(lag, 0, L - 1)], 0.0)
    toep = toep.reshape(L, L, NB, GB, SSM_GROUP, SSM_GROUP)
    am = jnp.transpose(toep, (2, 0, 3, 5, 1, 4)).reshape(NB, SSM_COLS, SSM_TN)
    krev = (L - 1) - jnp.arange(L, dtype=F32)[:, None, None]
    rmag = jnp.exp(lr[None] * dt[None] * krev)
    rev_r = rmag * jnp.cos(ang[None] * krev)
    rev_i = rmag * jnp.sin(ang[None] * krev)
    sb_r = rev_r[:, :, :, None] * bbr[None] - rev_i[:, :, :, None] * bbi[None]
    sb_i = rev_r[:, :, :, None] * bbi[None] + rev_i[:, :, :, None] * bbr[None]

    def state_in(t):
        t = t.reshape(L, NB, GB, SSM_STATE, SSM_GROUP)
        return jnp.transpose(t, (1, 0, 2, 4, 3)).reshape(NB, SSM_COLS, SSM_STATE)

    def state_out(t):
        t = t.reshape(L, NB, GB, SSM_GROUP, SSM_STATE)
        return jnp.transpose(t, (1, 2, 4, 0, 3)).reshape(NB, SSM_SCOLS, SSM_TN)

    are = pr[L].reshape(NB, 1, SSM_SCOLS)
    aim = pi[L].reshape(NB, 1, SSM_SCOLS)
    return (am.astype(BF16), state_in(sb_r).astype(BF16), state_in(sb_i).astype(BF16),
            state_out(e_r[1:]).astype(BF16), (-state_out(e_i[1:])).astype(BF16), are, aim)


def _s5_post_kernel(y_ref, u_ref, d_ref, w_ref, o_ref, tmp_ref):
    nrow = y_ref.shape[1]
    ycat = jnp.concatenate([y_ref[s] for s in range(SSM_CHUNK)], axis=0).astype(F32)
    ucat = jnp.concatenate([u_ref[s] for s in range(SSM_CHUNK)], axis=0).astype(F32)
    y = jax.nn.gelu(ycat + d_ref[...] * ucat, approximate=True)
    z = _dot(y.astype(BF16), w_ref[...])
    out = z[:, :SSM_WIDTH] * jax.nn.sigmoid(z[:, SSM_WIDTH:])
    for s in range(SSM_CHUNK):
        for c in range(SSM_WIDTH // LANES):
            tmp_ref[c, pl.ds(s, nrow, stride=SSM_CHUNK), :] = (
                out[s * nrow:(s + 1) * nrow, c * LANES:(c + 1) * LANES])
    for c in range(SSM_WIDTH // LANES):
        o_ref[:, c * LANES:(c + 1) * LANES] = tmp_ref[c].astype(BF16)


def _s5_post(y_stack, u_stack, d, w_glu, layer):
    tm = TM_POST
    stack_spec = pl.BlockSpec((SSM_CHUNK, tm // SSM_CHUNK, SSM_WIDTH), lambda i: (0, i, 0))
    return pl.pallas_call(
        _s5_post_kernel,
        grid=(TOKENS // tm,),
        in_specs=[
            stack_spec,
            stack_spec,
            pl.BlockSpec((1, SSM_WIDTH), lambda i: (0, 0)),
            pl.BlockSpec((None, SSM_WIDTH, 2 * SSM_WIDTH), lambda i: (layer, 0, 0)),
        ],
        out_specs=pl.BlockSpec((tm, SSM_WIDTH), lambda i: (i, 0)),
        out_shape=jax.ShapeDtypeStruct((TOKENS, SSM_WIDTH), BF16),
        scratch_shapes=[pltpu.VMEM((SSM_WIDTH // LANES, tm, LANES), F32)],
        compiler_params=_params("parallel"),
        name="s5_post",
    )(y_stack, u_stack, d, w_glu)


def _route(x, w_ref, b_ref):
    xh = x.astype(BF16)
    xl = (x - xh.astype(F32)).astype(BF16)
    hw = _dot(xh, w_ref[...])
    logits = hw[:, :LANES] + (hw[:, LANES:] + _dot(xl, w_ref[:, :LANES])) + b_ref[...]
    lane = lax.broadcasted_iota(jnp.int32, logits.shape, 1).astype(F32)
    big = jnp.float32(LANES)
    neg = jnp.float32(-jnp.inf)

    def first_argmax(v, vmax):
        return jnp.min(jnp.where(v == vmax, lane, big), axis=1, keepdims=True)

    gl = jnp.where(lane < N_EXPERT_GROUPS, logits, neg)
    ge = jnp.exp(gl - jnp.max(gl, axis=1, keepdims=True))
    pg = ge / jnp.sum(ge, axis=1, keepdims=True)
    gp = jnp.max(pg, axis=1, keepdims=True)
    gi = first_argmax(pg, gp)
    base = N_EXPERT_GROUPS + EXPERTS_PER_GROUP * gi
    sel = jnp.logical_and(lane >= base, lane < base + EXPERTS_PER_GROUP)
    el = jnp.where(sel, logits, neg)
    ev1 = jnp.max(el, axis=1, keepdims=True)
    i1 = first_argmax(el, ev1)
    el2 = jnp.where(lane == i1, neg, el)
    ev2 = jnp.max(el2, axis=1, keepdims=True)
    i2 = first_argmax(el2, ev2)
    e2 = jnp.exp(ev2 - ev1)
    den = 1.0 + e2
    w1 = gp * (1.0 / den)
    w2 = gp * (e2 / den)
    return (jnp.where(lane == i1, w1, 0.0) + jnp.where(lane == i2, w2, 0.0)
            + jnp.where(lane == 0.0, gi, 0.0))


def _router_kernel(x_ref, w_ref, b_ref, o_ref):
    o_ref[...] = _route(x_ref[...], w_ref, b_ref)


def _router(x, w_r, b_r, layer):
    tm = TM_ROUTE
    return pl.pallas_call(
        _router_kernel,
        grid=(TOKENS // tm,),
        in_specs=[
            pl.BlockSpec((tm, D_MODEL), lambda i: (i, 0)),
            pl.BlockSpec((None, D_MODEL, 2 * LANES), lambda i: (layer, 0, 0)),
            pl.BlockSpec((None, 1, LANES), lambda i: (layer, 0, 0)),
        ],
        out_specs=pl.BlockSpec((tm, LANES), lambda i: (i, 0)),
        out_shape=jax.ShapeDtypeStruct((TOKENS, LANES), F32),
        compiler_params=_params("parallel"),
        name="router",
    )(x, w_r, b_r)


def _merge_kernel(x_ref, ys_ref, yd_ref, yf_ref, wgl_ref, wbr_ref, wout_ref, g_ref, b_ref, o_ref):
    x = x_ref[...]
    xb = x.astype(BF16)
    merged = None
    for n, y_ref in enumerate((ys_ref, yd_ref, yf_ref)):
        gate = jax.nn.sigmoid(_dot(xb, wgl_ref[:, n * D_MODEL:(n + 1) * D_MODEL]))
        term = gate * _dot(y_ref[...], wbr_ref[n])
        merged = term if merged is None else merged + term
    mix = _dot(merged.astype(BF16), wout_ref[...])
    o_ref[...] = _layer_norm(ALPHA * x + mix, g_ref[...], b_ref[...])


def _merge(x, ys, yd, yf, wgl, wbr, wout, g, b, layer):
    tm = TM_MERGE
    row = lambda i: (i, 0)
    const2 = lambda i: (0, 0)
    per_layer = lambda i: (layer, 0, 0)
    return pl.pallas_call(
        _merge_kernel,
        grid=(TOKENS // tm,),
        in_specs=[
            pl.BlockSpec((tm, D_MODEL), row),
            pl.BlockSpec((tm, 512), row),
            pl.BlockSpec((tm, 512), row),
            pl.BlockSpec((tm, 512), row),
            pl.BlockSpec((None, D_MODEL, N_BRANCH * D_MODEL), per_layer),
            pl.BlockSpec((None, N_BRANCH, BRANCH_WIDTH, D_MODEL), lambda i: (layer, 0, 0, 0)),
            pl.BlockSpec((None, D_MODEL, D_MODEL), per_layer),
            pl.BlockSpec((1, D_MODEL), const2),
            pl.BlockSpec((1, D_MODEL), const2),
        ],
        out_specs=pl.BlockSpec((tm, D_MODEL), row),
        out_shape=jax.ShapeDtypeStruct((TOKENS, D_MODEL), F32),
        compiler_params=_params("parallel"),
        name="merge_ln",
    )(x, ys, yd, yf, wgl, wbr, wout, g, b)


MOE_SUB = 128


def _one_hot(cond):
    return jnp.where(cond, 1.0, 0.0).astype(BF16)


def _moe_sort(x_ref, gate_ref, xs_ref, gs_ref, ys_ref, pos_ref, off_ref):
    tm = x_ref.shape[0]
    gates = gate_ref[...]
    lane = lax.broadcasted_iota(jnp.int32, (1, LANES), 1)
    row8 = lax.broadcasted_iota(jnp.int32, (8, 1), 0)
    t_col = lax.broadcasted_iota(jnp.int32, (tm, 1), 0)
    t_row = lax.broadcasted_iota(jnp.int32, (1, tm), 1)
    member = jnp.where(gates[:, 0:1] == lane.astype(F32), 1.0, 0.0)
    member_b = member.astype(BF16)
    counts = jnp.sum(member, axis=0, keepdims=True)
    cnt = [jnp.sum(jnp.where(lane == k, counts, 0.0), axis=1, keepdims=True)
           for k in range(N_EXPERT_GROUPS - 1)]
    start = [cnt[0], cnt[0] + cnt[1], cnt[0] + cnt[1] + cnt[2]]
    off_ref[0] = 0
    for k in range(N_EXPERT_GROUPS - 1):
        off_ref[k + 1] = start[k][0, 0].astype(jnp.int32)
    off_ref[N_EXPERT_GROUPS] = tm

    def offsets(index):
        return sum(jnp.where(index > k, cnt[k], 0.0) for k in range(N_EXPERT_GROUPS - 1))

    earlier = _one_hot(t_row < t_col)
    rank_col = _dot(earlier, member_b)
    pos_col = jnp.sum(member * (offsets(lane) + rank_col), axis=1, keepdims=True)
    pos_ref[...] = jnp.broadcast_to(pos_col, (tm, LANES))
    pick = _one_hot(lax.broadcasted_iota(jnp.int32, (8, LANES), 0)
                    == lax.broadcasted_iota(jnp.int32, (8, LANES), 1))
    member_t = _dot_nt(pick, member_b)
    rank_row = _dot(member_t.astype(BF16), _one_hot(t_col < t_row))
    pos_row = jnp.sum(member_t * (offsets(row8) + rank_row), axis=0, keepdims=True)
    perm = _one_hot(t_col.astype(F32) == pos_row)
    xs_ref[...] = _dot(perm, x_ref[...].astype(BF16)).astype(BF16)
    g_hi = gates.astype(BF16)
    rest = gates - g_hi.astype(F32)
    g_mid = rest.astype(BF16)
    g_lo = (rest - g_mid.astype(F32)).astype(BF16)
    moved = _dot(perm, jnp.concatenate([g_hi, g_mid, g_lo], axis=1))
    gs_ref[...] = moved[:, :LANES] + moved[:, LANES:2 * LANES] + moved[:, 2 * LANES:]
    ys_ref[...] = jnp.zeros(ys_ref.shape, F32)


def _moe_kernel(x_ref, gate_ref, wg_ref, wu_ref, wd_ref, g_ref, b_ref, o_ref,
                xs_ref, gs_ref, ys_ref, pos_ref, off_ref):
    grp = pl.program_id(1)
    tm = x_ref.shape[0]

    @pl.when(grp == 0)
    def _():
        _moe_sort(x_ref, gate_ref, xs_ref, gs_ref, ys_ref, pos_ref, off_ref)

    first = N_EXPERT_GROUPS + EXPERTS_PER_GROUP * grp
    shift = MOE_SUB.bit_length() - 1

    def sub_tile(k, carry):
        rows = pl.ds(pl.multiple_of(k * MOE_SUB, MOE_SUB), MOE_SUB)
        xb = xs_ref[rows, :]
        gates = gs_ref[rows, :]
        lane = lax.broadcasted_iota(jnp.int32, gates.shape, 1)
        hidden = []
        for e in range(EXPERTS_PER_GROUP):
            gate = jnp.sum(jnp.where(lane == first + e, gates, 0.0), axis=1, keepdims=True)
            he = jax.nn.silu(_dot(xb, wg_ref[e])) * _dot(xb, wu_ref[e]) * gate
            hidden.append(he.astype(BF16))
        ys_ref[rows, :] += _dot(jnp.concatenate(hidden, axis=1), wd_ref[...])
        return carry

    lo = off_ref[grp] >> shift
    hi = (off_ref[grp + 1] + (MOE_SUB - 1)) >> shift
    lax.fori_loop(lo, hi, sub_tile, 0)

    @pl.when(grp == N_EXPERT_GROUPS - 1)
    def _():
        slot = lax.broadcasted_iota(jnp.int32, (1, tm), 1).astype(F32)
        unsort = _one_hot(pos_ref[:, 0:1] == slot)
        y = _dot(unsort, ys_ref[...].astype(BF16))
        o_ref[...] = _layer_norm(ALPHA * x_ref[...] + y, g_ref[...], b_ref[...])


def _moe(x, gates, wg, wu, wd_grouped, g, b, layer):
    tm = TM_MOE
    return pl.pallas_call(
        _moe_kernel,
        grid=(TOKENS // tm, N_EXPERT_GROUPS),
        in_specs=[
            pl.BlockSpec((tm, D_MODEL), lambda i, e: (i, 0)),
            pl.BlockSpec((tm, LANES), lambda i, e: (i, 0)),
            pl.BlockSpec((None, EXPERTS_PER_GROUP, D_MODEL, D_EXPERT), lambda i, e: (layer, e, 0, 0)),
            pl.BlockSpec((None, EXPERTS_PER_GROUP, D_MODEL, D_EXPERT), lambda i, e: (layer, e, 0, 0)),
            pl.BlockSpec((None, None, EXPERTS_PER_GROUP * D_EXPERT, D_MODEL), lambda i, e: (layer, e, 0, 0)),
            pl.BlockSpec((1, D_MODEL), lambda i, e: (0, 0)),
            pl.BlockSpec((1, D_MODEL), lambda i, e: (0, 0)),
        ],
        out_specs=pl.BlockSpec((tm, D_MODEL), lambda i, e: (i, 0)),
        out_shape=jax.ShapeDtypeStruct((TOKENS, D_MODEL), F32),
        scratch_shapes=[pltpu.VMEM((tm, D_MODEL), BF16),
                        pltpu.VMEM((tm, LANES), F32),
                        pltpu.VMEM((tm, D_MODEL), F32),
                        pltpu.VMEM((tm, LANES), F32),
                        pltpu.SMEM((8,), jnp.int32)],
        compiler_params=_params("parallel", "arbitrary"),
        name="moe_ln",
    )(x, gates, wg, wu, wd_grouped, g, b)


def _rope_tables():
    pos = jnp.arange(SEQ, dtype=F32)
    inv_freq = ROPE_THETA ** (-jnp.arange(0, DIFF_HEAD_DIM, 2, dtype=F32) / DIFF_HEAD_DIM)
    ang = pos[:, None] * inv_freq[None, :]
    emb = jnp.concatenate([ang, ang], axis=-1)
    cos = jnp.cos(emb)
    sin = jnp.sin(emb)
    sign = jnp.where(jnp.arange(DIFF_HEAD_DIM) < DIFF_HEAD_DIM // 2, -1.0, 1.0).astype(F32)
    return jnp.tile(cos, (1, 2)), jnp.tile(sin * sign, (1, 2))


def _split_w_in(w_in):
    qscale = DIFF_HEAD_DIM ** -0.5 * LOG2E

    def seg(k):
        return w_in[:, :, k * 512:(k + 1) * 512]

    w_tok = jnp.stack([seg(0), seg(1) * qscale, seg(2), seg(4) * qscale, seg(5)], axis=1).astype(BF16)
    w_vt = jnp.stack([jnp.swapaxes(seg(3), 1, 2), jnp.swapaxes(seg(6), 1, 2)], axis=1).astype(BF16)
    off = 7 * 512
    wff_t = jnp.swapaxes(w_in[:, :, off:off + FOX_HEADS], 1, 2).astype(BF16)
    wgl = w_in[:, :, off + FOX_HEADS:].astype(BF16)
    return w_tok, w_vt, wff_t, wgl


def kernel(x, w_in, w_branch, w_out, ssm_lambda_re, ssm_lambda_im, ssm_log_dt, ssm_b_re, ssm_b_im,
           ssm_c_re, ssm_c_im, ssm_d, ssm_w_glu, diff_lambda, diff_norm_g, fox_f_bias, ln1_g, ln1_b,
           moe_w_group, moe_b_group, moe_w_expert, moe_b_expert, moe_w_gate, moe_w_up, moe_w_down,
           ln2_g, ln2_b):
    cos, sin = _rope_tables()
    w_tok, w_vt, wff_t, wgl = _split_w_in(w_in)
    s5_mats = jax.vmap(_s5_matrices)(ssm_lambda_re, ssm_lambda_im, ssm_log_dt, ssm_b_re, ssm_b_im,
                                     ssm_c_re, ssm_c_im)
    w_glu = ssm_w_glu.astype(BF16)
    w_br = w_branch.astype(BF16)
    w_o = w_out.astype(BF16)
    w_gate, w_up = moe_w_gate.astype(BF16), moe_w_up.astype(BF16)
    w_down = moe_w_down.astype(BF16).reshape(DEPTH, N_EXPERT_GROUPS, EXPERTS_PER_GROUP * D_EXPERT, D_MODEL)
    w_r = jnp.concatenate([moe_w_group, moe_w_expert], axis=2).astype(F32)
    w_r = jnp.pad(w_r, ((0, 0), (0, 0), (0, LANES - w_r.shape[2])))
    w_r_hi = w_r.astype(BF16)
    w_r = jnp.concatenate([w_r_hi, (w_r - w_r_hi.astype(F32)).astype(BF16)], axis=2)
    b_r = jnp.concatenate([moe_b_group, moe_b_expert], axis=1).astype(F32)
    b_r = jnp.pad(b_r, ((0, 0), (0, LANES - b_r.shape[1]))).reshape(DEPTH, 1, LANES)

    h = x.reshape(TOKENS, D_MODEL)
    for l in range(DEPTH):
        lam_init = 0.8 - 0.6 * math.exp(-0.3 * l)
        proj, u_stack, vt, fft = _inproj(h, w_tok, w_vt, wff_t, cos, sin, l)

        y_stack = _s5_scan(u_stack, s5_mats, l)
        y_ssm = _s5_post(y_stack, u_stack, ssm_d[l].reshape(1, SSM_WIDTH).astype(F32), w_glu, l)

        y_diff = _diff(proj, vt, diff_lambda[l].astype(F32),
                       diff_norm_g[l].reshape(1, 512).astype(F32), lam_init)
        c = _fgate(fft, fox_f_bias[l].reshape(FOX_HEADS, 1).astype(F32))
        y_fox = _fox(proj, vt, c)

        h = _merge(h, y_ssm, y_diff, y_fox, wgl, w_br, w_o,
                   ln1_g[l].reshape(1, D_MODEL), ln1_b[l].reshape(1, D_MODEL), l)

        gates = _router(h, w_r, b_r, l)
        h = _moe(h, gates, w_gate, w_up, w_down,
                 ln2_g[l].reshape(1, D_MODEL), ln2_b[l].reshape(1, D_MODEL), l)
    return h.reshape(BATCH, SEQ, D_MODEL)
```

```python
import functools
import math

import jax
import jax.numpy as jnp
from jax import lax
from jax.experimental import pallas as pl
from jax.experimental.pallas import tpu as pltpu

F32 = jnp.float32
BF16 = jnp.bfloat16

D_MODEL = 1024
BATCH = 8
SEQ = 2048
DEPTH = 4
TOKENS = BATCH * SEQ

SSM_WIDTH = 512
SSM_GROUP = 16
SSM_GROUPS = 32
SSM_STATE = 64
SSM_CHUNK = 8
SSM_NCHUNK = SEQ // SSM_CHUNK
SSM_ROWS = BATCH * SSM_NCHUNK
SSM_GBLK = 8
SSM_NGBLK = SSM_GROUPS // SSM_GBLK
SSM_COLS = SSM_CHUNK * 128
SSM_SCOLS = SSM_GBLK * SSM_STATE
SSM_TN = SSM_CHUNK * SSM_GROUP

DIFF_HEADS = 4
DIFF_HEAD_DIM = 64
FOX_HEADS = 8
FOX_HEAD_DIM = 64
BRANCH_WIDTH = 512
N_BRANCH = 3
ROPE_THETA = 10000.0

N_EXPERT_GROUPS = 4
EXPERTS_PER_GROUP = 4
N_EXPERTS = 16
D_EXPERT = 256

ALPHA = (2 * DEPTH) ** 0.25
LN_EPS = 1e-5
RMS_EPS = 1e-6
NEG_INF = -1e30
LOG2E = math.log2(math.e)

LANES = 128
N_SEG = 4
SEG_DQ, SEG_DK, SEG_FQ, SEG_FK = range(N_SEG)
VT_DIFF, VT_FOX = 0, 1
VMEM_LIMIT = 56 * 1024 * 1024

TM_PROJ = 512
TQ = 256
TM_POST = 512
TM_MERGE = 256
TM_ROUTE = 512
TM_MOE = 1024


def _params(*sem):
    return pltpu.CompilerParams(dimension_semantics=sem, vmem_limit_bytes=VMEM_LIMIT)


def _dot(a, b):
    return jnp.dot(a, b, preferred_element_type=F32)


def _dot_nt(a, b):
    return lax.dot_general(a, b, (((1,), (1,)), ((), ())), preferred_element_type=F32)


def _layer_norm(y, g, b):
    mu = jnp.mean(y, axis=-1, keepdims=True)
    d = y - mu
    var = jnp.mean(d * d, axis=-1, keepdims=True)
    return d * lax.rsqrt(var + LN_EPS) * g + b


def _inproj_kernel(x_ref, w_ref, wvt_ref, wff_ref, cos_ref, sin_ref, o_ref, u_ref, vt_ref, fft_ref,
                   tmp_ref):
    tm = x_ref.shape[0]
    xb = x_ref[...].astype(BF16)
    cos = cos_ref[...]
    sin = sin_ref[...]
    lane = lax.broadcasted_iota(jnp.int32, cos.shape, 1)
    first_half = (lane & 32) == 0
    for seg in range(N_SEG):
        acc = _dot(xb, w_ref[seg + 1])
        if seg in (SEG_DQ, SEG_DK):
            for c in range(512 // LANES):
                t = acc[:, c * LANES:(c + 1) * LANES]
                rot = jnp.where(first_half, pltpu.roll(t, LANES - 32, 1), pltpu.roll(t, 32, 1))
                o_ref[seg, :, c * LANES:(c + 1) * LANES] = (t * cos + rot * sin).astype(BF16)
        else:
            o_ref[seg] = acc.astype(BF16)
    for v in range(2):
        vt_ref[v] = _dot_nt(wvt_ref[v], xb).astype(BF16)
    fft_ref[...] = _dot_nt(wff_ref[...], xb)
    u = _dot(xb, w_ref[0])
    for c in range(SSM_WIDTH // LANES):
        tmp_ref[c] = u[:, c * LANES:(c + 1) * LANES]
    for s in range(SSM_CHUNK):
        for c in range(SSM_WIDTH // LANES):
            u_ref[s, :, c * LANES:(c + 1) * LANES] = (
                tmp_ref[c, pl.ds(s, tm // SSM_CHUNK, stride=SSM_CHUNK), :].astype(BF16))


def _inproj(x, w_tok, w_vt, wff_t, cos, sin, layer):
    tm = TM_PROJ
    nrope = SEQ // tm
    return pl.pallas_call(
        _inproj_kernel,
        grid=(TOKENS // tm,),
        in_specs=[
            pl.BlockSpec((tm, D_MODEL), lambda i: (i, 0)),
            pl.BlockSpec((None, N_SEG + 1, D_MODEL, 512), lambda i: (layer, 0, 0, 0)),
            pl.BlockSpec((None, 2, 512, D_MODEL), lambda i: (layer, 0, 0, 0)),
            pl.BlockSpec((None, FOX_HEADS, D_MODEL), lambda i: (layer, 0, 0)),
            pl.BlockSpec((tm, LANES), lambda i: (i % nrope, 0)),
            pl.BlockSpec((tm, LANES), lambda i: (i % nrope, 0)),
        ],
        out_specs=[
            pl.BlockSpec((N_SEG, tm, 512), lambda i: (0, i, 0)),
            pl.BlockSpec((SSM_CHUNK, tm // SSM_CHUNK, SSM_WIDTH), lambda i: (0, i, 0)),
            pl.BlockSpec((2, 512, tm), lambda i: (0, 0, i)),
            pl.BlockSpec((FOX_HEADS, tm), lambda i: (0, i)),
        ],
        out_shape=[
            jax.ShapeDtypeStruct((N_SEG, TOKENS, 512), BF16),
            jax.ShapeDtypeStruct((SSM_CHUNK, SSM_ROWS, SSM_WIDTH), BF16),
            jax.ShapeDtypeStruct((2, 512, TOKENS), BF16),
            jax.ShapeDtypeStruct((FOX_HEADS, TOKENS), F32),
        ],
        scratch_shapes=[pltpu.VMEM((SSM_WIDTH // LANES, tm, LANES), F32)],
        compiler_params=_params("parallel"),
        name="inproj",
    )(x, w_tok, w_vt, wff_t, cos, sin)


def _fgate_kernel(fft_ref, bias_ref, c_ref):
    z = fft_ref[...] + bias_ref[...]
    c = jnp.minimum(z, 0.0) - jnp.log1p(jnp.exp(-jnp.abs(z)))
    lane = lax.broadcasted_iota(jnp.int32, c.shape, 1)
    shift = 1
    while shift < SEQ:
        c = c + jnp.where(lane >= shift, pltpu.roll(c, shift, 1), 0.0)
        shift *= 2
    c = c * LOG2E
    padded = jnp.concatenate([c, jnp.zeros((LANES - FOX_HEADS, SEQ), F32)], axis=0)
    c_ref[...] = padded.T


def _fgate(fft, bias):
    return pl.pallas_call(
        _fgate_kernel,
        grid=(BATCH,),
        in_specs=[
            pl.BlockSpec((FOX_HEADS, SEQ), lambda b: (0, b)),
            pl.BlockSpec((FOX_HEADS, 1), lambda b: (0, 0)),
        ],
        out_specs=pl.BlockSpec((SEQ, LANES), lambda b: (b, 0)),
        out_shape=jax.ShapeDtypeStruct((TOKENS, LANES), F32),
        compiler_params=_params("parallel"),
        name="fgate",
    )(fft, bias)


ONES_ROWS = 16


def _online_softmax(s, m_ref, idx):
    m_old = m_ref[idx]
    m_new = jnp.maximum(m_old, jnp.max(s, axis=0, keepdims=True))
    m_ref[idx] = m_new
    return jnp.exp2(s - m_new).astype(BF16), jnp.exp2(m_old - m_new)


def _causal_mask_t(n):
    key = lax.broadcasted_iota(jnp.int32, (n, n), 0)
    query = lax.broadcasted_iota(jnp.int32, (n, n), 1)
    return key <= query


def _init_softmax_state(m_ref, acc_ref):
    m_ref[...] = jnp.full(m_ref.shape, NEG_INF, F32)
    acc_ref[...] = jnp.zeros(acc_ref.shape, F32)


def _with_ones(vt):
    return jnp.concatenate([vt, jnp.ones((ONES_ROWS, vt.shape[1]), BF16)], axis=0)


def _causal_sweep(i, tq, scores_of, consume):
    def keys(j):
        return pl.ds(pl.multiple_of(j * tq, tq), tq)

    def step(blocks):
        scores = [scores_of(k) for k, _ in blocks]
        for (k, masked), s in zip(blocks, scores):
            consume(s, k, masked)

    def pair(j, carry):
        step([(keys(2 * j), False), (keys(2 * j + 1), False)])
        return carry

    lax.fori_loop(0, i >> 1, pair, 0)

    @pl.when((i & 1) == 1)
    def _():
        step([(keys(i - 1), False), (keys(i), True)])

    @pl.when((i & 1) == 0)
    def _():
        step([(keys(i), True)])


BIAS_LANES = 8


def _split3(c):
    hi = c.astype(BF16)
    rest = c - hi.astype(F32)
    mid = rest.astype(BF16)
    lo = (rest - mid.astype(F32)).astype(BF16)
    return jnp.concatenate([hi, mid, lo], axis=1)


def _bias_select(pair, key_side):
    r = lax.broadcasted_iota(jnp.int32, (3 * LANES, 1), 0)
    d = lax.broadcasted_iota(jnp.int32, (1, LANES), 1)
    piece, src = r >> 7, r & (LANES - 1)
    shift = 3 if key_side else 0
    first = jnp.logical_and(src == 2 * pair, d == piece + shift)
    second = jnp.logical_and(src == 2 * pair + 1, d == piece + shift + BIAS_LANES)
    value = -1.0 if key_side else 1.0
    return jnp.where(jnp.logical_or(first, second), value, 0.0).astype(BF16)


def _bias_ones(key_side):
    d = lax.broadcasted_iota(jnp.int32, (1, LANES), 1)
    within = d & (BIAS_LANES - 1)
    hit = (within < 3) if key_side else jnp.logical_and(within >= 3, within < 6)
    return jnp.where(jnp.logical_and(hit, d < 2 * BIAS_LANES), 1.0, 0.0)


def _fox_kernel(q_ref, k_ref, vt_ref, c_ref, o_ref, m_ref, acc_ref, kb_ref):
    i = pl.program_id(1)
    tq = q_ref.shape[0]
    hd = FOX_HEAD_DIM
    npair = FOX_HEADS // 2
    lane = lax.broadcasted_iota(jnp.int32, (tq, LANES), 1)
    lo = lane < hd
    lo_bias = lane < BIAS_LANES
    _init_softmax_state(m_ref, acc_ref)

    @pl.when(i == 0)
    def _():
        pieces = _split3(c_ref[...])
        for hp in range(npair):
            kb_ref[hp] = (_dot(pieces, _bias_select(hp, True)) + _bias_ones(True)).astype(BF16)

    q_pieces = _split3(c_ref[pl.ds(pl.multiple_of(i * tq, tq), tq), :])
    q_ops = []
    for hp in range(npair):
        q = q_ref[:, hp * LANES:(hp + 1) * LANES]
        zero = jnp.zeros_like(q)
        qb = (_dot(q_pieces, _bias_select(hp, False)) + _bias_ones(False)).astype(BF16)
        q_ops.append(jnp.concatenate([jnp.where(lo, q, zero), jnp.where(lo_bias, qb, zero)], axis=1))
        q_ops.append(jnp.concatenate([jnp.where(lo, zero, q), jnp.where(lo_bias, zero, qb)], axis=1))

    def scores_of(keys):
        out = []
        for hp in range(npair):
            k_op = jnp.concatenate([k_ref[keys, hp * LANES:(hp + 1) * LANES], kb_ref[hp, keys, :]], axis=1)
            out.append(_dot_nt(k_op, q_ops[2 * hp]))
            out.append(_dot_nt(k_op, q_ops[2 * hp + 1]))
        return out

    def consume(scores, keys, masked):
        probs = []
        for h in range(FOX_HEADS):
            s = scores[h]
            if masked:
                s = jnp.where(_causal_mask_t(tq), s, NEG_INF)
            probs.append(_online_softmax(s, m_ref, h))
        for h in range(FOX_HEADS):
            p, alpha = probs[h]
            pv = _dot(_with_ones(vt_ref[h * hd:(h + 1) * hd, keys]), p)
            acc_ref[h] = acc_ref[h] * alpha + pv

    _causal_sweep(i, tq, scores_of, consume)
    for hp in range(npair):
        heads = [acc_ref[h, :hd, :] / acc_ref[h, hd:hd + 1, :] for h in (2 * hp, 2 * hp + 1)]
        o_ref[:, hp * LANES:(hp + 1) * LANES] = jnp.concatenate(heads, axis=0).T.astype(BF16)


def _fox(proj, vt, c):
    nq = SEQ // TQ
    return pl.pallas_call(
        _fox_kernel,
        grid=(BATCH, nq),
        in_specs=[
            pl.BlockSpec((None, TQ, 512), lambda b, i: (SEG_FQ, b * nq + i, 0)),
            pl.BlockSpec((None, SEQ, 512), lambda b, i: (SEG_FK, b, 0)),
            pl.BlockSpec((None, 512, SEQ), lambda b, i: (VT_FOX, 0, b)),
            pl.BlockSpec((SEQ, LANES), lambda b, i: (b, 0)),
        ],
        out_specs=pl.BlockSpec((TQ, 512), lambda b, i: (b * nq + i, 0)),
        out_shape=jax.ShapeDtypeStruct((TOKENS, 512), BF16),
        scratch_shapes=[pltpu.VMEM((FOX_HEADS, 1, TQ), F32),
                        pltpu.VMEM((FOX_HEADS, FOX_HEAD_DIM + ONES_ROWS, TQ), F32),
                        pltpu.VMEM((FOX_HEADS // 2, SEQ, LANES), BF16)],
        compiler_params=_params("parallel", "arbitrary"),
        name="fox_attn",
    )(proj, proj, vt, c)


def _diff_kernel(lam_init, q_ref, k_ref, vt_ref, lam_ref, g_ref, o_ref, m_ref, acc_ref):
    i = pl.program_id(1)
    tq = q_ref.shape[0]
    dv = LANES
    lane = lax.broadcasted_iota(jnp.int32, (tq, LANES), 1)
    lo = lane < DIFF_HEAD_DIM
    _init_softmax_state(m_ref, acc_ref)

    def scores_of(keys):
        out = []
        for h in range(DIFF_HEADS):
            cols = slice(h * LANES, (h + 1) * LANES)
            q = q_ref[:, cols]
            zero = jnp.zeros_like(q)
            kb = k_ref[keys, cols]
            out.append(_dot_nt(kb, jnp.where(lo, q, zero)))
            out.append(_dot_nt(kb, jnp.where(lo, zero, q)))
        return out

    def consume(scores, keys, masked):
        probs = []
        for idx in range(2 * DIFF_HEADS):
            s = scores[idx]
            if masked:
                s = jnp.where(_causal_mask_t(tq), s, NEG_INF)
            probs.append(_online_softmax(s, m_ref, idx))
        for h in range(DIFF_HEADS):
            vtb = _with_ones(vt_ref[h * dv:(h + 1) * dv, keys])
            for idx in (2 * h, 2 * h + 1):
                p, alpha = probs[idx]
                acc_ref[idx] = acc_ref[idx] * alpha + _dot(vtb, p)

    _causal_sweep(i, tq, scores_of, consume)
    lam_rows = lam_ref[...]
    lam = (jnp.exp(jnp.sum(lam_rows[0:1] * lam_rows[1:2], axis=1, keepdims=True))
           - jnp.exp(jnp.sum(lam_rows[2:3] * lam_rows[3:4], axis=1, keepdims=True)) + lam_init)
    for h in range(DIFF_HEADS):
        cols = slice(h * LANES, (h + 1) * LANES)
        o1 = acc_ref[2 * h, :dv, :] / acc_ref[2 * h, dv:dv + 1, :]
        o2 = acc_ref[2 * h + 1, :dv, :] / acc_ref[2 * h + 1, dv:dv + 1, :]
        o = o1 - lam * o2
        o = o * lax.rsqrt(jnp.mean(o * o, axis=0, keepdims=True) + RMS_EPS)
        o_ref[:, cols] = (o.T * g_ref[:, cols] * (1.0 - lam_init)).astype(BF16)


def _diff(proj, vt, lam, g, lam_init):
    nq = SEQ // TQ
    return pl.pallas_call(
        functools.partial(_diff_kernel, lam_init),
        grid=(BATCH, nq),
        in_specs=[
            pl.BlockSpec((None, TQ, 512), lambda b, i: (SEG_DQ, b * nq + i, 0)),
            pl.BlockSpec((None, SEQ, 512), lambda b, i: (SEG_DK, b, 0)),
            pl.BlockSpec((None, 512, SEQ), lambda b, i: (VT_DIFF, 0, b)),
            pl.BlockSpec((4, DIFF_HEAD_DIM), lambda b, i: (0, 0)),
            pl.BlockSpec((1, 512), lambda b, i: (0, 0)),
        ],
        out_specs=pl.BlockSpec((TQ, 512), lambda b, i: (b * nq + i, 0)),
        out_shape=jax.ShapeDtypeStruct((TOKENS, 512), BF16),
        scratch_shapes=[pltpu.VMEM((2 * DIFF_HEADS, 1, TQ), F32),
                        pltpu.VMEM((2 * DIFF_HEADS, LANES + ONES_ROWS, TQ), F32)],
        compiler_params=_params("parallel", "arbitrary"),
        name="diff_attn",
    )(proj, proj, vt, lam, g)


def _spread(compact, select, row_group, col_group):
    full = _dot(compact, select)
    return jnp.where(row_group == col_group, full, 0.0).astype(BF16)


def _s5_kernel(u_ref, kc_ref, abre_ref, abim_ref, acre_ref, acim_ref, are_ref, aim_ref, y_ref, *state):
    nslab = SSM_SCOLS // LANES
    sre, sim = state[:nslab], state[nslab:]

    def iota(shape, dim):
        return lax.broadcasted_iota(jnp.int32, shape, dim)

    i_tn, j_cols = iota((SSM_TN, 1), 0), iota((1, SSM_COLS), 1)
    sel_tn = jnp.where(jnp.logical_and((i_tn >> 4) == (j_cols >> 7), (i_tn & 15) == (j_cols & 15)),
                       1.0, 0.0).astype(BF16)
    i_p, j_sc = iota((SSM_STATE, 1), 0), iota((1, SSM_SCOLS), 1)
    sel_p = jnp.where(i_p == (j_sc & (SSM_STATE - 1)), 1.0, 0.0).astype(BF16)
    grp_rows_cols = (iota((SSM_COLS, 1), 0) >> 4) & (SSM_GBLK - 1)
    grp_cols_cols = (j_cols >> 4) & (SSM_GBLK - 1)
    grp_rows_state = iota((SSM_SCOLS, 1), 0) >> 6
    grp_cols_state = j_sc >> 6
    kc = kc_ref[...]
    lane_tn = iota(kc.shape, 1)
    am = jnp.concatenate(
        [kc] + [jnp.where(lane_tn >= s * SSM_GROUP, pltpu.roll(kc, s * SSM_GROUP, 1), 0.0)
                for s in range(1, SSM_CHUNK)], axis=0).astype(BF16)
    m = _spread(am, sel_tn, grp_rows_cols, grp_cols_cols)
    bre = _spread(abre_ref[...], sel_p, grp_rows_cols, grp_cols_state)
    bim = _spread(abim_ref[...], sel_p, grp_rows_cols, grp_cols_state)
    cre = _spread(acre_ref[...], sel_tn, grp_rows_state, grp_cols_cols)
    cim = _spread(acim_ref[...], sel_tn, grp_rows_state, grp_cols_cols)

    u = jnp.concatenate([u_ref[s] for s in range(SSM_CHUNK)], axis=1)
    for k in range(0, nslab, 2):
        cols = slice(k * LANES, (k + 2) * LANES)
        for dst, b in ((sre, bre), (sim, bim)):
            part = _dot(u, b[:, cols])
            dst[k][...] = part[:, :LANES]
            dst[k + 1][...] = part[:, LANES:]
    ar = [are_ref[:, k * LANES:(k + 1) * LANES] for k in range(nslab)]
    ai = [aim_ref[:, k * LANES:(k + 1) * LANES] for k in range(nslab)]

    def body(c, carry):
        rows = pl.ds(c, BATCH, stride=SSM_NCHUNK)
        nxt = []
        for k in range(nslab):
            xr, xi = carry[k]
            sr = sre[k][rows, :]
            si = sim[k][rows, :]
            sre[k][rows, :] = xr
            sim[k][rows, :] = xi
            nxt.append((ar[k] * xr - ai[k] * xi + sr, ar[k] * xi + ai[k] * xr + si))
        return tuple(nxt)

    zero = jnp.zeros((BATCH, LANES), F32)
    lax.fori_loop(0, SSM_NCHUNK, body, tuple((zero, zero) for _ in range(nslab)))
    xr = jnp.concatenate([r[...] for r in sre], axis=1).astype(BF16)
    xi = jnp.concatenate([r[...] for r in sim], axis=1).astype(BF16)
    for c in range(SSM_COLS // (2 * LANES)):
        cols = slice(c * 2 * LANES, (c + 1) * 2 * LANES)
        y = _dot(u, m[:, cols]) + _dot(xr, cre[:, cols]) + _dot(xi, cim[:, cols])
        y_ref[2 * c] = y[:, :LANES].astype(BF16)
        y_ref[2 * c + 1] = y[:, LANES:].astype(BF16)


def _s5_scan(u_stack, mats, layer):
    am, abre, abim, acre, acim, are, aim = mats

    def spec(*shape):
        return pl.BlockSpec((None, None) + shape, lambda g: (layer, g) + (0,) * len(shape))

    io_spec = pl.BlockSpec((SSM_CHUNK, SSM_ROWS, LANES), lambda g: (0, 0, g))
    return pl.pallas_call(
        _s5_kernel,
        grid=(SSM_NGBLK,),
        in_specs=[io_spec, spec(LANES, SSM_TN),
                  spec(SSM_COLS, SSM_STATE), spec(SSM_COLS, SSM_STATE),
                  spec(SSM_SCOLS, SSM_TN), spec(SSM_SCOLS, SSM_TN),
                  spec(1, SSM_SCOLS), spec(1, SSM_SCOLS)],
        out_specs=io_spec,
        out_shape=jax.ShapeDtypeStruct((SSM_CHUNK, SSM_ROWS, SSM_WIDTH), BF16),
        scratch_shapes=[pltpu.VMEM((SSM_ROWS, LANES), F32) for _ in range(2 * SSM_SCOLS // LANES)],
        compiler_params=_params("parallel"),
        name="s5_scan",
    )(u_stack, am, abre, abim, acre, acim, are, aim)


def _s5_matrices(lam_re, lam_im, log_dt, b_re, b_im, c_re, c_im):
    hi = lax.Precision.HIGHEST
    L, NB, GB = SSM_CHUNK, SSM_NGBLK, SSM_GBLK
    lr = lam_re.astype(F32)
    li = lam_im.astype(F32)
    dt = jnp.exp(log_dt.astype(F32))[:, None]
    mag = jnp.exp(lr * dt)
    ang = li * dt
    ar = mag * jnp.cos(ang)
    ai = mag * jnp.sin(ang)
    er = ar - 1.0
    ei = ai
    den = lr * lr + li * li
    qr = (er * lr + ei * li) / den
    qi = (ei * lr - er * li) / den
    br = b_re.astype(F32)
    bi = b_im.astype(F32)
    bbr = qr[:, :, None] * br - qi[:, :, None] * bi
    bbi = qr[:, :, None] * bi + qi[:, :, None] * br
    k = jnp.arange(L + 1, dtype=F32)[:, None, None]
    pmag = jnp.exp(lr[None] * dt[None] * k)
    pr = pmag * jnp.cos(ang[None] * k)
    pi = pmag * jnp.sin(ang[None] * k)
    cr = c_re.astype(F32)
    ci = c_im.astype(F32)
    e_r = cr[None] * pr[:, :, None, :] - ci[None] * pi[:, :, None, :]
    e_i = cr[None] * pi[:, :, None, :] + ci[None] * pr[:, :, None, :]
    kern = jnp.einsum('kgnq,gqm->kgnm', jnp.concatenate([e_r[:L], -e_i[:L]], axis=-1),
                      jnp.concatenate([bbr, bbi], axis=1), precision=hi)
    kc = jnp.transpose(kern.reshape(L, NB, GB, SSM_GROUP, SSM_GROUP), (1, 2, 4, 0, 3))
    kc = kc.reshape(NB, LANES, SSM_TN)
    krev = (L - 1) - jnp.arange(L, dtype=F32)[:, None, None]
    rmag = jnp.exp(lr[None] * dt[None] * krev)
    rev_r = rmag * jnp.cos(ang[None] * krev)
    rev_i = rmag * jnp.sin(ang[None] * krev)
    sb_r = rev_r[:, :, :, None] * bbr[None] - rev_i[:, :, :, None] * bbi[None]
    sb_i = rev_r[:, :, :, None] * bbi[None] + rev_i[:, :, :, None] * bbr[None]

    def state_in(t):
        t = t.reshape(L, NB, GB, SSM_STATE, SSM_GROUP)
        return jnp.transpose(t, (1, 0, 2, 4, 3)).reshape(NB, SSM_COLS, SSM_STATE)

    def state_out(t):
        t = t.reshape(L, NB, GB, SSM_GROUP, SSM_STATE)
        return jnp.transpose(t, (1, 2, 4, 0, 3)).reshape(NB, SSM_SCOLS, SSM_TN)

    are = pr[L].reshape(NB, 1, SSM_SCOLS)
    aim = pi[L].reshape(NB, 1, SSM_SCOLS)
    return (kc, state_in(sb_r).astype(BF16), state_in(sb_i).astype(BF16),
            state_out(e_r[1:]).astype(BF16), (-state_out(e_i[1:])).astype(BF16), are, aim)


def _s5_post_kernel(y_ref, u_ref, d_ref, w_ref, o_ref, tmp_ref):
    nrow = y_ref.shape[1]
    ycat = jnp.concatenate([y_ref[s] for s in range(SSM_CHUNK)], axis=0).astype(F32)
    ucat = jnp.concatenate([u_ref[s] for s in range(SSM_CHUNK)], axis=0).astype(F32)
    y = jax.nn.gelu(ycat + d_ref[...] * ucat, approximate=True)
    z = _dot(y.astype(BF16), w_ref[...])
    out = z[:, :SSM_WIDTH] * jax.nn.sigmoid(z[:, SSM_WIDTH:])
    for s in range(SSM_CHUNK):
        for c in range(SSM_WIDTH // LANES):
            tmp_ref[c, pl.ds(s, nrow, stride=SSM_CHUNK), :] = (
                out[s * nrow:(s + 1) * nrow, c * LANES:(c + 1) * LANES])
    for c in range(SSM_WIDTH // LANES):
        o_ref[:, c * LANES:(c + 1) * LANES] = tmp_ref[c].astype(BF16)


def _s5_post(y_stack, u_stack, d, w_glu, layer):
    tm = TM_POST
    stack_spec = pl.BlockSpec((SSM_CHUNK, tm // SSM_CHUNK, SSM_WIDTH), lambda i: (0, i, 0))
    return pl.pallas_call(
        _s5_post_kernel,
        grid=(TOKENS // tm,),
        in_specs=[
            stack_spec,
            stack_spec,
            pl.BlockSpec((1, SSM_WIDTH), lambda i: (0, 0)),
            pl.BlockSpec((None, SSM_WIDTH, 2 * SSM_WIDTH), lambda i: (layer, 0, 0)),
        ],
        out_specs=pl.BlockSpec((tm, SSM_WIDTH), lambda i: (i, 0)),
        out_shape=jax.ShapeDtypeStruct((TOKENS, SSM_WIDTH), BF16),
        scratch_shapes=[pltpu.VMEM((SSM_WIDTH // LANES, tm, LANES), F32)],
        compiler_params=_params("parallel"),
        name="s5_post",
    )(y_stack, u_stack, d, w_glu)


def _route(x, w_ref, b_ref):
    xh = x.astype(BF16)
    xl = (x - xh.astype(F32)).astype(BF16)
    hw = _dot(xh, w_ref[...])
    logits = hw[:, :LANES] + (hw[:, LANES:] + _dot(xl, w_ref[:, :LANES])) + b_ref[...]
    lane = lax.broadcasted_iota(jnp.int32, logits.shape, 1).astype(F32)
    big = jnp.float32(LANES)
    neg = jnp.float32(-jnp.inf)

    def first_argmax(v, vmax):
        return jnp.min(jnp.where(v == vmax, lane, big), axis=1, keepdims=True)

    gl = jnp.where(lane < N_EXPERT_GROUPS, logits, neg)
    ge = jnp.exp(gl - jnp.max(gl, axis=1, keepdims=True))
    pg = ge / jnp.sum(ge, axis=1, keepdims=True)
    gp = jnp.max(pg, axis=1, keepdims=True)
    gi = first_argmax(pg, gp)
    base = N_EXPERT_GROUPS + EXPERTS_PER_GROUP * gi
    sel = jnp.logical_and(lane >= base, lane < base + EXPERTS_PER_GROUP)
    el = jnp.where(sel, logits, neg)
    ev1 = jnp.max(el, axis=1, keepdims=True)
    i1 = first_argmax(el, ev1)
    el2 = jnp.where(lane == i1, neg, el)
    ev2 = jnp.max(el2, axis=1, keepdims=True)
    i2 = first_argmax(el2, ev2)
    e2 = jnp.exp(ev2 - ev1)
    den = 1.0 + e2
    w1 = gp * (1.0 / den)
    w2 = gp * (e2 / den)
    return (jnp.where(lane == i1, w1, 0.0) + jnp.where(lane == i2, w2, 0.0)
            + jnp.where(lane == 0.0, gi, 0.0))


def _router_kernel(x_ref, w_ref, b_ref, o_ref):
    o_ref[...] = _route(x_ref[...], w_ref, b_ref)


def _router(x, w_r, b_r, layer):
    tm = TM_ROUTE
    return pl.pallas_call(
        _router_kernel,
        grid=(TOKENS // tm,),
        in_specs=[
            pl.BlockSpec((tm, D_MODEL), lambda i: (i, 0)),
            pl.BlockSpec((None, D_MODEL, 2 * LANES), lambda i: (layer, 0, 0)),
            pl.BlockSpec((None, 1, LANES), lambda i: (layer, 0, 0)),
        ],
        out_specs=pl.BlockSpec((tm, LANES), lambda i: (i, 0)),
        out_shape=jax.ShapeDtypeStruct((TOKENS, LANES), F32),
        compiler_params=_params("parallel"),
        name="router",
    )(x, w_r, b_r)


def _merge_kernel(x_ref, ys_ref, yd_ref, yf_ref, wgl_ref, wbr_ref, wout_ref, g_ref, b_ref, o_ref):
    x = x_ref[...]
    xb = x.astype(BF16)
    merged = None
    for n, y_ref in enumerate((ys_ref, yd_ref, yf_ref)):
        gate = jax.nn.sigmoid(_dot(xb, wgl_ref[:, n * D_MODEL:(n + 1) * D_MODEL]))
        term = gate * _dot(y_ref[...], wbr_ref[n])
        merged = term if merged is None else merged + term
    mix = _dot(merged.astype(BF16), wout_ref[...])
    o_ref[...] = _layer_norm(ALPHA * x + mix, g_ref[...], b_ref[...])


def _merge(x, ys, yd, yf, wgl, wbr, wout, g, b, layer):
    tm = TM_MERGE
    row = lambda i: (i, 0)
    const2 = lambda i: (0, 0)
    per_layer = lambda i: (layer, 0, 0)
    return pl.pallas_call(
        _merge_kernel,
        grid=(TOKENS // tm,),
        in_specs=[
            pl.BlockSpec((tm, D_MODEL), row),
            pl.BlockSpec((tm, 512), row),
            pl.BlockSpec((tm, 512), row),
            pl.BlockSpec((tm, 512), row),
            pl.BlockSpec((None, D_MODEL, N_BRANCH * D_MODEL), per_layer),
            pl.BlockSpec((None, N_BRANCH, BRANCH_WIDTH, D_MODEL), lambda i: (layer, 0, 0, 0)),
            pl.BlockSpec((None, D_MODEL, D_MODEL), per_layer),
            pl.BlockSpec((1, D_MODEL), const2),
            pl.BlockSpec((1, D_MODEL), const2),
        ],
        out_specs=pl.BlockSpec((tm, D_MODEL), row),
        out_shape=jax.ShapeDtypeStruct((TOKENS, D_MODEL), F32),
        compiler_params=_params("parallel"),
        name="merge_ln",
    )(x, ys, yd, yf, wgl, wbr, wout, g, b)


MOE_SUB = 128


def _one_hot(cond):
    return jnp.where(cond, 1.0, 0.0).astype(BF16)


def _moe_sort(x_ref, gate_ref, xs_ref, gs_ref, ys_ref, pos_ref, off_ref):
    tm = x_ref.shape[0]
    gates = gate_ref[...]
    lane = lax.broadcasted_iota(jnp.int32, (1, LANES), 1)
    row8 = lax.broadcasted_iota(jnp.int32, (8, 1), 0)
    t_col = lax.broadcasted_iota(jnp.int32, (tm, 1), 0)
    t_row = lax.broadcasted_iota(jnp.int32, (1, tm), 1)
    member = jnp.where(gates[:, 0:1] == lane.astype(F32), 1.0, 0.0)
    member_b = member.astype(BF16)
    counts = jnp.sum(member, axis=0, keepdims=True)
    cnt = [jnp.sum(jnp.where(lane == k, counts, 0.0), axis=1, keepdims=True)
           for k in range(N_EXPERT_GROUPS - 1)]
    start = [cnt[0], cnt[0] + cnt[1], cnt[0] + cnt[1] + cnt[2]]
    off_ref[0] = 0
    for k in range(N_EXPERT_GROUPS - 1):
        off_ref[k + 1] = start[k][0, 0].astype(jnp.int32)
    off_ref[N_EXPERT_GROUPS] = tm

    def offsets(index):
        return sum(jnp.where(index > k, cnt[k], 0.0) for k in range(N_EXPERT_GROUPS - 1))

    earlier = _one_hot(t_row < t_col)
    rank_col = _dot(earlier, member_b)
    pos_col = jnp.sum(member * (offsets(lane) + rank_col), axis=1, keepdims=True)
    pos_ref[...] = jnp.broadcast_to(pos_col, (tm, LANES))
    pick = _one_hot(lax.broadcasted_iota(jnp.int32, (8, LANES), 0)
                    == lax.broadcasted_iota(jnp.int32, (8, LANES), 1))
    member_t = _dot_nt(pick, member_b)
    rank_row = _dot(member_t.astype(BF16), _one_hot(t_col < t_row))
    pos_row = jnp.sum(member_t * (offsets(row8) + rank_row), axis=0, keepdims=True)
    perm = _one_hot(t_col.astype(F32) == pos_row)
    xs_ref[...] = _dot(perm, x_ref[...].astype(BF16)).astype(BF16)
    g_hi = gates.astype(BF16)
    rest = gates - g_hi.astype(F32)
    g_mid = rest.astype(BF16)
    g_lo = (rest - g_mid.astype(F32)).astype(BF16)
    moved = _dot(perm, jnp.concatenate([g_hi, g_mid, g_lo], axis=1))
    gs_ref[...] = moved[:, :LANES] + moved[:, LANES:2 * LANES] + moved[:, 2 * LANES:]
    ys_ref[...] = jnp.zeros(ys_ref.shape, F32)


def _moe_kernel(x_ref, gate_ref, wg_ref, wu_ref, wd_ref, g_ref, b_ref, o_ref,
                xs_ref, gs_ref, ys_ref, pos_ref, off_ref):
    grp = pl.program_id(1)
    tm = x_ref.shape[0]

    @pl.when(grp == 0)
    def _():
        _moe_sort(x_ref, gate_ref, xs_ref, gs_ref, ys_ref, pos_ref, off_ref)

    first = N_EXPERT_GROUPS + EXPERTS_PER_GROUP * grp
    shift = MOE_SUB.bit_length() - 1

    def sub_tile(k, carry):
        rows = pl.ds(pl.multiple_of(k * MOE_SUB, MOE_SUB), MOE_SUB)
        xb = xs_ref[rows, :]
        gates = gs_ref[rows, :]
        lane = lax.broadcasted_iota(jnp.int32, gates.shape, 1)
        hidden = []
        for e in range(EXPERTS_PER_GROUP):
            gate = jnp.sum(jnp.where(lane == first + e, gates, 0.0), axis=1, keepdims=True)
            he = jax.nn.silu(_dot(xb, wg_ref[e])) * _dot(xb, wu_ref[e]) * gate
            hidden.append(he.astype(BF16))
        ys_ref[rows, :] += _dot(jnp.concatenate(hidden, axis=1), wd_ref[...])
        return carry

    lo = off_ref[grp] >> shift
    hi = (off_ref[grp + 1] + (MOE_SUB - 1)) >> shift
    lax.fori_loop(lo, hi, sub_tile, 0)

    @pl.when(grp == N_EXPERT_GROUPS - 1)
    def _():
        slot = lax.broadcasted_iota(jnp.int32, (1, tm), 1).astype(F32)
        unsort = _one_hot(pos_ref[:, 0:1] == slot)
        y = _dot(unsort, ys_ref[...].astype(BF16))
        o_ref[...] = _layer_norm(ALPHA * x_ref[...] + y, g_ref[...], b_ref[...])


def _moe(x, gates, wg, wu, wd_grouped, g, b, layer):
    tm = TM_MOE
    return pl.pallas_call(
        _moe_kernel,
        grid=(TOKENS // tm, N_EXPERT_GROUPS),
        in_specs=[
            pl.BlockSpec((tm, D_MODEL), lambda i, e: (i, 0)),
            pl.BlockSpec((tm, LANES), lambda i, e: (i, 0)),
            pl.BlockSpec((None, EXPERTS_PER_GROUP, D_MODEL, D_EXPERT), lambda i, e: (layer, e, 0, 0)),
            pl.BlockSpec((None, EXPERTS_PER_GROUP, D_MODEL, D_EXPERT), lambda i, e: (layer, e, 0, 0)),
            pl.BlockSpec((None, None, EXPERTS_PER_GROUP * D_EXPERT, D_MODEL), lambda i, e: (layer, e, 0, 0)),
            pl.BlockSpec((1, D_MODEL), lambda i, e: (0, 0)),
            pl.BlockSpec((1, D_MODEL), lambda i, e: (0, 0)),
        ],
        out_specs=pl.BlockSpec((tm, D_MODEL), lambda i, e: (i, 0)),
        out_shape=jax.ShapeDtypeStruct((TOKENS, D_MODEL), F32),
        scratch_shapes=[pltpu.VMEM((tm, D_MODEL), BF16),
                        pltpu.VMEM((tm, LANES), F32),
                        pltpu.VMEM((tm, D_MODEL), F32),
                        pltpu.VMEM((tm, LANES), F32),
                        pltpu.SMEM((8,), jnp.int32)],
        compiler_params=_params("parallel", "arbitrary"),
        name="moe_ln",
    )(x, gates, wg, wu, wd_grouped, g, b)


def _rope_tables():
    pos = jnp.arange(SEQ, dtype=F32)
    inv_freq = ROPE_THETA ** (-jnp.arange(0, DIFF_HEAD_DIM, 2, dtype=F32) / DIFF_HEAD_DIM)
    ang = pos[:, None] * inv_freq[None, :]
    emb = jnp.concatenate([ang, ang], axis=-1)
    cos = jnp.cos(emb)
    sin = jnp.sin(emb)
    sign = jnp.where(jnp.arange(DIFF_HEAD_DIM) < DIFF_HEAD_DIM // 2, -1.0, 1.0).astype(F32)
    return jnp.tile(cos, (1, 2)), jnp.tile(sin * sign, (1, 2))


def _split_w_in(w_in):
    qscale = DIFF_HEAD_DIM ** -0.5 * LOG2E

    def seg(k):
        return w_in[:, :, k * 512:(k + 1) * 512]

    w_tok = jnp.stack([seg(0), seg(1) * qscale, seg(2), seg(4) * qscale, seg(5)], axis=1).astype(BF16)
    w_vt = jnp.stack([jnp.swapaxes(seg(3), 1, 2), jnp.swapaxes(seg(6), 1, 2)], axis=1).astype(BF16)
    off = 7 * 512
    wff_t = jnp.swapaxes(w_in[:, :, off:off + FOX_HEADS], 1, 2).astype(BF16)
    wgl = w_in[:, :, off + FOX_HEADS:].astype(BF16)
    return w_tok, w_vt, wff_t, wgl


def kernel(x, w_in, w_branch, w_out, ssm_lambda_re, ssm_lambda_im, ssm_log_dt, ssm_b_re, ssm_b_im,
           ssm_c_re, ssm_c_im, ssm_d, ssm_w_glu, diff_lambda, diff_norm_g, fox_f_bias, ln1_g, ln1_b,
           moe_w_group, moe_b_group, moe_w_expert, moe_b_expert, moe_w_gate, moe_w_up, moe_w_down,
           ln2_g, ln2_b):
    cos, sin = _rope_tables()
    w_tok, w_vt, wff_t, wgl = _split_w_in(w_in)
    s5_mats = jax.vmap(_s5_matrices)(ssm_lambda_re, ssm_lambda_im, ssm_log_dt, ssm_b_re, ssm_b_im,
                                     ssm_c_re, ssm_c_im)
    w_glu = ssm_w_glu.astype(BF16)
    w_br = w_branch.astype(BF16)
    w_o = w_out.astype(BF16)
    w_gate, w_up = moe_w_gate.astype(BF16), moe_w_up.astype(BF16)
    w_down = moe_w_down.astype(BF16).reshape(DEPTH, N_EXPERT_GROUPS, EXPERTS_PER_GROUP * D_EXPERT, D_MODEL)
    w_r = jnp.concatenate([moe_w_group, moe_w_expert], axis=2).astype(F32)
    w_r = jnp.pad(w_r, ((0, 0), (0, 0), (0, LANES - w_r.shape[2])))
    w_r_hi = w_r.astype(BF16)
    w_r = jnp.concatenate([w_r_hi, (w_r - w_r_hi.astype(F32)).astype(BF16)], axis=2)
    b_r = jnp.concatenate([moe_b_group, moe_b_expert], axis=1).astype(F32)
    b_r = jnp.pad(b_r, ((0, 0), (0, LANES - b_r.shape[1]))).reshape(DEPTH, 1, LANES)

    h = x.reshape(TOKENS, D_MODEL)
    for l in range(DEPTH):
        lam_init = 0.8 - 0.6 * math.exp(-0.3 * l)
        proj, u_stack, vt, fft = _inproj(h, w_tok, w_vt, wff_t, cos, sin, l)

        y_stack = _s5_scan(u_stack, s5_mats, l)
        y_ssm = _s5_post(y_stack, u_stack, ssm_d[l].reshape(1, SSM_WIDTH).astype(F32), w_glu, l)

        y_diff = _diff(proj, vt, diff_lambda[l].astype(F32),
                       diff_norm_g[l].reshape(1, 512).astype(F32), lam_init)
        c = _fgate(fft, fox_f_bias[l].reshape(FOX_HEADS, 1).astype(F32))
        y_fox = _fox(proj, vt, c)

        h = _merge(h, y_ssm, y_diff, y_fox, wgl, w_br, w_o,
                   ln1_g[l].reshape(1, D_MODEL), ln1_b[l].reshape(1, D_MODEL), l)

        gates = _router(h, w_r, b_r, l)
        h = _moe(h, gates, w_gate, w_up, w_down,
                 ln2_g[l].reshape(1, D_MODEL), ln2_b[l].reshape(1, D_MODEL), l)
    return h.reshape(BATCH, SEQ, D_MODEL)
```

```python
import functools
import math

import jax
import jax.numpy as jnp
from jax import lax
from jax.experimental import pallas as pl
from jax.experimental.pallas import tpu as pltpu

F32 = jnp.float32
BF16 = jnp.bfloat16

D_MODEL = 1024
BATCH = 8
SEQ = 2048
DEPTH = 4
TOKENS = BATCH * SEQ

SSM_WIDTH = 512
SSM_GROUP = 16
SSM_GROUPS = 32
SSM_STATE = 64
SSM_CHUNK = 8
SSM_NCHUNK = SEQ // SSM_CHUNK
SSM_ROWS = BATCH * SSM_NCHUNK
SSM_GBLK = 8
SSM_NGBLK = SSM_GROUPS // SSM_GBLK
SSM_COLS = SSM_CHUNK * 128
SSM_SCOLS = SSM_GBLK * SSM_STATE
SSM_TN = SSM_CHUNK * SSM_GROUP

DIFF_HEADS = 4
DIFF_HEAD_DIM = 64
FOX_HEADS = 8
FOX_HEAD_DIM = 64
BRANCH_WIDTH = 512
N_BRANCH = 3
ROPE_THETA = 10000.0

N_EXPERT_GROUPS = 4
EXPERTS_PER_GROUP = 4
N_EXPERTS = 16
D_EXPERT = 256

ALPHA = (2 * DEPTH) ** 0.25
LN_EPS = 1e-5
RMS_EPS = 1e-6
NEG_INF = -1e30
LOG2E = math.log2(math.e)

LANES = 128
N_SEG = 4
SEG_DQ, SEG_DK, SEG_FQ, SEG_FK = range(N_SEG)
VT_DIFF, VT_FOX = 0, 1
VMEM_LIMIT = 56 * 1024 * 1024

TM_PROJ = 512
TQ = 256
TM_POST = 512
TM_MERGE = 512
TM_ROUTE = 512
TM_MOE = 1024


def _params(*sem):
    return pltpu.CompilerParams(dimension_semantics=sem, vmem_limit_bytes=VMEM_LIMIT)


def _dot(a, b):
    return jnp.dot(a, b, preferred_element_type=F32)


def _dot_nt(a, b):
    return lax.dot_general(a, b, (((1,), (1,)), ((), ())), preferred_element_type=F32)


def _layer_norm(y, g, b):
    mu = jnp.mean(y, axis=-1, keepdims=True)
    d = y - mu
    var = jnp.mean(d * d, axis=-1, keepdims=True)
    return d * lax.rsqrt(var + LN_EPS) * g + b


def _inproj_kernel(x_ref, w_ref, wvt_ref, wff_ref, cos_ref, sin_ref, o_ref, u_ref, vt_ref, fft_ref,
                   tmp_ref):
    tm = x_ref.shape[0]
    xb = x_ref[...].astype(BF16)
    cos = cos_ref[...]
    sin = sin_ref[...]
    lane = lax.broadcasted_iota(jnp.int32, cos.shape, 1)
    first_half = (lane & 32) == 0
    for seg in range(N_SEG):
        acc = _dot(xb, w_ref[seg + 1])
        if seg in (SEG_DQ, SEG_DK):
            for c in range(512 // LANES):
                t = acc[:, c * LANES:(c + 1) * LANES]
                rot = jnp.where(first_half, pltpu.roll(t, LANES - 32, 1), pltpu.roll(t, 32, 1))
                o_ref[seg, :, c * LANES:(c + 1) * LANES] = (t * cos + rot * sin).astype(BF16)
        else:
            o_ref[seg] = acc.astype(BF16)
    for v in range(2):
        vt_ref[v] = _dot_nt(wvt_ref[v], xb).astype(BF16)
    fft_ref[...] = _dot_nt(wff_ref[...], xb)
    u = _dot(xb, w_ref[0])
    for c in range(SSM_WIDTH // LANES):
        tmp_ref[c] = u[:, c * LANES:(c + 1) * LANES]
    for s in range(SSM_CHUNK):
        for c in range(SSM_WIDTH // LANES):
            u_ref[s, :, c * LANES:(c + 1) * LANES] = (
                tmp_ref[c, pl.ds(s, tm // SSM_CHUNK, stride=SSM_CHUNK), :].astype(BF16))


def _inproj(x, w_tok, w_vt, wff_t, cos, sin, layer):
    tm = TM_PROJ
    nrope = SEQ // tm
    return pl.pallas_call(
        _inproj_kernel,
        grid=(TOKENS // tm,),
        in_specs=[
            pl.BlockSpec((tm, D_MODEL), lambda i: (i, 0)),
            pl.BlockSpec((None, N_SEG + 1, D_MODEL, 512), lambda i: (layer, 0, 0, 0)),
            pl.BlockSpec((None, 2, 512, D_MODEL), lambda i: (layer, 0, 0, 0)),
            pl.BlockSpec((None, FOX_HEADS, D_MODEL), lambda i: (layer, 0, 0)),
            pl.BlockSpec((tm, LANES), lambda i: (i % nrope, 0)),
            pl.BlockSpec((tm, LANES), lambda i: (i % nrope, 0)),
        ],
        out_specs=[
            pl.BlockSpec((N_SEG, tm, 512), lambda i: (0, i, 0)),
            pl.BlockSpec((SSM_CHUNK, tm // SSM_CHUNK, SSM_WIDTH), lambda i: (0, i, 0)),
            pl.BlockSpec((2, 512, tm), lambda i: (0, 0, i)),
            pl.BlockSpec((FOX_HEADS, tm), lambda i: (0, i)),
        ],
        out_shape=[
            jax.ShapeDtypeStruct((N_SEG, TOKENS, 512), BF16),
            jax.ShapeDtypeStruct((SSM_CHUNK, SSM_ROWS, SSM_WIDTH), BF16),
            jax.ShapeDtypeStruct((2, 512, TOKENS), BF16),
            jax.ShapeDtypeStruct((FOX_HEADS, TOKENS), F32),
        ],
        scratch_shapes=[pltpu.VMEM((SSM_WIDTH // LANES, tm, LANES), F32)],
        compiler_params=_params("parallel"),
        name="inproj",
    )(x, w_tok, w_vt, wff_t, cos, sin)


def _fgate_kernel(fft_ref, bias_ref, c_ref):
    z = fft_ref[...] + bias_ref[...]
    c = jnp.minimum(z, 0.0) - jnp.log1p(jnp.exp(-jnp.abs(z)))
    lane = lax.broadcasted_iota(jnp.int32, c.shape, 1)
    shift = 1
    while shift < SEQ:
        c = c + jnp.where(lane >= shift, pltpu.roll(c, shift, 1), 0.0)
        shift *= 2
    c = c * LOG2E
    padded = jnp.concatenate([c, jnp.zeros((LANES - FOX_HEADS, SEQ), F32)], axis=0)
    c_ref[...] = padded.T


def _fgate(fft, bias):
    return pl.pallas_call(
        _fgate_kernel,
        grid=(BATCH,),
        in_specs=[
            pl.BlockSpec((FOX_HEADS, SEQ), lambda b: (0, b)),
            pl.BlockSpec((FOX_HEADS, 1), lambda b: (0, 0)),
        ],
        out_specs=pl.BlockSpec((SEQ, LANES), lambda b: (b, 0)),
        out_shape=jax.ShapeDtypeStruct((TOKENS, LANES), F32),
        compiler_params=_params("parallel"),
        name="fgate",
    )(fft, bias)


ONES_ROWS = 16


def _online_softmax(s, m_ref, idx):
    m_old = m_ref[idx]
    m_new = jnp.maximum(m_old, jnp.max(s, axis=0, keepdims=True))
    m_ref[idx] = m_new
    return jnp.exp2(s - m_new).astype(BF16), jnp.exp2(m_old - m_new)


def _causal_mask_t(n):
    key = lax.broadcasted_iota(jnp.int32, (n, n), 0)
    query = lax.broadcasted_iota(jnp.int32, (n, n), 1)
    return key <= query


def _init_softmax_state(m_ref, acc_ref):
    m_ref[...] = jnp.full(m_ref.shape, NEG_INF, F32)
    acc_ref[...] = jnp.zeros(acc_ref.shape, F32)


def _with_ones(vt):
    return jnp.concatenate([vt, jnp.ones((ONES_ROWS, vt.shape[1]), BF16)], axis=0)


def _causal_sweep(i, tq, scores_of, consume):
    def keys(j):
        return pl.ds(pl.multiple_of(j * tq, tq), tq)

    def step(blocks):
        scores = [scores_of(k) for k, _ in blocks]
        for (k, masked), s in zip(blocks, scores):
            consume(s, k, masked)

    def pair(j, carry):
        step([(keys(2 * j), False), (keys(2 * j + 1), False)])
        return carry

    lax.fori_loop(0, i >> 1, pair, 0)

    @pl.when((i & 1) == 1)
    def _():
        step([(keys(i - 1), False), (keys(i), True)])

    @pl.when((i & 1) == 0)
    def _():
        step([(keys(i), True)])


BIAS_LANES = 8


def _split3(c):
    hi = c.astype(BF16)
    rest = c - hi.astype(F32)
    mid = rest.astype(BF16)
    lo = (rest - mid.astype(F32)).astype(BF16)
    return jnp.concatenate([hi, mid, lo], axis=1)


def _bias_select(pair, key_side):
    r = lax.broadcasted_iota(jnp.int32, (3 * LANES, 1), 0)
    d = lax.broadcasted_iota(jnp.int32, (1, LANES), 1)
    piece, src = r >> 7, r & (LANES - 1)
    shift = 3 if key_side else 0
    first = jnp.logical_and(src == 2 * pair, d == piece + shift)
    second = jnp.logical_and(src == 2 * pair + 1, d == piece + shift + BIAS_LANES)
    value = -1.0 if key_side else 1.0
    return jnp.where(jnp.logical_or(first, second), value, 0.0).astype(BF16)


def _bias_ones(key_side):
    d = lax.broadcasted_iota(jnp.int32, (1, LANES), 1)
    within = d & (BIAS_LANES - 1)
    hit = (within < 3) if key_side else jnp.logical_and(within >= 3, within < 6)
    return jnp.where(jnp.logical_and(hit, d < 2 * BIAS_LANES), 1.0, 0.0)


def _fox_kernel(q_ref, k_ref, vt_ref, c_ref, o_ref, m_ref, acc_ref, kb_ref):
    i = pl.program_id(1)
    tq = q_ref.shape[0]
    hd = FOX_HEAD_DIM
    npair = FOX_HEADS // 2
    lane = lax.broadcasted_iota(jnp.int32, (tq, LANES), 1)
    lo = lane < hd
    lo_bias = lane < BIAS_LANES
    _init_softmax_state(m_ref, acc_ref)

    @pl.when(i == 0)
    def _():
        pieces = _split3(c_ref[...])
        for hp in range(npair):
            kb_ref[hp] = (_dot(pieces, _bias_select(hp, True)) + _bias_ones(True)).astype(BF16)

    q_pieces = _split3(c_ref[pl.ds(pl.multiple_of(i * tq, tq), tq), :])
    q_ops = []
    for hp in range(npair):
        q = q_ref[:, hp * LANES:(hp + 1) * LANES]
        zero = jnp.zeros_like(q)
        qb = (_dot(q_pieces, _bias_select(hp, False)) + _bias_ones(False)).astype(BF16)
        q_ops.append(jnp.concatenate([jnp.where(lo, q, zero), jnp.where(lo_bias, qb, zero)], axis=1))
        q_ops.append(jnp.concatenate([jnp.where(lo, zero, q), jnp.where(lo_bias, zero, qb)], axis=1))

    def scores_of(keys):
        out = []
        for hp in range(npair):
            k_op = jnp.concatenate([k_ref[keys, hp * LANES:(hp + 1) * LANES], kb_ref[hp, keys, :]], axis=1)
            out.append(_dot_nt(k_op, q_ops[2 * hp]))
            out.append(_dot_nt(k_op, q_ops[2 * hp + 1]))
        return out

    def consume(scores, keys, masked):
        probs = []
        for h in range(FOX_HEADS):
            s = scores[h]
            if masked:
                s = jnp.where(_causal_mask_t(tq), s, NEG_INF)
            probs.append(_online_softmax(s, m_ref, h))
        for h in range(FOX_HEADS):
            p, alpha = probs[h]
            pv = _dot(_with_ones(vt_ref[h * hd:(h + 1) * hd, keys]), p)
            acc_ref[h] = acc_ref[h] * alpha + pv

    _causal_sweep(i, tq, scores_of, consume)
    for hp in range(npair):
        heads = [acc_ref[h, :hd, :] / acc_ref[h, hd:hd + 1, :] for h in (2 * hp, 2 * hp + 1)]
        o_ref[:, hp * LANES:(hp + 1) * LANES] = jnp.concatenate(heads, axis=0).T.astype(BF16)


def _fox(proj, vt, c):
    nq = SEQ // TQ
    return pl.pallas_call(
        _fox_kernel,
        grid=(BATCH, nq),
        in_specs=[
            pl.BlockSpec((None, TQ, 512), lambda b, i: (SEG_FQ, b * nq + i, 0)),
            pl.BlockSpec((None, SEQ, 512), lambda b, i: (SEG_FK, b, 0)),
            pl.BlockSpec((None, 512, SEQ), lambda b, i: (VT_FOX, 0, b)),
            pl.BlockSpec((SEQ, LANES), lambda b, i: (b, 0)),
        ],
        out_specs=pl.BlockSpec((TQ, 512), lambda b, i: (b * nq + i, 0)),
        out_shape=jax.ShapeDtypeStruct((TOKENS, 512), BF16),
        scratch_shapes=[pltpu.VMEM((FOX_HEADS, 1, TQ), F32),
                        pltpu.VMEM((FOX_HEADS, FOX_HEAD_DIM + ONES_ROWS, TQ), F32),
                        pltpu.VMEM((FOX_HEADS // 2, SEQ, LANES), BF16)],
        compiler_params=_params("parallel", "arbitrary"),
        name="fox_attn",
    )(proj, proj, vt, c)


def _diff_kernel(lam_init, q_ref, k_ref, vt_ref, lam_ref, g_ref, o_ref, m_ref, acc_ref):
    i = pl.program_id(1)
    tq = q_ref.shape[0]
    dv = LANES
    lane = lax.broadcasted_iota(jnp.int32, (tq, LANES), 1)
    lo = lane < DIFF_HEAD_DIM
    _init_softmax_state(m_ref, acc_ref)

    def scores_of(keys):
        out = []
        for h in range(DIFF_HEADS):
            cols = slice(h * LANES, (h + 1) * LANES)
            q = q_ref[:, cols]
            zero = jnp.zeros_like(q)
            kb = k_ref[keys, cols]
            out.append(_dot_nt(kb, jnp.where(lo, q, zero)))
            out.append(_dot_nt(kb, jnp.where(lo, zero, q)))
        return out

    def consume(scores, keys, masked):
        probs = []
        for idx in range(2 * DIFF_HEADS):
            s = scores[idx]
            if masked:
                s = jnp.where(_causal_mask_t(tq), s, NEG_INF)
            probs.append(_online_softmax(s, m_ref, idx))
        for h in range(DIFF_HEADS):
            vtb = _with_ones(vt_ref[h * dv:(h + 1) * dv, keys])
            for idx in (2 * h, 2 * h + 1):
                p, alpha = probs[idx]
                acc_ref[idx] = acc_ref[idx] * alpha + _dot(vtb, p)

    _causal_sweep(i, tq, scores_of, consume)
    lam_rows = lam_ref[...]
    lam = (jnp.exp(jnp.sum(lam_rows[0:1] * lam_rows[1:2], axis=1, keepdims=True))
           - jnp.exp(jnp.sum(lam_rows[2:3] * lam_rows[3:4], axis=1, keepdims=True)) + lam_init)
    for h in range(DIFF_HEADS):
        cols = slice(h * LANES, (h + 1) * LANES)
        o1 = acc_ref[2 * h, :dv, :] / acc_ref[2 * h, dv:dv + 1, :]
        o2 = acc_ref[2 * h + 1, :dv, :] / acc_ref[2 * h + 1, dv:dv + 1, :]
        o = o1 - lam * o2
        o = o * lax.rsqrt(jnp.mean(o * o, axis=0, keepdims=True) + RMS_EPS)
        o_ref[:, cols] = (o.T * g_ref[:, cols] * (1.0 - lam_init)).astype(BF16)


def _diff(proj, vt, lam, g, lam_init):
    nq = SEQ // TQ
    return pl.pallas_call(
        functools.partial(_diff_kernel, lam_init),
        grid=(BATCH, nq),
        in_specs=[
            pl.BlockSpec((None, TQ, 512), lambda b, i: (SEG_DQ, b * nq + i, 0)),
            pl.BlockSpec((None, SEQ, 512), lambda b, i: (SEG_DK, b, 0)),
            pl.BlockSpec((None, 512, SEQ), lambda b, i: (VT_DIFF, 0, b)),
            pl.BlockSpec((4, DIFF_HEAD_DIM), lambda b, i: (0, 0)),
            pl.BlockSpec((1, 512), lambda b, i: (0, 0)),
        ],
        out_specs=pl.BlockSpec((TQ, 512), lambda b, i: (b * nq + i, 0)),
        out_shape=jax.ShapeDtypeStruct((TOKENS, 512), BF16),
        scratch_shapes=[pltpu.VMEM((2 * DIFF_HEADS, 1, TQ), F32),
                        pltpu.VMEM((2 * DIFF_HEADS, LANES + ONES_ROWS, TQ), F32)],
        compiler_params=_params("parallel", "arbitrary"),
        name="diff_attn",
    )(proj, proj, vt, lam, g)


def _spread(compact, select, row_group, col_group):
    full = _dot(compact, select)
    return jnp.where(row_group == col_group, full, 0.0).astype(BF16)


def _s5_kernel(u_ref, kc_ref, abre_ref, abim_ref, acre_ref, acim_ref, are_ref, aim_ref, y_ref, *state):
    nslab = SSM_SCOLS // LANES
    sre, sim = state[:nslab], state[nslab:]

    def iota(shape, dim):
        return lax.broadcasted_iota(jnp.int32, shape, dim)

    i_tn, j_cols = iota((SSM_TN, 1), 0), iota((1, SSM_COLS), 1)
    sel_tn = jnp.where(jnp.logical_and((i_tn >> 4) == (j_cols >> 7), (i_tn & 15) == (j_cols & 15)),
                       1.0, 0.0).astype(BF16)
    i_p, j_sc = iota((SSM_STATE, 1), 0), iota((1, SSM_SCOLS), 1)
    sel_p = jnp.where(i_p == (j_sc & (SSM_STATE - 1)), 1.0, 0.0).astype(BF16)
    grp_rows_cols = (iota((SSM_COLS, 1), 0) >> 4) & (SSM_GBLK - 1)
    grp_cols_cols = (j_cols >> 4) & (SSM_GBLK - 1)
    grp_rows_state = iota((SSM_SCOLS, 1), 0) >> 6
    grp_cols_state = j_sc >> 6
    kc = kc_ref[...]
    lane_tn = iota(kc.shape, 1)
    am = jnp.concatenate(
        [kc] + [jnp.where(lane_tn >= s * SSM_GROUP, pltpu.roll(kc, s * SSM_GROUP, 1), 0.0)
                for s in range(1, SSM_CHUNK)], axis=0).astype(BF16)
    m = _spread(am, sel_tn, grp_rows_cols, grp_cols_cols)
    bre = _spread(abre_ref[...], sel_p, grp_rows_cols, grp_cols_state)
    bim = _spread(abim_ref[...], sel_p, grp_rows_cols, grp_cols_state)
    cre = _spread(acre_ref[...], sel_tn, grp_rows_state, grp_cols_cols)
    cim = _spread(acim_ref[...], sel_tn, grp_rows_state, grp_cols_cols)

    u = jnp.concatenate([u_ref[s] for s in range(SSM_CHUNK)], axis=1)
    for k in range(0, nslab, 2):
        cols = slice(k * LANES, (k + 2) * LANES)
        for dst, b in ((sre, bre), (sim, bim)):
            part = _dot(u, b[:, cols])
            dst[k][...] = part[:, :LANES]
            dst[k + 1][...] = part[:, LANES:]
    ar = [are_ref[:, k * LANES:(k + 1) * LANES] for k in range(nslab)]
    ai = [aim_ref[:, k * LANES:(k + 1) * LANES] for k in range(nslab)]

    def body(c, carry):
        rows = pl.ds(c, BATCH, stride=SSM_NCHUNK)
        nxt = []
        for k in range(nslab):
            xr, xi = carry[k]
            sr = sre[k][rows, :]
            si = sim[k][rows, :]
            sre[k][rows, :] = xr
            sim[k][rows, :] = xi
            nxt.append((ar[k] * xr - ai[k] * xi + sr, ar[k] * xi + ai[k] * xr + si))
        return tuple(nxt)

    zero = jnp.zeros((BATCH, LANES), F32)
    lax.fori_loop(0, SSM_NCHUNK, body, tuple((zero, zero) for _ in range(nslab)))
    xr = jnp.concatenate([r[...] for r in sre], axis=1).astype(BF16)
    xi = jnp.concatenate([r[...] for r in sim], axis=1).astype(BF16)
    for c in range(SSM_COLS // (2 * LANES)):
        cols = slice(c * 2 * LANES, (c + 1) * 2 * LANES)
        y = _dot(u, m[:, cols]) + _dot(xr, cre[:, cols]) + _dot(xi, cim[:, cols])
        y_ref[2 * c] = y[:, :LANES].astype(BF16)
        y_ref[2 * c + 1] = y[:, LANES:].astype(BF16)


def _s5_scan(u_stack, mats, layer):
    am, abre, abim, acre, acim, are, aim = mats

    def spec(*shape):
        return pl.BlockSpec((None, None) + shape, lambda g: (layer, g) + (0,) * len(shape))

    io_spec = pl.BlockSpec((SSM_CHUNK, SSM_ROWS, LANES), lambda g: (0, 0, g))
    return pl.pallas_call(
        _s5_kernel,
        grid=(SSM_NGBLK,),
        in_specs=[io_spec, spec(LANES, SSM_TN),
                  spec(SSM_COLS, SSM_STATE), spec(SSM_COLS, SSM_STATE),
                  spec(SSM_SCOLS, SSM_TN), spec(SSM_SCOLS, SSM_TN),
                  spec(1, SSM_SCOLS), spec(1, SSM_SCOLS)],
        out_specs=io_spec,
        out_shape=jax.ShapeDtypeStruct((SSM_CHUNK, SSM_ROWS, SSM_WIDTH), BF16),
        scratch_shapes=[pltpu.VMEM((SSM_ROWS, LANES), F32) for _ in range(2 * SSM_SCOLS // LANES)],
        compiler_params=_params("parallel"),
        name="s5_scan",
    )(u_stack, am, abre, abim, acre, acim, are, aim)


def _s5_matrices(lam_re, lam_im, log_dt, b_re, b_im, c_re, c_im):
    hi = lax.Precision.HIGHEST
    L, NB, GB = SSM_CHUNK, SSM_NGBLK, SSM_GBLK
    lr = lam_re.astype(F32)
    li = lam_im.astype(F32)
    dt = jnp.exp(log_dt.astype(F32))[:, None]
    mag = jnp.exp(lr * dt)
    ang = li * dt
    ar = mag * jnp.cos(ang)
    ai = mag * jnp.sin(ang)
    er = ar - 1.0
    ei = ai
    den = lr * lr + li * li
    qr = (er * lr + ei * li) / den
    qi = (ei * lr - er * li) / den
    br = b_re.astype(F32)
    bi = b_im.astype(F32)
    bbr = qr[:, :, None] * br - qi[:, :, None] * bi
    bbi = qr[:, :, None] * bi + qi[:, :, None] * br
    k = jnp.arange(L + 1, dtype=F32)[:, None, None]
    pmag = jnp.exp(lr[None] * dt[None] * k)
    pr = pmag * jnp.cos(ang[None] * k)
    pi = pmag * jnp.sin(ang[None] * k)
    cr = c_re.astype(F32)
    ci = c_im.astype(F32)
    e_r = cr[None] * pr[:, :, None, :] - ci[None] * pi[:, :, None, :]
    e_i = cr[None] * pi[:, :, None, :] + ci[None] * pr[:, :, None, :]
    kern = jnp.einsum('kgnq,gqm->kgnm', jnp.concatenate([e_r[:L], -e_i[:L]], axis=-1),
                      jnp.concatenate([bbr, bbi], axis=1), precision=hi)
    kc = jnp.transpose(kern.reshape(L, NB, GB, SSM_GROUP, SSM_GROUP), (1, 2, 4, 0, 3))
    kc = kc.reshape(NB, LANES, SSM_TN)
    krev = (L - 1) - jnp.arange(L, dtype=F32)[:, None, None]
    rmag = jnp.exp(lr[None] * dt[None] * krev)
    rev_r = rmag * jnp.cos(ang[None] * krev)
    rev_i = rmag * jnp.sin(ang[None] * krev)
    sb_r = rev_r[:, :, :, None] * bbr[None] - rev_i[:, :, :, None] * bbi[None]
    sb_i = rev_r[:, :, :, None] * bbi[None] + rev_i[:, :, :, None] * bbr[None]

    def state_in(t):
        t = t.reshape(L, NB, GB, SSM_STATE, SSM_GROUP)
        return jnp.transpose(t, (1, 0, 2, 4, 3)).reshape(NB, SSM_COLS, SSM_STATE)

    def state_out(t):
        t = t.reshape(L, NB, GB, SSM_GROUP, SSM_STATE)
        return jnp.transpose(t, (1, 2, 4, 0, 3)).reshape(NB, SSM_SCOLS, SSM_TN)

    are = pr[L].reshape(NB, 1, SSM_SCOLS)
    aim = pi[L].reshape(NB, 1, SSM_SCOLS)
    return (kc, state_in(sb_r).astype(BF16), state_in(sb_i).astype(BF16),
            state_out(e_r[1:]).astype(BF16), (-state_out(e_i[1:])).astype(BF16), are, aim)


def _s5_post_kernel(y_ref, u_ref, d_ref, w_ref, o_ref, tmp_ref):
    nrow = y_ref.shape[1]
    ycat = jnp.concatenate([y_ref[s] for s in range(SSM_CHUNK)], axis=0).astype(F32)
    ucat = jnp.concatenate([u_ref[s] for s in range(SSM_CHUNK)], axis=0).astype(F32)
    y = jax.nn.gelu(ycat + d_ref[...] * ucat, approximate=True)
    z = _dot(y.astype(BF16), w_ref[...])
    out = z[:, :SSM_WIDTH] * jax.nn.sigmoid(z[:, SSM_WIDTH:])
    for s in range(SSM_CHUNK):
        for c in range(SSM_WIDTH // LANES):
            tmp_ref[c, pl.ds(s, nrow, stride=SSM_CHUNK), :] = (
                out[s * nrow:(s + 1) * nrow, c * LANES:(c + 1) * LANES])
    for c in range(SSM_WIDTH // LANES):
        o_ref[:, c * LANES:(c + 1) * LANES] = tmp_ref[c].astype(BF16)


def _s5_post(y_stack, u_stack, d, w_glu, layer):
    tm = TM_POST
    stack_spec = pl.BlockSpec((SSM_CHUNK, tm // SSM_CHUNK, SSM_WIDTH), lambda i: (0, i, 0))
    return pl.pallas_call(
        _s5_post_kernel,
        grid=(TOKENS // tm,),
        in_specs=[
            stack_spec,
            stack_spec,
            pl.BlockSpec((1, SSM_WIDTH), lambda i: (0, 0)),
            pl.BlockSpec((None, SSM_WIDTH, 2 * SSM_WIDTH), lambda i: (layer, 0, 0)),
        ],
        out_specs=pl.BlockSpec((tm, SSM_WIDTH), lambda i: (i, 0)),
        out_shape=jax.ShapeDtypeStruct((TOKENS, SSM_WIDTH), BF16),
        scratch_shapes=[pltpu.VMEM((SSM_WIDTH // LANES, tm, LANES), F32)],
        compiler_params=_params("parallel"),
        name="s5_post",
    )(y_stack, u_stack, d, w_glu)


def _route(x, w_ref, b_ref):
    xh = x.astype(BF16)
    xl = (x - xh.astype(F32)).astype(BF16)
    hw = _dot(xh, w_ref[...])
    logits = hw[:, :LANES] + (hw[:, LANES:] + _dot(xl, w_ref[:, :LANES])) + b_ref[...]
    lane = lax.broadcasted_iota(jnp.int32, logits.shape, 1).astype(F32)
    big = jnp.float32(LANES)
    neg = jnp.float32(-jnp.inf)

    def first_argmax(v, vmax):
        return jnp.min(jnp.where(v == vmax, lane, big), axis=1, keepdims=True)

    gl = jnp.where(lane < N_EXPERT_GROUPS, logits, neg)
    ge = jnp.exp(gl - jnp.max(gl, axis=1, keepdims=True))
    pg = ge / jnp.sum(ge, axis=1, keepdims=True)
    gp = jnp.max(pg, axis=1, keepdims=True)
    gi = first_argmax(pg, gp)
    base = N_EXPERT_GROUPS + EXPERTS_PER_GROUP * gi
    sel = jnp.logical_and(lane >= base, lane < base + EXPERTS_PER_GROUP)
    el = jnp.where(sel, logits, neg)
    ev1 = jnp.max(el, axis=1, keepdims=True)
    i1 = first_argmax(el, ev1)
    el2 = jnp.where(lane == i1, neg, el)
    ev2 = jnp.max(el2, axis=1, keepdims=True)
    i2 = first_argmax(el2, ev2)
    e2 = jnp.exp(ev2 - ev1)
    den = 1.0 + e2
    w1 = gp * (1.0 / den)
    w2 = gp * (e2 / den)
    return (jnp.where(lane == i1, w1, 0.0) + jnp.where(lane == i2, w2, 0.0)
            + jnp.where(lane == 0.0, gi, 0.0))


def _router_kernel(x_ref, w_ref, b_ref, o_ref):
    o_ref[...] = _route(x_ref[...], w_ref, b_ref)


def _router(x, w_r, b_r, layer):
    tm = TM_ROUTE
    return pl.pallas_call(
        _router_kernel,
        grid=(TOKENS // tm,),
        in_specs=[
            pl.BlockSpec((tm, D_MODEL), lambda i: (i, 0)),
            pl.BlockSpec((None, D_MODEL, 2 * LANES), lambda i: (layer, 0, 0)),
            pl.BlockSpec((None, 1, LANES), lambda i: (layer, 0, 0)),
        ],
        out_specs=pl.BlockSpec((tm, LANES), lambda i: (i, 0)),
        out_shape=jax.ShapeDtypeStruct((TOKENS, LANES), F32),
        compiler_params=_params("parallel"),
        name="router",
    )(x, w_r, b_r)


def _merge_kernel(x_ref, ys_ref, yd_ref, yf_ref, wgl_ref, wbr_ref, wout_ref, g_ref, b_ref, o_ref):
    x = x_ref[...]
    xb = x.astype(BF16)
    merged = None
    for n, y_ref in enumerate((ys_ref, yd_ref, yf_ref)):
        gate = jax.nn.sigmoid(_dot(xb, wgl_ref[:, n * D_MODEL:(n + 1) * D_MODEL]))
        term = gate * _dot(y_ref[...], wbr_ref[n])
        merged = term if merged is None else merged + term
    mix = _dot(merged.astype(BF16), wout_ref[...])
    o_ref[...] = _layer_norm(ALPHA * x + mix, g_ref[...], b_ref[...])


def _merge(x, ys, yd, yf, wgl, wbr, wout, g, b, layer):
    tm = TM_MERGE
    row = lambda i: (i, 0)
    const2 = lambda i: (0, 0)
    per_layer = lambda i: (layer, 0, 0)
    return pl.pallas_call(
        _merge_kernel,
        grid=(TOKENS // tm,),
        in_specs=[
            pl.BlockSpec((tm, D_MODEL), row),
            pl.BlockSpec((tm, 512), row),
            pl.BlockSpec((tm, 512), row),
            pl.BlockSpec((tm, 512), row),
            pl.BlockSpec((None, D_MODEL, N_BRANCH * D_MODEL), per_layer),
            pl.BlockSpec((None, N_BRANCH, BRANCH_WIDTH, D_MODEL), lambda i: (layer, 0, 0, 0)),
            pl.BlockSpec((None, D_MODEL, D_MODEL), per_layer),
            pl.BlockSpec((1, D_MODEL), const2),
            pl.BlockSpec((1, D_MODEL), const2),
        ],
        out_specs=pl.BlockSpec((tm, D_MODEL), row),
        out_shape=jax.ShapeDtypeStruct((TOKENS, D_MODEL), F32),
        compiler_params=_params("parallel"),
        name="merge_ln",
    )(x, ys, yd, yf, wgl, wbr, wout, g, b)


MOE_SUB = 128


def _one_hot(cond):
    return jnp.where(cond, 1.0, 0.0).astype(BF16)


def _moe_sort(x_ref, gate_ref, xs_ref, gs_ref, ys_ref, pos_ref, off_ref):
    tm = x_ref.shape[0]
    gates = gate_ref[...]
    lane = lax.broadcasted_iota(jnp.int32, (1, LANES), 1)
    t_col = lax.broadcasted_iota(jnp.int32, (tm, 1), 0)
    t_row = lax.broadcasted_iota(jnp.int32, (1, tm), 1)
    member = jnp.where(gates[:, 0:1] == lane.astype(F32), 1.0, 0.0)
    member_b = member.astype(BF16)
    counts = jnp.sum(member, axis=0, keepdims=True)
    cnt = [jnp.sum(jnp.where(lane == k, counts, 0.0), axis=1, keepdims=True)
           for k in range(N_EXPERT_GROUPS - 1)]
    start = [cnt[0], cnt[0] + cnt[1], cnt[0] + cnt[1] + cnt[2]]
    off_ref[0] = 0
    for k in range(N_EXPERT_GROUPS - 1):
        off_ref[k + 1] = start[k][0, 0].astype(jnp.int32)
    off_ref[N_EXPERT_GROUPS] = tm

    def offsets(index):
        return sum(jnp.where(index > k, cnt[k], 0.0) for k in range(N_EXPERT_GROUPS - 1))

    earlier = _one_hot(t_row < t_col)
    rank_col = _dot(earlier, member_b)
    pos_col = jnp.sum(member * (offsets(lane) + rank_col), axis=1, keepdims=True)
    pos_ref[...] = jnp.broadcast_to(pos_col, (tm, LANES))
    pos_hi = jnp.floor(pos_col * (1.0 / 32.0))
    pos_lo = pos_col - 32.0 * pos_hi
    digits = jnp.where(lane == 0, pos_hi, jnp.where(lane == 1, pos_lo, 0.0)).astype(BF16)
    pick = _one_hot(lax.broadcasted_iota(jnp.int32, (8, LANES), 0)
                    == lax.broadcasted_iota(jnp.int32, (8, LANES), 1))
    digits_t = _dot_nt(pick, digits)
    pos_row = 32.0 * digits_t[0:1, :] + digits_t[1:2, :]
    perm = _one_hot(t_col.astype(F32) == pos_row)
    xs_ref[...] = _dot(perm, x_ref[...].astype(BF16)).astype(BF16)
    g_hi = gates.astype(BF16)
    rest = gates - g_hi.astype(F32)
    g_mid = rest.astype(BF16)
    g_lo = (rest - g_mid.astype(F32)).astype(BF16)
    moved = _dot(perm, jnp.concatenate([g_hi, g_mid, g_lo], axis=1))
    gs_ref[...] = moved[:, :LANES] + moved[:, LANES:2 * LANES] + moved[:, 2 * LANES:]
    ys_ref[...] = jnp.zeros(ys_ref.shape, F32)


def _moe_kernel(x_ref, gate_ref, wg_ref, wu_ref, wd_ref, g_ref, b_ref, o_ref,
                xs_ref, gs_ref, ys_ref, pos_ref, off_ref):
    grp = pl.program_id(1)
    tm = x_ref.shape[0]

    @pl.when(grp == 0)
    def _():
        _moe_sort(x_ref, gate_ref, xs_ref, gs_ref, ys_ref, pos_ref, off_ref)

    first = N_EXPERT_GROUPS + EXPERTS_PER_GROUP * grp
    shift = MOE_SUB.bit_length() - 1

    def sub_tile(k, carry):
        rows = pl.ds(pl.multiple_of(k * MOE_SUB, MOE_SUB), MOE_SUB)
        xb = xs_ref[rows, :]
        gates = gs_ref[rows, :]
        lane = lax.broadcasted_iota(jnp.int32, gates.shape, 1)
        hidden = []
        for e in range(EXPERTS_PER_GROUP):
            gate = jnp.sum(jnp.where(lane == first + e, gates, 0.0), axis=1, keepdims=True)
            he = jax.nn.silu(_dot(xb, wg_ref[e])) * _dot(xb, wu_ref[e]) * gate
            hidden.append(he.astype(BF16))
        ys_ref[rows, :] += _dot(jnp.concatenate(hidden, axis=1), wd_ref[...])
        return carry

    lo = off_ref[grp] >> shift
    hi = (off_ref[grp + 1] + (MOE_SUB - 1)) >> shift
    lax.fori_loop(lo, hi, sub_tile, 0)

    @pl.when(grp == N_EXPERT_GROUPS - 1)
    def _():
        slot = lax.broadcasted_iota(jnp.int32, (1, tm), 1).astype(F32)
        unsort = _one_hot(pos_ref[:, 0:1] == slot)
        y = _dot(unsort, ys_ref[...].astype(BF16))
        o_ref[...] = _layer_norm(ALPHA * x_ref[...] + y, g_ref[...], b_ref[...])


def _moe(x, gates, wg, wu, wd_grouped, g, b, layer):
    tm = TM_MOE
    return pl.pallas_call(
        _moe_kernel,
        grid=(TOKENS // tm, N_EXPERT_GROUPS),
        in_specs=[
            pl.BlockSpec((tm, D_MODEL), lambda i, e: (i, 0)),
            pl.BlockSpec((tm, LANES), lambda i, e: (i, 0)),
            pl.BlockSpec((None, EXPERTS_PER_GROUP, D_MODEL, D_EXPERT), lambda i, e: (layer, e, 0, 0)),
            pl.BlockSpec((None, EXPERTS_PER_GROUP, D_MODEL, D_EXPERT), lambda i, e: (layer, e, 0, 0)),
            pl.BlockSpec((None, None, EXPERTS_PER_GROUP * D_EXPERT, D_MODEL), lambda i, e: (layer, e, 0, 0)),
            pl.BlockSpec((1, D_MODEL), lambda i, e: (0, 0)),
            pl.BlockSpec((1, D_MODEL), lambda i, e: (0, 0)),
        ],
        out_specs=pl.BlockSpec((tm, D_MODEL), lambda i, e: (i, 0)),
        out_shape=jax.ShapeDtypeStruct((TOKENS, D_MODEL), F32),
        scratch_shapes=[pltpu.VMEM((tm, D_MODEL), BF16),
                        pltpu.VMEM((tm, LANES), F32),
                        pltpu.VMEM((tm, D_MODEL), F32),
                        pltpu.VMEM((tm, LANES), F32),
                        pltpu.SMEM((8,), jnp.int32)],
        compiler_params=_params("parallel", "arbitrary"),
        name="moe_ln",
    )(x, gates, wg, wu, wd_grouped, g, b)


def _rope_tables():
    pos = jnp.arange(SEQ, dtype=F32)
    inv_freq = ROPE_THETA ** (-jnp.arange(0, DIFF_HEAD_DIM, 2, dtype=F32) / DIFF_HEAD_DIM)
    ang = pos[:, None] * inv_freq[None, :]
    emb = jnp.concatenate([ang, ang], axis=-1)
    cos = jnp.cos(emb)
    sin = jnp.sin(emb)
    sign = jnp.where(jnp.arange(DIFF_HEAD_DIM) < DIFF_HEAD_DIM // 2, -1.0, 1.0).astype(F32)
    return jnp.tile(cos, (1, 2)), jnp.tile(sin * sign, (1, 2))


def _split_w_in(w_in):
    qscale = DIFF_HEAD_DIM ** -0.5 * LOG2E

    def seg(k):
        return w_in[:, :, k * 512:(k + 1) * 512]

    w_tok = jnp.stack([seg(0), seg(1) * qscale, seg(2), seg(4) * qscale, seg(5)], axis=1).astype(BF16)
    w_vt = jnp.stack([jnp.swapaxes(seg(3), 1, 2), jnp.swapaxes(seg(6), 1, 2)], axis=1).astype(BF16)
    off = 7 * 512
    wff_t = jnp.swapaxes(w_in[:, :, off:off + FOX_HEADS], 1, 2).astype(BF16)
    wgl = w_in[:, :, off + FOX_HEADS:].astype(BF16)
    return w_tok, w_vt, wff_t, wgl


def kernel(x, w_in, w_branch, w_out, ssm_lambda_re, ssm_lambda_im, ssm_log_dt, ssm_b_re, ssm_b_im,
           ssm_c_re, ssm_c_im, ssm_d, ssm_w_glu, diff_lambda, diff_norm_g, fox_f_bias, ln1_g, ln1_b,
           moe_w_group, moe_b_group, moe_w_expert, moe_b_expert, moe_w_gate, moe_w_up, moe_w_down,
           ln2_g, ln2_b):
    cos, sin = _rope_tables()
    w_tok, w_vt, wff_t, wgl = _split_w_in(w_in)
    s5_mats = jax.vmap(_s5_matrices)(ssm_lambda_re, ssm_lambda_im, ssm_log_dt, ssm_b_re, ssm_b_im,
                                     ssm_c_re, ssm_c_im)
    w_glu = ssm_w_glu.astype(BF16)
    w_br = w_branch.astype(BF16)
    w_o = w_out.astype(BF16)
    w_gate, w_up = moe_w_gate.astype(BF16), moe_w_up.astype(BF16)
    w_down = moe_w_down.astype(BF16).reshape(DEPTH, N_EXPERT_GROUPS, EXPERTS_PER_GROUP * D_EXPERT, D_MODEL)
    w_r = jnp.concatenate([moe_w_group, moe_w_expert], axis=2).astype(F32)
    w_r = jnp.pad(w_r, ((0, 0), (0, 0), (0, LANES - w_r.shape[2])))
    w_r_hi = w_r.astype(BF16)
    w_r = jnp.concatenate([w_r_hi, (w_r - w_r_hi.astype(F32)).astype(BF16)], axis=2)
    b_r = jnp.concatenate([moe_b_group, moe_b_expert], axis=1).astype(F32)
    b_r = jnp.pad(b_r, ((0, 0), (0, LANES - b_r.shape[1]))).reshape(DEPTH, 1, LANES)

    h = x.reshape(TOKENS, D_MODEL)
    for l in range(DEPTH):
        lam_init = 0.8 - 0.6 * math.exp(-0.3 * l)
        proj, u_stack, vt, fft = _inproj(h, w_tok, w_vt, wff_t, cos, sin, l)

        y_stack = _s5_scan(u_stack, s5_mats, l)
        y_ssm = _s5_post(y_stack, u_stack, ssm_d[l].reshape(1, SSM_WIDTH).astype(F32), w_glu, l)

        y_diff = _diff(proj, vt, diff_lambda[l].astype(F32),
                       diff_norm_g[l].reshape(1, 512).astype(F32), lam_init)
        c = _fgate(fft, fox_f_bias[l].reshape(FOX_HEADS, 1).astype(F32))
        y_fox = _fox(proj, vt, c)

        h = _merge(h, y_ssm, y_diff, y_fox, wgl, w_br, w_o,
                   ln1_g[l].reshape(1, D_MODEL), ln1_b[l].reshape(1, D_MODEL), l)

        gates = _router(h, w_r, b_r, l)
        h = _moe(h, gates, w_gate, w_up, w_down,
                 ln2_g[l].reshape(1, D_MODEL), ln2_b[l].reshape(1, D_MODEL), l)
    return h.reshape(BATCH, SEQ, D_MODEL)
```

```python
import functools
import math

import jax
import jax.numpy as jnp
from jax import lax
from jax.experimental import pallas as pl
from jax.experimental.pallas import tpu as pltpu

F32 = jnp.float32
BF16 = jnp.bfloat16

D_MODEL = 1024
BATCH = 8
SEQ = 2048
DEPTH = 4
TOKENS = BATCH * SEQ

SSM_WIDTH = 512
SSM_GROUP = 16
SSM_GROUPS = 32
SSM_STATE = 64
SSM_CHUNK = 8
SSM_NCHUNK = SEQ // SSM_CHUNK
SSM_ROWS = BATCH * SSM_NCHUNK
SSM_GBLK = 8
SSM_NGBLK = SSM_GROUPS // SSM_GBLK
SSM_COLS = SSM_CHUNK * 128
SSM_SCOLS = SSM_GBLK * SSM_STATE
SSM_TN = SSM_CHUNK * SSM_GROUP

DIFF_HEADS = 4
DIFF_HEAD_DIM = 64
FOX_HEADS = 8
FOX_HEAD_DIM = 64
BRANCH_WIDTH = 512
N_BRANCH = 3
ROPE_THETA = 10000.0

N_EXPERT_GROUPS = 4
EXPERTS_PER_GROUP = 4
N_EXPERTS = 16
D_EXPERT = 256

ALPHA = (2 * DEPTH) ** 0.25
LN_EPS = 1e-5
RMS_EPS = 1e-6
NEG_INF = -1e30
LOG2E = math.log2(math.e)

LANES = 128
N_SEG = 4
SEG_DQ, SEG_DK, SEG_FQ, SEG_FK = range(N_SEG)
VT_DIFF, VT_FOX = 0, 1
VMEM_LIMIT = 56 * 1024 * 1024

TM_PROJ = 1024
TQ = 256
TM_POST = 512
TM_MERGE = 512
TM_ROUTE = 512
TM_MOE = 1024


def _params(*sem):
    return pltpu.CompilerParams(dimension_semantics=sem, vmem_limit_bytes=VMEM_LIMIT)


def _dot(a, b):
    return jnp.dot(a, b, preferred_element_type=F32)


def _dot_nt(a, b):
    return lax.dot_general(a, b, (((1,), (1,)), ((), ())), preferred_element_type=F32)


def _layer_norm(y, g, b):
    mu = jnp.mean(y, axis=-1, keepdims=True)
    d = y - mu
    var = jnp.mean(d * d, axis=-1, keepdims=True)
    return d * lax.rsqrt(var + LN_EPS) * g + b


def _inproj_kernel(x_ref, w_ref, wvt_ref, wff_ref, cos_ref, sin_ref, o_ref, u_ref, vt_ref, fft_ref,
                   tmp_ref):
    tm = x_ref.shape[0]
    xb = x_ref[...].astype(BF16)
    cos = cos_ref[...]
    sin = sin_ref[...]
    lane = lax.broadcasted_iota(jnp.int32, cos.shape, 1)
    first_half = (lane & 32) == 0
    for seg in range(N_SEG):
        acc = _dot(xb, w_ref[seg + 1])
        if seg in (SEG_DQ, SEG_DK):
            for c in range(512 // LANES):
                t = acc[:, c * LANES:(c + 1) * LANES]
                rot = jnp.where(first_half, pltpu.roll(t, LANES - 32, 1), pltpu.roll(t, 32, 1))
                o_ref[seg, :, c * LANES:(c + 1) * LANES] = (t * cos + rot * sin).astype(BF16)
        else:
            o_ref[seg] = acc.astype(BF16)
    for v in range(2):
        vt_ref[v] = _dot_nt(wvt_ref[v], xb).astype(BF16)
    fft_ref[...] = _dot_nt(wff_ref[...], xb)
    u = _dot(xb, w_ref[0])
    for c in range(SSM_WIDTH // LANES):
        tmp_ref[c] = u[:, c * LANES:(c + 1) * LANES]
    for s in range(SSM_CHUNK):
        for c in range(SSM_WIDTH // LANES):
            u_ref[s, :, c * LANES:(c + 1) * LANES] = (
                tmp_ref[c, pl.ds(s, tm // SSM_CHUNK, stride=SSM_CHUNK), :].astype(BF16))


def _inproj(x, w_tok, w_vt, wff_t, cos, sin, layer):
    tm = TM_PROJ
    nrope = SEQ // tm
    return pl.pallas_call(
        _inproj_kernel,
        grid=(TOKENS // tm,),
        in_specs=[
            pl.BlockSpec((tm, D_MODEL), lambda i: (i, 0)),
            pl.BlockSpec((None, N_SEG + 1, D_MODEL, 512), lambda i: (layer, 0, 0, 0)),
            pl.BlockSpec((None, 2, 512, D_MODEL), lambda i: (layer, 0, 0, 0)),
            pl.BlockSpec((None, FOX_HEADS, D_MODEL), lambda i: (layer, 0, 0)),
            pl.BlockSpec((tm, LANES), lambda i: (i % nrope, 0)),
            pl.BlockSpec((tm, LANES), lambda i: (i % nrope, 0)),
        ],
        out_specs=[
            pl.BlockSpec((N_SEG, tm, 512), lambda i: (0, i, 0)),
            pl.BlockSpec((SSM_CHUNK, tm // SSM_CHUNK, SSM_WIDTH), lambda i: (0, i, 0)),
            pl.BlockSpec((2, 512, tm), lambda i: (0, 0, i)),
            pl.BlockSpec((FOX_HEADS, tm), lambda i: (0, i)),
        ],
        out_shape=[
            jax.ShapeDtypeStruct((N_SEG, TOKENS, 512), BF16),
            jax.ShapeDtypeStruct((SSM_CHUNK, SSM_ROWS, SSM_WIDTH), BF16),
            jax.ShapeDtypeStruct((2, 512, TOKENS), BF16),
            jax.ShapeDtypeStruct((FOX_HEADS, TOKENS), F32),
        ],
        scratch_shapes=[pltpu.VMEM((SSM_WIDTH // LANES, tm, LANES), F32)],
        compiler_params=_params("parallel"),
        name="inproj",
    )(x, w_tok, w_vt, wff_t, cos, sin)


def _fgate_kernel(fft_ref, bias_ref, c_ref):
    z = fft_ref[...] + bias_ref[...]
    c = jnp.minimum(z, 0.0) - jnp.log1p(jnp.exp(-jnp.abs(z)))
    lane = lax.broadcasted_iota(jnp.int32, c.shape, 1)
    shift = 1
    while shift < SEQ:
        c = c + jnp.where(lane >= shift, pltpu.roll(c, shift, 1), 0.0)
        shift *= 2
    c = c * LOG2E
    padded = jnp.concatenate([c, jnp.zeros((LANES - FOX_HEADS, SEQ), F32)], axis=0)
    c_ref[...] = padded.T


def _fgate(fft, bias):
    return pl.pallas_call(
        _fgate_kernel,
        grid=(BATCH,),
        in_specs=[
            pl.BlockSpec((FOX_HEADS, SEQ), lambda b: (0, b)),
            pl.BlockSpec((FOX_HEADS, 1), lambda b: (0, 0)),
        ],
        out_specs=pl.BlockSpec((SEQ, LANES), lambda b: (b, 0)),
        out_shape=jax.ShapeDtypeStruct((TOKENS, LANES), F32),
        compiler_params=_params("parallel"),
        name="fgate",
    )(fft, bias)


ONES_ROWS = 16


def _online_softmax(s, m_ref, idx):
    m_old = m_ref[idx]
    m_new = jnp.maximum(m_old, jnp.max(s, axis=0, keepdims=True))
    m_ref[idx] = m_new
    return jnp.exp2(s - m_new).astype(BF16), jnp.exp2(m_old - m_new)


def _causal_mask_t(n):
    key = lax.broadcasted_iota(jnp.int32, (n, n), 0)
    query = lax.broadcasted_iota(jnp.int32, (n, n), 1)
    return key <= query


def _init_softmax_state(m_ref, acc_ref):
    m_ref[...] = jnp.full(m_ref.shape, NEG_INF, F32)
    acc_ref[...] = jnp.zeros(acc_ref.shape, F32)


def _with_ones(vt):
    return jnp.concatenate([vt, jnp.ones((ONES_ROWS, vt.shape[1]), BF16)], axis=0)


def _causal_sweep(i, tq, scores_of, consume):
    def keys(j):
        return pl.ds(pl.multiple_of(j * tq, tq), tq)

    def step(blocks):
        scores = [scores_of(k) for k, _ in blocks]
        for (k, masked), s in zip(blocks, scores):
            consume(s, k, masked)

    def pair(j, carry):
        step([(keys(2 * j), False), (keys(2 * j + 1), False)])
        return carry

    lax.fori_loop(0, i >> 1, pair, 0)

    @pl.when((i & 1) == 1)
    def _():
        step([(keys(i - 1), False), (keys(i), True)])

    @pl.when((i & 1) == 0)
    def _():
        step([(keys(i), True)])


BIAS_LANES = 8


def _split3(c):
    hi = c.astype(BF16)
    rest = c - hi.astype(F32)
    mid = rest.astype(BF16)
    lo = (rest - mid.astype(F32)).astype(BF16)
    return jnp.concatenate([hi, mid, lo], axis=1)


def _bias_select(pair, key_side):
    r = lax.broadcasted_iota(jnp.int32, (3 * LANES, 1), 0)
    d = lax.broadcasted_iota(jnp.int32, (1, LANES), 1)
    piece, src = r >> 7, r & (LANES - 1)
    shift = 3 if key_side else 0
    first = jnp.logical_and(src == 2 * pair, d == piece + shift)
    second = jnp.logical_and(src == 2 * pair + 1, d == piece + shift + BIAS_LANES)
    value = -1.0 if key_side else 1.0
    return jnp.where(jnp.logical_or(first, second), value, 0.0).astype(BF16)


def _bias_ones(key_side):
    d = lax.broadcasted_iota(jnp.int32, (1, LANES), 1)
    within = d & (BIAS_LANES - 1)
    hit = (within < 3) if key_side else jnp.logical_and(within >= 3, within < 6)
    return jnp.where(jnp.logical_and(hit, d < 2 * BIAS_LANES), 1.0, 0.0)


def _fox_kernel(q_ref, k_ref, vt_ref, c_ref, o_ref, m_ref, acc_ref, kb_ref):
    i = pl.program_id(1)
    tq = q_ref.shape[0]
    hd = FOX_HEAD_DIM
    npair = FOX_HEADS // 2
    lane = lax.broadcasted_iota(jnp.int32, (tq, LANES), 1)
    lo = lane < hd
    lo_bias = lane < BIAS_LANES
    _init_softmax_state(m_ref, acc_ref)

    @pl.when(i == 0)
    def _():
        pieces = _split3(c_ref[...])
        for hp in range(npair):
            kb_ref[hp] = (_dot(pieces, _bias_select(hp, True)) + _bias_ones(True)).astype(BF16)

    q_pieces = _split3(c_ref[pl.ds(pl.multiple_of(i * tq, tq), tq), :])
    q_ops = []
    for hp in range(npair):
        q = q_ref[:, hp * LANES:(hp + 1) * LANES]
        zero = jnp.zeros_like(q)
        qb = (_dot(q_pieces, _bias_select(hp, False)) + _bias_ones(False)).astype(BF16)
        q_ops.append(jnp.concatenate([jnp.where(lo, q, zero), jnp.where(lo_bias, qb, zero)], axis=1))
        q_ops.append(jnp.concatenate([jnp.where(lo, zero, q), jnp.where(lo_bias, zero, qb)], axis=1))

    def scores_of(keys):
        out = []
        for hp in range(npair):
            k_op = jnp.concatenate([k_ref[keys, hp * LANES:(hp + 1) * LANES], kb_ref[hp, keys, :]], axis=1)
            out.append(_dot_nt(k_op, q_ops[2 * hp]))
            out.append(_dot_nt(k_op, q_ops[2 * hp + 1]))
        return out

    def consume(scores, keys, masked):
        probs = []
        for h in range(FOX_HEADS):
            s = scores[h]
            if masked:
                s = jnp.where(_causal_mask_t(tq), s, NEG_INF)
            probs.append(_online_softmax(s, m_ref, h))
        for h in range(FOX_HEADS):
            p, alpha = probs[h]
            pv = _dot(_with_ones(vt_ref[h * hd:(h + 1) * hd, keys]), p)
            acc_ref[h] = acc_ref[h] * alpha + pv

    _causal_sweep(i, tq, scores_of, consume)
    for hp in range(npair):
        heads = [acc_ref[h, :hd, :] / acc_ref[h, hd:hd + 1, :] for h in (2 * hp, 2 * hp + 1)]
        o_ref[:, hp * LANES:(hp + 1) * LANES] = jnp.concatenate(heads, axis=0).T.astype(BF16)


def _fox(proj, vt, c):
    nq = SEQ // TQ
    return pl.pallas_call(
        _fox_kernel,
        grid=(BATCH, nq),
        in_specs=[
            pl.BlockSpec((None, TQ, 512), lambda b, i: (SEG_FQ, b * nq + i, 0)),
            pl.BlockSpec((None, SEQ, 512), lambda b, i: (SEG_FK, b, 0)),
            pl.BlockSpec((None, 512, SEQ), lambda b, i: (VT_FOX, 0, b)),
            pl.BlockSpec((SEQ, LANES), lambda b, i: (b, 0)),
        ],
        out_specs=pl.BlockSpec((TQ, 512), lambda b, i: (b * nq + i, 0)),
        out_shape=jax.ShapeDtypeStruct((TOKENS, 512), BF16),
        scratch_shapes=[pltpu.VMEM((FOX_HEADS, 1, TQ), F32),
                        pltpu.VMEM((FOX_HEADS, FOX_HEAD_DIM + ONES_ROWS, TQ), F32),
                        pltpu.VMEM((FOX_HEADS // 2, SEQ, LANES), BF16)],
        compiler_params=_params("parallel", "arbitrary"),
        name="fox_attn",
    )(proj, proj, vt, c)


def _diff_kernel(lam_init, q_ref, k_ref, vt_ref, lam_ref, g_ref, o_ref, m_ref, acc_ref):
    i = pl.program_id(1)
    tq = q_ref.shape[0]
    dv = LANES
    lane = lax.broadcasted_iota(jnp.int32, (tq, LANES), 1)
    lo = lane < DIFF_HEAD_DIM
    _init_softmax_state(m_ref, acc_ref)

    def scores_of(keys):
        out = []
        for h in range(DIFF_HEADS):
            cols = slice(h * LANES, (h + 1) * LANES)
            q = q_ref[:, cols]
            zero = jnp.zeros_like(q)
            kb = k_ref[keys, cols]
            out.append(_dot_nt(kb, jnp.where(lo, q, zero)))
            out.append(_dot_nt(kb, jnp.where(lo, zero, q)))
        return out

    def consume(scores, keys, masked):
        probs = []
        for idx in range(2 * DIFF_HEADS):
            s = scores[idx]
            if masked:
                s = jnp.where(_causal_mask_t(tq), s, NEG_INF)
            probs.append(_online_softmax(s, m_ref, idx))
        for h in range(DIFF_HEADS):
            vtb = _with_ones(vt_ref[h * dv:(h + 1) * dv, keys])
            for idx in (2 * h, 2 * h + 1):
                p, alpha = probs[idx]
                acc_ref[idx] = acc_ref[idx] * alpha + _dot(vtb, p)

    _causal_sweep(i, tq, scores_of, consume)
    lam_rows = lam_ref[...]
    lam = (jnp.exp(jnp.sum(lam_rows[0:1] * lam_rows[1:2], axis=1, keepdims=True))
           - jnp.exp(jnp.sum(lam_rows[2:3] * lam_rows[3:4], axis=1, keepdims=True)) + lam_init)
    for h in range(DIFF_HEADS):
        cols = slice(h * LANES, (h + 1) * LANES)
        o1 = acc_ref[2 * h, :dv, :] / acc_ref[2 * h, dv:dv + 1, :]
        o2 = acc_ref[2 * h + 1, :dv, :] / acc_ref[2 * h + 1, dv:dv + 1, :]
        o = o1 - lam * o2
        o = o * lax.rsqrt(jnp.mean(o * o, axis=0, keepdims=True) + RMS_EPS)
        o_ref[:, cols] = (o.T * g_ref[:, cols] * (1.0 - lam_init)).astype(BF16)


def _diff(proj, vt, lam, g, lam_init):
    nq = SEQ // TQ
    return pl.pallas_call(
        functools.partial(_diff_kernel, lam_init),
        grid=(BATCH, nq),
        in_specs=[
            pl.BlockSpec((None, TQ, 512), lambda b, i: (SEG_DQ, b * nq + i, 0)),
            pl.BlockSpec((None, SEQ, 512), lambda b, i: (SEG_DK, b, 0)),
            pl.BlockSpec((None, 512, SEQ), lambda b, i: (VT_DIFF, 0, b)),
            pl.BlockSpec((4, DIFF_HEAD_DIM), lambda b, i: (0, 0)),
            pl.BlockSpec((1, 512), lambda b, i: (0, 0)),
        ],
        out_specs=pl.BlockSpec((TQ, 512), lambda b, i: (b * nq + i, 0)),
        out_shape=jax.ShapeDtypeStruct((TOKENS, 512), BF16),
        scratch_shapes=[pltpu.VMEM((2 * DIFF_HEADS, 1, TQ), F32),
                        pltpu.VMEM((2 * DIFF_HEADS, LANES + ONES_ROWS, TQ), F32)],
        compiler_params=_params("parallel", "arbitrary"),
        name="diff_attn",
    )(proj, proj, vt, lam, g)


def _spread(compact, select, row_group, col_group):
    full = _dot(compact, select)
    return jnp.where(row_group == col_group, full, 0.0).astype(BF16)


def _s5_kernel(u_ref, kc_ref, abre_ref, abim_ref, acre_ref, acim_ref, are_ref, aim_ref, y_ref, *state):
    nslab = SSM_SCOLS // LANES
    sre, sim = state[:nslab], state[nslab:]

    def iota(shape, dim):
        return lax.broadcasted_iota(jnp.int32, shape, dim)

    i_tn, j_cols = iota((SSM_TN, 1), 0), iota((1, SSM_COLS), 1)
    sel_tn = jnp.where(jnp.logical_and((i_tn >> 4) == (j_cols >> 7), (i_tn & 15) == (j_cols & 15)),
                       1.0, 0.0).astype(BF16)
    i_p, j_sc = iota((SSM_STATE, 1), 0), iota((1, SSM_SCOLS), 1)
    sel_p = jnp.where(i_p == (j_sc & (SSM_STATE - 1)), 1.0, 0.0).astype(BF16)
    grp_rows_cols = (iota((SSM_COLS, 1), 0) >> 4) & (SSM_GBLK - 1)
    grp_cols_cols = (j_cols >> 4) & (SSM_GBLK - 1)
    grp_rows_state = iota((SSM_SCOLS, 1), 0) >> 6
    grp_cols_state = j_sc >> 6
    kc = kc_ref[...]
    lane_tn = iota(kc.shape, 1)
    am = jnp.concatenate(
        [kc] + [jnp.where(lane_tn >= s * SSM_GROUP, pltpu.roll(kc, s * SSM_GROUP, 1), 0.0)
                for s in range(1, SSM_CHUNK)], axis=0).astype(BF16)
    m = _spread(am, sel_tn, grp_rows_cols, grp_cols_cols)
    bre = _spread(abre_ref[...], sel_p, grp_rows_cols, grp_cols_state)
    bim = _spread(abim_ref[...], sel_p, grp_rows_cols, grp_cols_state)
    cre = _spread(acre_ref[...], sel_tn, grp_rows_state, grp_cols_cols)
    cim = _spread(acim_ref[...], sel_tn, grp_rows_state, grp_cols_cols)

    u = jnp.concatenate([u_ref[s] for s in range(SSM_CHUNK)], axis=1)
    for k in range(0, nslab, 2):
        cols = slice(k * LANES, (k + 2) * LANES)
        for dst, b in ((sre, bre), (sim, bim)):
            part = _dot(u, b[:, cols])
            dst[k][...] = part[:, :LANES]
            dst[k + 1][...] = part[:, LANES:]
    ar = [are_ref[:, k * LANES:(k + 1) * LANES] for k in range(nslab)]
    ai = [aim_ref[:, k * LANES:(k + 1) * LANES] for k in range(nslab)]

    def body(c, carry):
        rows = pl.ds(c, BATCH, stride=SSM_NCHUNK)
        nxt = []
        for k in range(nslab):
            xr, xi = carry[k]
            sr = sre[k][rows, :]
            si = sim[k][rows, :]
            sre[k][rows, :] = xr
            sim[k][rows, :] = xi
            nxt.append((ar[k] * xr - ai[k] * xi + sr, ar[k] * xi + ai[k] * xr + si))
        return tuple(nxt)

    zero = jnp.zeros((BATCH, LANES), F32)
    lax.fori_loop(0, SSM_NCHUNK, body, tuple((zero, zero) for _ in range(nslab)))
    xr = jnp.concatenate([r[...] for r in sre], axis=1).astype(BF16)
    xi = jnp.concatenate([r[...] for r in sim], axis=1).astype(BF16)
    for c in range(SSM_COLS // (2 * LANES)):
        cols = slice(c * 2 * LANES, (c + 1) * 2 * LANES)
        y = _dot(u, m[:, cols]) + _dot(xr, cre[:, cols]) + _dot(xi, cim[:, cols])
        y_ref[2 * c] = y[:, :LANES].astype(BF16)
        y_ref[2 * c + 1] = y[:, LANES:].astype(BF16)


def _s5_scan(u_stack, mats, layer):
    am, abre, abim, acre, acim, are, aim = mats

    def spec(*shape):
        return pl.BlockSpec((None, None) + shape, lambda g: (layer, g) + (0,) * len(shape))

    io_spec = pl.BlockSpec((SSM_CHUNK, SSM_ROWS, LANES), lambda g: (0, 0, g))
    return pl.pallas_call(
        _s5_kernel,
        grid=(SSM_NGBLK,),
        in_specs=[io_spec, spec(LANES, SSM_TN),
                  spec(SSM_COLS, SSM_STATE), spec(SSM_COLS, SSM_STATE),
                  spec(SSM_SCOLS, SSM_TN), spec(SSM_SCOLS, SSM_TN),
                  spec(1, SSM_SCOLS), spec(1, SSM_SCOLS)],
        out_specs=io_spec,
        out_shape=jax.ShapeDtypeStruct((SSM_CHUNK, SSM_ROWS, SSM_WIDTH), BF16),
        scratch_shapes=[pltpu.VMEM((SSM_ROWS, LANES), F32) for _ in range(2 * SSM_SCOLS // LANES)],
        compiler_params=_params("parallel"),
        name="s5_scan",
    )(u_stack, am, abre, abim, acre, acim, are, aim)


def _s5_matrices(lam_re, lam_im, log_dt, b_re, b_im, c_re, c_im):
    hi = lax.Precision.HIGHEST
    L, NB, GB = SSM_CHUNK, SSM_NGBLK, SSM_GBLK
    lr = lam_re.astype(F32)
    li = lam_im.astype(F32)
    dt = jnp.exp(log_dt.astype(F32))[:, None]
    mag = jnp.exp(lr * dt)
    ang = li * dt
    ar = mag * jnp.cos(ang)
    ai = mag * jnp.sin(ang)
    er = ar - 1.0
    ei = ai
    den = lr * lr + li * li
    qr = (er * lr + ei * li) / den
    qi = (ei * lr - er * li) / den
    br = b_re.astype(F32)
    bi = b_im.astype(F32)
    bbr = qr[:, :, None] * br - qi[:, :, None] * bi
    bbi = qr[:, :, None] * bi + qi[:, :, None] * br
    k = jnp.arange(L + 1, dtype=F32)[:, None, None]
    pmag = jnp.exp(lr[None] * dt[None] * k)
    pr = pmag * jnp.cos(ang[None] * k)
    pi = pmag * jnp.sin(ang[None] * k)
    cr = c_re.astype(F32)
    ci = c_im.astype(F32)
    e_r = cr[None] * pr[:, :, None, :] - ci[None] * pi[:, :, None, :]
    e_i = cr[None] * pi[:, :, None, :] + ci[None] * pr[:, :, None, :]
    kern = jnp.einsum('kgnq,gqm->kgnm', jnp.concatenate([e_r[:L], -e_i[:L]], axis=-1),
                      jnp.concatenate([bbr, bbi], axis=1), precision=hi)
    kc = jnp.transpose(kern.reshape(L, NB, GB, SSM_GROUP, SSM_GROUP), (1, 2, 4, 0, 3))
    kc = kc.reshape(NB, LANES, SSM_TN)
    krev = (L - 1) - jnp.arange(L, dtype=F32)[:, None, None]
    rmag = jnp.exp(lr[None] * dt[None] * krev)
    rev_r = rmag * jnp.cos(ang[None] * krev)
    rev_i = rmag * jnp.sin(ang[None] * krev)
    sb_r = rev_r[:, :, :, None] * bbr[None] - rev_i[:, :, :, None] * bbi[None]
    sb_i = rev_r[:, :, :, None] * bbi[None] + rev_i[:, :, :, None] * bbr[None]

    def state_in(t):
        t = t.reshape(L, NB, GB, SSM_STATE, SSM_GROUP)
        return jnp.transpose(t, (1, 0, 2, 4, 3)).reshape(NB, SSM_COLS, SSM_STATE)

    def state_out(t):
        t = t.reshape(L, NB, GB, SSM_GROUP, SSM_STATE)
        return jnp.transpose(t, (1, 2, 4, 0, 3)).reshape(NB, SSM_SCOLS, SSM_TN)

    are = pr[L].reshape(NB, 1, SSM_SCOLS)
    aim = pi[L].reshape(NB, 1, SSM_SCOLS)
    return (kc, state_in(sb_r).astype(BF16), state_in(sb_i).astype(BF16),
            state_out(e_r[1:]).astype(BF16), (-state_out(e_i[1:])).astype(BF16), are, aim)


def _s5_post_kernel(y_ref, u_ref, d_ref, w_ref, o_ref, tmp_ref):
    nrow = y_ref.shape[1]
    ycat = jnp.concatenate([y_ref[s] for s in range(SSM_CHUNK)], axis=0).astype(F32)
    ucat = jnp.concatenate([u_ref[s] for s in range(SSM_CHUNK)], axis=0).astype(F32)
    y = jax.nn.gelu(ycat + d_ref[...] * ucat, approximate=True)
    z = _dot(y.astype(BF16), w_ref[...])
    out = z[:, :SSM_WIDTH] * jax.nn.sigmoid(z[:, SSM_WIDTH:])
    for s in range(SSM_CHUNK):
        for c in range(SSM_WIDTH // LANES):
            tmp_ref[c, pl.ds(s, nrow, stride=SSM_CHUNK), :] = (
                out[s * nrow:(s + 1) * nrow, c * LANES:(c + 1) * LANES])
    for c in range(SSM_WIDTH // LANES):
        o_ref[:, c * LANES:(c + 1) * LANES] = tmp_ref[c].astype(BF16)


def _s5_post(y_stack, u_stack, d, w_glu, layer):
    tm = TM_POST
    stack_spec = pl.BlockSpec((SSM_CHUNK, tm // SSM_CHUNK, SSM_WIDTH), lambda i: (0, i, 0))
    return pl.pallas_call(
        _s5_post_kernel,
        grid=(TOKENS // tm,),
        in_specs=[
            stack_spec,
            stack_spec,
            pl.BlockSpec((1, SSM_WIDTH), lambda i: (0, 0)),
            pl.BlockSpec((None, SSM_WIDTH, 2 * SSM_WIDTH), lambda i: (layer, 0, 0)),
        ],
        out_specs=pl.BlockSpec((tm, SSM_WIDTH), lambda i: (i, 0)),
        out_shape=jax.ShapeDtypeStruct((TOKENS, SSM_WIDTH), BF16),
        scratch_shapes=[pltpu.VMEM((SSM_WIDTH // LANES, tm, LANES), F32)],
        compiler_params=_params("parallel"),
        name="s5_post",
    )(y_stack, u_stack, d, w_glu)


def _route(x, w_ref, b_ref):
    xh = x.astype(BF16)
    xl = (x - xh.astype(F32)).astype(BF16)
    hw = _dot(xh, w_ref[...])
    logits = hw[:, :LANES] + (hw[:, LANES:] + _dot(xl, w_ref[:, :LANES])) + b_ref[...]
    lane = lax.broadcasted_iota(jnp.int32, logits.shape, 1).astype(F32)
    big = jnp.float32(LANES)
    neg = jnp.float32(-jnp.inf)

    def first_argmax(v, vmax):
        return jnp.min(jnp.where(v == vmax, lane, big), axis=1, keepdims=True)

    gl = jnp.where(lane < N_EXPERT_GROUPS, logits, neg)
    ge = jnp.exp(gl - jnp.max(gl, axis=1, keepdims=True))
    pg = ge / jnp.sum(ge, axis=1, keepdims=True)
    gp = jnp.max(pg, axis=1, keepdims=True)
    gi = first_argmax(pg, gp)
    base = N_EXPERT_GROUPS + EXPERTS_PER_GROUP * gi
    sel = jnp.logical_and(lane >= base, lane < base + EXPERTS_PER_GROUP)
    el = jnp.where(sel, logits, neg)
    ev1 = jnp.max(el, axis=1, keepdims=True)
    i1 = first_argmax(el, ev1)
    el2 = jnp.where(lane == i1, neg, el)
    ev2 = jnp.max(el2, axis=1, keepdims=True)
    i2 = first_argmax(el2, ev2)
    e2 = jnp.exp(ev2 - ev1)
    den = 1.0 + e2
    w1 = gp * (1.0 / den)
    w2 = gp * (e2 / den)
    return (jnp.where(lane == i1, w1, 0.0) + jnp.where(lane == i2, w2, 0.0)
            + jnp.where(lane == 0.0, gi, 0.0))


def _router_kernel(x_ref, w_ref, b_ref, o_ref):
    o_ref[...] = _route(x_ref[...], w_ref, b_ref)


def _router(x, w_r, b_r, layer):
    tm = TM_ROUTE
    return pl.pallas_call(
        _router_kernel,
        grid=(TOKENS // tm,),
        in_specs=[
            pl.BlockSpec((tm, D_MODEL), lambda i: (i, 0)),
            pl.BlockSpec((None, D_MODEL, 2 * LANES), lambda i: (layer, 0, 0)),
            pl.BlockSpec((None, 1, LANES), lambda i: (layer, 0, 0)),
        ],
        out_specs=pl.BlockSpec((tm, LANES), lambda i: (i, 0)),
        out_shape=jax.ShapeDtypeStruct((TOKENS, LANES), F32),
        compiler_params=_params("parallel"),
        name="router",
    )(x, w_r, b_r)


def _merge_kernel(x_ref, ys_ref, yd_ref, yf_ref, wgl_ref, wbr_ref, wout_ref, g_ref, b_ref, o_ref):
    x = x_ref[...]
    xb = x.astype(BF16)
    merged = None
    for n, y_ref in enumerate((ys_ref, yd_ref, yf_ref)):
        gate = jax.nn.sigmoid(_dot(xb, wgl_ref[:, n * D_MODEL:(n + 1) * D_MODEL]))
        term = gate * _dot(y_ref[...], wbr_ref[n])
        merged = term if merged is None else merged + term
    mix = _dot(merged.astype(BF16), wout_ref[...])
    o_ref[...] = _layer_norm(ALPHA * x + mix, g_ref[...], b_ref[...])


def _merge(x, ys, yd, yf, wgl, wbr, wout, g, b, layer):
    tm = TM_MERGE
    row = lambda i: (i, 0)
    const2 = lambda i: (0, 0)
    per_layer = lambda i: (layer, 0, 0)
    return pl.pallas_call(
        _merge_kernel,
        grid=(TOKENS // tm,),
        in_specs=[
            pl.BlockSpec((tm, D_MODEL), row),
            pl.BlockSpec((tm, 512), row),
            pl.BlockSpec((tm, 512), row),
            pl.BlockSpec((tm, 512), row),
            pl.BlockSpec((None, D_MODEL, N_BRANCH * D_MODEL), per_layer),
            pl.BlockSpec((None, N_BRANCH, BRANCH_WIDTH, D_MODEL), lambda i: (layer, 0, 0, 0)),
            pl.BlockSpec((None, D_MODEL, D_MODEL), per_layer),
            pl.BlockSpec((1, D_MODEL), const2),
            pl.BlockSpec((1, D_MODEL), const2),
        ],
        out_specs=pl.BlockSpec((tm, D_MODEL), row),
        out_shape=jax.ShapeDtypeStruct((TOKENS, D_MODEL), F32),
        compiler_params=_params("parallel"),
        name="merge_ln",
    )(x, ys, yd, yf, wgl, wbr, wout, g, b)


MOE_SUB = 128


def _one_hot(cond):
    return jnp.where(cond, 1.0, 0.0).astype(BF16)


def _moe_sort(x_ref, gate_ref, xs_ref, gs_ref, ys_ref, pos_ref, off_ref):
    tm = x_ref.shape[0]
    gates = gate_ref[...]
    lane = lax.broadcasted_iota(jnp.int32, (1, LANES), 1)
    t_col = lax.broadcasted_iota(jnp.int32, (tm, 1), 0)
    t_row = lax.broadcasted_iota(jnp.int32, (1, tm), 1)
    member = jnp.where(gates[:, 0:1] == lane.astype(F32), 1.0, 0.0)
    member_b = member.astype(BF16)
    counts = jnp.sum(member, axis=0, keepdims=True)
    cnt = [jnp.sum(jnp.where(lane == k, counts, 0.0), axis=1, keepdims=True)
           for k in range(N_EXPERT_GROUPS - 1)]
    start = [cnt[0], cnt[0] + cnt[1], cnt[0] + cnt[1] + cnt[2]]
    off_ref[0] = 0
    for k in range(N_EXPERT_GROUPS - 1):
        off_ref[k + 1] = start[k][0, 0].astype(jnp.int32)
    off_ref[N_EXPERT_GROUPS] = tm

    def offsets(index):
        return sum(jnp.where(index > k, cnt[k], 0.0) for k in range(N_EXPERT_GROUPS - 1))

    earlier = _one_hot(t_row < t_col)
    rank_col = _dot(earlier, member_b)
    pos_col = jnp.sum(member * (offsets(lane) + rank_col), axis=1, keepdims=True)
    pos_ref[...] = jnp.broadcast_to(pos_col, (tm, LANES))
    pos_hi = jnp.floor(pos_col * (1.0 / 32.0))
    pos_lo = pos_col - 32.0 * pos_hi
    digits = jnp.where(lane == 0, pos_hi, jnp.where(lane == 1, pos_lo, 0.0)).astype(BF16)
    pick = _one_hot(lax.broadcasted_iota(jnp.int32, (8, LANES), 0)
                    == lax.broadcasted_iota(jnp.int32, (8, LANES), 1))
    digits_t = _dot_nt(pick, digits)
    pos_row = 32.0 * digits_t[0:1, :] + digits_t[1:2, :]
    perm = _one_hot(t_col.astype(F32) == pos_row)
    xs_ref[...] = _dot(perm, x_ref[...].astype(BF16)).astype(BF16)
    g_hi = gates.astype(BF16)
    rest = gates - g_hi.astype(F32)
    g_mid = rest.astype(BF16)
    g_lo = (rest - g_mid.astype(F32)).astype(BF16)
    moved = _dot(perm, jnp.concatenate([g_hi, g_mid, g_lo], axis=1))
    gs_ref[...] = moved[:, :LANES] + moved[:, LANES:2 * LANES] + moved[:, 2 * LANES:]
    ys_ref[...] = jnp.zeros(ys_ref.shape, F32)


def _moe_kernel(x_ref, gate_ref, wg_ref, wu_ref, wd_ref, g_ref, b_ref, o_ref,
                xs_ref, gs_ref, ys_ref, pos_ref, off_ref):
    grp = pl.program_id(1)
    tm = x_ref.shape[0]

    @pl.when(grp == 0)
    def _():
        _moe_sort(x_ref, gate_ref, xs_ref, gs_ref, ys_ref, pos_ref, off_ref)

    first = N_EXPERT_GROUPS + EXPERTS_PER_GROUP * grp
    shift = MOE_SUB.bit_length() - 1

    def sub_tile(k, carry):
        rows = pl.ds(pl.multiple_of(k * MOE_SUB, MOE_SUB), MOE_SUB)
        xb = xs_ref[rows, :]
        gates = gs_ref[rows, :]
        lane = lax.broadcasted_iota(jnp.int32, gates.shape, 1)
        hidden = []
        for e in range(EXPERTS_PER_GROUP):
            gate = jnp.sum(jnp.where(lane == first + e, gates, 0.0), axis=1, keepdims=True)
            he = jax.nn.silu(_dot(xb, wg_ref[e])) * _dot(xb, wu_ref[e]) * gate
            hidden.append(he.astype(BF16))
        ys_ref[rows, :] += _dot(jnp.concatenate(hidden, axis=1), wd_ref[...])
        return carry

    lo = off_ref[grp] >> shift
    hi = (off_ref[grp + 1] + (MOE_SUB - 1)) >> shift
    lax.fori_loop(lo, hi, sub_tile, 0)

    @pl.when(grp == N_EXPERT_GROUPS - 1)
    def _():
        slot = lax.broadcasted_iota(jnp.int32, (1, tm), 1).astype(F32)
        unsort = _one_hot(pos_ref[:, 0:1] == slot)
        y = _dot(unsort, ys_ref[...].astype(BF16))
        o_ref[...] = _layer_norm(ALPHA * x_ref[...] + y, g_ref[...], b_ref[...])


def _moe(x, gates, wg, wu, wd_grouped, g, b, layer):
    tm = TM_MOE
    return pl.pallas_call(
        _moe_kernel,
        grid=(TOKENS // tm, N_EXPERT_GROUPS),
        in_specs=[
            pl.BlockSpec((tm, D_MODEL), lambda i, e: (i, 0)),
            pl.BlockSpec((tm, LANES), lambda i, e: (i, 0)),
            pl.BlockSpec((None, EXPERTS_PER_GROUP, D_MODEL, D_EXPERT), lambda i, e: (layer, e, 0, 0)),
            pl.BlockSpec((None, EXPERTS_PER_GROUP, D_MODEL, D_EXPERT), lambda i, e: (layer, e, 0, 0)),
            pl.BlockSpec((None, None, EXPERTS_PER_GROUP * D_EXPERT, D_MODEL), lambda i, e: (layer, e, 0, 0)),
            pl.BlockSpec((1, D_MODEL), lambda i, e: (0, 0)),
            pl.BlockSpec((1, D_MODEL), lambda i, e: (0, 0)),
        ],
        out_specs=pl.BlockSpec((tm, D_MODEL), lambda i, e: (i, 0)),
        out_shape=jax.ShapeDtypeStruct((TOKENS, D_MODEL), F32),
        scratch_shapes=[pltpu.VMEM((tm, D_MODEL), BF16),
                        pltpu.VMEM((tm, LANES), F32),
                        pltpu.VMEM((tm, D_MODEL), F32),
                        pltpu.VMEM((tm, LANES), F32),
                        pltpu.SMEM((8,), jnp.int32)],
        compiler_params=_params("parallel", "arbitrary"),
        name="moe_ln",
    )(x, gates, wg, wu, wd_grouped, g, b)


def _rope_tables():
    pos = jnp.arange(SEQ, dtype=F32)
    inv_freq = ROPE_THETA ** (-jnp.arange(0, DIFF_HEAD_DIM, 2, dtype=F32) / DIFF_HEAD_DIM)
    ang = pos[:, None] * inv_freq[None, :]
    emb = jnp.concatenate([ang, ang], axis=-1)
    cos = jnp.cos(emb)
    sin = jnp.sin(emb)
    sign = jnp.where(jnp.arange(DIFF_HEAD_DIM) < DIFF_HEAD_DIM // 2, -1.0, 1.0).astype(F32)
    return jnp.tile(cos, (1, 2)), jnp.tile(sin * sign, (1, 2))


def _split_w_in(w_in):
    qscale = DIFF_HEAD_DIM ** -0.5 * LOG2E

    def seg(k):
        return w_in[:, :, k * 512:(k + 1) * 512]

    w_tok = jnp.stack([seg(0), seg(1) * qscale, seg(2), seg(4) * qscale, seg(5)], axis=1).astype(BF16)
    w_vt = jnp.stack([jnp.swapaxes(seg(3), 1, 2), jnp.swapaxes(seg(6), 1, 2)], axis=1).astype(BF16)
    off = 7 * 512
    wff_t = jnp.swapaxes(w_in[:, :, off:off + FOX_HEADS], 1, 2).astype(BF16)
    wgl = w_in[:, :, off + FOX_HEADS:].astype(BF16)
    return w_tok, w_vt, wff_t, wgl


def kernel(x, w_in, w_branch, w_out, ssm_lambda_re, ssm_lambda_im, ssm_log_dt, ssm_b_re, ssm_b_im,
           ssm_c_re, ssm_c_im, ssm_d, ssm_w_glu, diff_lambda, diff_norm_g, fox_f_bias, ln1_g, ln1_b,
           moe_w_group, moe_b_group, moe_w_expert, moe_b_expert, moe_w_gate, moe_w_up, moe_w_down,
           ln2_g, ln2_b):
    cos, sin = _rope_tables()
    w_tok, w_vt, wff_t, wgl = _split_w_in(w_in)
    s5_mats = jax.vmap(_s5_matrices)(ssm_lambda_re, ssm_lambda_im, ssm_log_dt, ssm_b_re, ssm_b_im,
                                     ssm_c_re, ssm_c_im)
    w_glu = ssm_w_glu.astype(BF16)
    w_br = w_branch.astype(BF16)
    w_o = w_out.astype(BF16)
    w_gate, w_up = moe_w_gate.astype(BF16), moe_w_up.astype(BF16)
    w_down = moe_w_down.astype(BF16).reshape(DEPTH, N_EXPERT_GROUPS, EXPERTS_PER_GROUP * D_EXPERT, D_MODEL)
    w_r = jnp.concatenate([moe_w_group, moe_w_expert], axis=2).astype(F32)
    w_r = jnp.pad(w_r, ((0, 0), (0, 0), (0, LANES - w_r.shape[2])))
    w_r_hi = w_r.astype(BF16)
    w_r = jnp.concatenate([w_r_hi, (w_r - w_r_hi.astype(F32)).astype(BF16)], axis=2)
    b_r = jnp.concatenate([moe_b_group, moe_b_expert], axis=1).astype(F32)
    b_r = jnp.pad(b_r, ((0, 0), (0, LANES - b_r.shape[1]))).reshape(DEPTH, 1, LANES)

    h = x.reshape(TOKENS, D_MODEL)
    for l in range(DEPTH):
        lam_init = 0.8 - 0.6 * math.exp(-0.3 * l)
        proj, u_stack, vt, fft = _inproj(h, w_tok, w_vt, wff_t, cos, sin, l)

        y_stack = _s5_scan(u_stack, s5_mats, l)
        y_ssm = _s5_post(y_stack, u_stack, ssm_d[l].reshape(1, SSM_WIDTH).astype(F32), w_glu, l)

        y_diff = _diff(proj, vt, diff_lambda[l].astype(F32),
                       diff_norm_g[l].reshape(1, 512).astype(F32), lam_init)
        c = _fgate(fft, fox_f_bias[l].reshape(FOX_HEADS, 1).astype(F32))
        y_fox = _fox(proj, vt, c)

        h = _merge(h, y_ssm, y_diff, y_fox, wgl, w_br, w_o,
                   ln1_g[l].reshape(1, D_MODEL), ln1_b[l].reshape(1, D_MODEL), l)

        gates = _router(h, w_r, b_r, l)
        h = _moe(h, gates, w_gate, w_up, w_down,
                 ln2_g[l].reshape(1, D_MODEL), ln2_b[l].reshape(1, D_MODEL), l)
    return h.reshape(BATCH, SEQ, D_MODEL)
```

```python
import functools
import math

import jax
import jax.numpy as jnp
from jax import lax
from jax.experimental import pallas as pl
from jax.experimental.pallas import tpu as pltpu

F32 = jnp.float32
BF16 = jnp.bfloat16

D_MODEL = 1024
BATCH = 8
SEQ = 2048
DEPTH = 4
TOKENS = BATCH * SEQ

SSM_WIDTH = 512
SSM_GROUP = 16
SSM_GROUPS = 32
SSM_STATE = 64
SSM_CHUNK = 8
SSM_NCHUNK = SEQ // SSM_CHUNK
SSM_ROWS = BATCH * SSM_NCHUNK
SSM_GBLK = 8
SSM_NGBLK = SSM_GROUPS // SSM_GBLK
SSM_COLS = SSM_CHUNK * 128
SSM_SCOLS = SSM_GBLK * SSM_STATE
SSM_TN = SSM_CHUNK * SSM_GROUP

DIFF_HEADS = 4
DIFF_HEAD_DIM = 64
FOX_HEADS = 8
FOX_HEAD_DIM = 64
BRANCH_WIDTH = 512
N_BRANCH = 3
ROPE_THETA = 10000.0

N_EXPERT_GROUPS = 4
EXPERTS_PER_GROUP = 4
N_EXPERTS = 16
D_EXPERT = 256

ALPHA = (2 * DEPTH) ** 0.25
LN_EPS = 1e-5
RMS_EPS = 1e-6
NEG_INF = -1e30
LOG2E = math.log2(math.e)

LANES = 128
N_SEG = 4
SEG_DQ, SEG_DK, SEG_FQ, SEG_FK = range(N_SEG)
VT_DIFF, VT_FOX = 0, 1
VMEM_LIMIT = 56 * 1024 * 1024

TM_PROJ = 1024
TQ = 256
TM_POST = 512
TM_MERGE = 512
TM_MOE = 1024


def _params(*sem):
    return pltpu.CompilerParams(dimension_semantics=sem, vmem_limit_bytes=VMEM_LIMIT)


def _dot(a, b):
    return jnp.dot(a, b, preferred_element_type=F32)


def _dot_nt(a, b):
    return lax.dot_general(a, b, (((1,), (1,)), ((), ())), preferred_element_type=F32)


def _layer_norm(y, g, b):
    mu = jnp.mean(y, axis=-1, keepdims=True)
    d = y - mu
    var = jnp.mean(d * d, axis=-1, keepdims=True)
    return d * lax.rsqrt(var + LN_EPS) * g + b


def _inproj_kernel(x_ref, w_ref, wvt_ref, wff_ref, cos_ref, sin_ref, o_ref, u_ref, vt_ref, fft_ref,
                   tmp_ref):
    tm = x_ref.shape[0]
    xb = x_ref[...].astype(BF16)
    cos = cos_ref[...]
    sin = sin_ref[...]
    lane = lax.broadcasted_iota(jnp.int32, cos.shape, 1)
    first_half = (lane & 32) == 0
    for seg in range(N_SEG):
        acc = _dot(xb, w_ref[seg + 1])
        if seg in (SEG_DQ, SEG_DK):
            for c in range(512 // LANES):
                t = acc[:, c * LANES:(c + 1) * LANES]
                rot = jnp.where(first_half, pltpu.roll(t, LANES - 32, 1), pltpu.roll(t, 32, 1))
                o_ref[seg, :, c * LANES:(c + 1) * LANES] = (t * cos + rot * sin).astype(BF16)
        else:
            o_ref[seg] = acc.astype(BF16)
    for v in range(2):
        vt_ref[v] = _dot_nt(wvt_ref[v], xb).astype(BF16)
    fft_ref[...] = _dot_nt(wff_ref[...], xb)
    u = _dot(xb, w_ref[0])
    for c in range(SSM_WIDTH // LANES):
        tmp_ref[c] = u[:, c * LANES:(c + 1) * LANES]
    for s in range(SSM_CHUNK):
        for c in range(SSM_WIDTH // LANES):
            u_ref[s, :, c * LANES:(c + 1) * LANES] = (
                tmp_ref[c, pl.ds(s, tm // SSM_CHUNK, stride=SSM_CHUNK), :].astype(BF16))


def _inproj(x, w_tok, w_vt, wff_t, cos, sin, layer):
    tm = TM_PROJ
    nrope = SEQ // tm
    return pl.pallas_call(
        _inproj_kernel,
        grid=(TOKENS // tm,),
        in_specs=[
            pl.BlockSpec((tm, D_MODEL), lambda i: (i, 0)),
            pl.BlockSpec((None, N_SEG + 1, D_MODEL, 512), lambda i: (layer, 0, 0, 0)),
            pl.BlockSpec((None, 2, 512, D_MODEL), lambda i: (layer, 0, 0, 0)),
            pl.BlockSpec((None, FOX_HEADS, D_MODEL), lambda i: (layer, 0, 0)),
            pl.BlockSpec((tm, LANES), lambda i: (i % nrope, 0)),
            pl.BlockSpec((tm, LANES), lambda i: (i % nrope, 0)),
        ],
        out_specs=[
            pl.BlockSpec((N_SEG, tm, 512), lambda i: (0, i, 0)),
            pl.BlockSpec((SSM_CHUNK, tm // SSM_CHUNK, SSM_WIDTH), lambda i: (0, i, 0)),
            pl.BlockSpec((2, 512, tm), lambda i: (0, 0, i)),
            pl.BlockSpec((FOX_HEADS, tm), lambda i: (0, i)),
        ],
        out_shape=[
            jax.ShapeDtypeStruct((N_SEG, TOKENS, 512), BF16),
            jax.ShapeDtypeStruct((SSM_CHUNK, SSM_ROWS, SSM_WIDTH), BF16),
            jax.ShapeDtypeStruct((2, 512, TOKENS), BF16),
            jax.ShapeDtypeStruct((FOX_HEADS, TOKENS), F32),
        ],
        scratch_shapes=[pltpu.VMEM((SSM_WIDTH // LANES, tm, LANES), F32)],
        compiler_params=_params("parallel"),
        name="inproj",
    )(x, w_tok, w_vt, wff_t, cos, sin)


def _fgate_kernel(fft_ref, bias_ref, c_ref):
    z = fft_ref[...] + bias_ref[...]
    c = jnp.minimum(z, 0.0) - jnp.log1p(jnp.exp(-jnp.abs(z)))
    lane = lax.broadcasted_iota(jnp.int32, c.shape, 1)
    shift = 1
    while shift < SEQ:
        c = c + jnp.where(lane >= shift, pltpu.roll(c, shift, 1), 0.0)
        shift *= 2
    c = c * LOG2E
    padded = jnp.concatenate([c, jnp.zeros((LANES - FOX_HEADS, SEQ), F32)], axis=0)
    c_ref[...] = padded.T


def _fgate(fft, bias):
    return pl.pallas_call(
        _fgate_kernel,
        grid=(BATCH,),
        in_specs=[
            pl.BlockSpec((FOX_HEADS, SEQ), lambda b: (0, b)),
            pl.BlockSpec((FOX_HEADS, 1), lambda b: (0, 0)),
        ],
        out_specs=pl.BlockSpec((SEQ, LANES), lambda b: (b, 0)),
        out_shape=jax.ShapeDtypeStruct((TOKENS, LANES), F32),
        compiler_params=_params("parallel"),
        name="fgate",
    )(fft, bias)


ONES_ROWS = 16


def _online_softmax(s, m_ref, idx):
    m_old = m_ref[idx]
    m_new = jnp.maximum(m_old, jnp.max(s, axis=0, keepdims=True))
    m_ref[idx] = m_new
    return jnp.exp2(s - m_new).astype(BF16), jnp.exp2(m_old - m_new)


def _causal_mask_t(n):
    key = lax.broadcasted_iota(jnp.int32, (n, n), 0)
    query = lax.broadcasted_iota(jnp.int32, (n, n), 1)
    return key <= query


def _init_softmax_state(m_ref, acc_ref):
    m_ref[...] = jnp.full(m_ref.shape, NEG_INF, F32)
    acc_ref[...] = jnp.zeros(acc_ref.shape, F32)


def _with_ones(vt):
    return jnp.concatenate([vt, jnp.ones((ONES_ROWS, vt.shape[1]), BF16)], axis=0)


def _causal_sweep(i, tq, scores_of, consume):
    def keys(j):
        return pl.ds(pl.multiple_of(j * tq, tq), tq)

    def step(blocks):
        scores = [scores_of(k) for k, _ in blocks]
        for (k, masked), s in zip(blocks, scores):
            consume(s, k, masked)

    def pair(j, carry):
        step([(keys(2 * j), False), (keys(2 * j + 1), False)])
        return carry

    lax.fori_loop(0, i >> 1, pair, 0)

    @pl.when((i & 1) == 1)
    def _():
        step([(keys(i - 1), False), (keys(i), True)])

    @pl.when((i & 1) == 0)
    def _():
        step([(keys(i), True)])


BIAS_LANES = 8


def _split3(c):
    hi = c.astype(BF16)
    rest = c - hi.astype(F32)
    mid = rest.astype(BF16)
    lo = (rest - mid.astype(F32)).astype(BF16)
    return jnp.concatenate([hi, mid, lo], axis=1)


def _bias_select(pair, key_side):
    r = lax.broadcasted_iota(jnp.int32, (3 * LANES, 1), 0)
    d = lax.broadcasted_iota(jnp.int32, (1, LANES), 1)
    piece, src = r >> 7, r & (LANES - 1)
    shift = 3 if key_side else 0
    first = jnp.logical_and(src == 2 * pair, d == piece + shift)
    second = jnp.logical_and(src == 2 * pair + 1, d == piece + shift + BIAS_LANES)
    value = -1.0 if key_side else 1.0
    return jnp.where(jnp.logical_or(first, second), value, 0.0).astype(BF16)


def _bias_ones(key_side):
    d = lax.broadcasted_iota(jnp.int32, (1, LANES), 1)
    within = d & (BIAS_LANES - 1)
    hit = (within < 3) if key_side else jnp.logical_and(within >= 3, within < 6)
    return jnp.where(jnp.logical_and(hit, d < 2 * BIAS_LANES), 1.0, 0.0)


def _fox_kernel(q_ref, k_ref, vt_ref, c_ref, o_ref, m_ref, acc_ref, kb_ref):
    i = pl.program_id(1)
    tq = q_ref.shape[0]
    hd = FOX_HEAD_DIM
    npair = FOX_HEADS // 2
    lane = lax.broadcasted_iota(jnp.int32, (tq, LANES), 1)
    lo = lane < hd
    lo_bias = lane < BIAS_LANES
    _init_softmax_state(m_ref, acc_ref)

    @pl.when(i == 0)
    def _():
        pieces = _split3(c_ref[...])
        for hp in range(npair):
            kb_ref[hp] = (_dot(pieces, _bias_select(hp, True)) + _bias_ones(True)).astype(BF16)

    q_pieces = _split3(c_ref[pl.ds(pl.multiple_of(i * tq, tq), tq), :])
    q_ops = []
    for hp in range(npair):
        q = q_ref[:, hp * LANES:(hp + 1) * LANES]
        zero = jnp.zeros_like(q)
        qb = (_dot(q_pieces, _bias_select(hp, False)) + _bias_ones(False)).astype(BF16)
        q_ops.append(jnp.concatenate([jnp.where(lo, q, zero), jnp.where(lo_bias, qb, zero)], axis=1))
        q_ops.append(jnp.concatenate([jnp.where(lo, zero, q), jnp.where(lo_bias, zero, qb)], axis=1))

    def scores_of(keys):
        out = []
        for hp in range(npair):
            k_op = jnp.concatenate([k_ref[keys, hp * LANES:(hp + 1) * LANES], kb_ref[hp, keys, :]], axis=1)
            out.append(_dot_nt(k_op, q_ops[2 * hp]))
            out.append(_dot_nt(k_op, q_ops[2 * hp + 1]))
        return out

    def consume(scores, keys, masked):
        probs = []
        for h in range(FOX_HEADS):
            s = scores[h]
            if masked:
                s = jnp.where(_causal_mask_t(tq), s, NEG_INF)
            probs.append(_online_softmax(s, m_ref, h))
        for h in range(FOX_HEADS):
            p, alpha = probs[h]
            pv = _dot(_with_ones(vt_ref[h * hd:(h + 1) * hd, keys]), p)
            acc_ref[h] = acc_ref[h] * alpha + pv

    _causal_sweep(i, tq, scores_of, consume)
    for hp in range(npair):
        heads = [acc_ref[h, :hd, :] / acc_ref[h, hd:hd + 1, :] for h in (2 * hp, 2 * hp + 1)]
        o_ref[:, hp * LANES:(hp + 1) * LANES] = jnp.concatenate(heads, axis=0).T.astype(BF16)


def _fox(proj, vt, c):
    nq = SEQ // TQ
    return pl.pallas_call(
        _fox_kernel,
        grid=(BATCH, nq),
        in_specs=[
            pl.BlockSpec((None, TQ, 512), lambda b, i: (SEG_FQ, b * nq + i, 0)),
            pl.BlockSpec((None, SEQ, 512), lambda b, i: (SEG_FK, b, 0)),
            pl.BlockSpec((None, 512, SEQ), lambda b, i: (VT_FOX, 0, b)),
            pl.BlockSpec((SEQ, LANES), lambda b, i: (b, 0)),
        ],
        out_specs=pl.BlockSpec((TQ, 512), lambda b, i: (b * nq + i, 0)),
        out_shape=jax.ShapeDtypeStruct((TOKENS, 512), BF16),
        scratch_shapes=[pltpu.VMEM((FOX_HEADS, 1, TQ), F32),
                        pltpu.VMEM((FOX_HEADS, FOX_HEAD_DIM + ONES_ROWS, TQ), F32),
                        pltpu.VMEM((FOX_HEADS // 2, SEQ, LANES), BF16)],
        compiler_params=_params("parallel", "arbitrary"),
        name="fox_attn",
    )(proj, proj, vt, c)


def _diff_kernel(lam_init, q_ref, k_ref, vt_ref, lam_ref, g_ref, o_ref, m_ref, acc_ref):
    i = pl.program_id(1)
    tq = q_ref.shape[0]
    dv = LANES
    lane = lax.broadcasted_iota(jnp.int32, (tq, LANES), 1)
    lo = lane < DIFF_HEAD_DIM
    _init_softmax_state(m_ref, acc_ref)

    def scores_of(keys):
        out = []
        for h in range(DIFF_HEADS):
            cols = slice(h * LANES, (h + 1) * LANES)
            q = q_ref[:, cols]
            zero = jnp.zeros_like(q)
            kb = k_ref[keys, cols]
            out.append(_dot_nt(kb, jnp.where(lo, q, zero)))
            out.append(_dot_nt(kb, jnp.where(lo, zero, q)))
        return out

    def consume(scores, keys, masked):
        probs = []
        for idx in range(2 * DIFF_HEADS):
            s = scores[idx]
            if masked:
                s = jnp.where(_causal_mask_t(tq), s, NEG_INF)
            probs.append(_online_softmax(s, m_ref, idx))
        for h in range(DIFF_HEADS):
            vtb = _with_ones(vt_ref[h * dv:(h + 1) * dv, keys])
            for idx in (2 * h, 2 * h + 1):
                p, alpha = probs[idx]
                acc_ref[idx] = acc_ref[idx] * alpha + _dot(vtb, p)

    _causal_sweep(i, tq, scores_of, consume)
    lam_rows = lam_ref[...]
    lam = (jnp.exp(jnp.sum(lam_rows[0:1] * lam_rows[1:2], axis=1, keepdims=True))
           - jnp.exp(jnp.sum(lam_rows[2:3] * lam_rows[3:4], axis=1, keepdims=True)) + lam_init)
    for h in range(DIFF_HEADS):
        cols = slice(h * LANES, (h + 1) * LANES)
        o1 = acc_ref[2 * h, :dv, :] / acc_ref[2 * h, dv:dv + 1, :]
        o2 = acc_ref[2 * h + 1, :dv, :] / acc_ref[2 * h + 1, dv:dv + 1, :]
        o = o1 - lam * o2
        o = o * lax.rsqrt(jnp.mean(o * o, axis=0, keepdims=True) + RMS_EPS)
        o_ref[:, cols] = (o.T * g_ref[:, cols] * (1.0 - lam_init)).astype(BF16)


def _diff(proj, vt, lam, g, lam_init):
    nq = SEQ // TQ
    return pl.pallas_call(
        functools.partial(_diff_kernel, lam_init),
        grid=(BATCH, nq),
        in_specs=[
            pl.BlockSpec((None, TQ, 512), lambda b, i: (SEG_DQ, b * nq + i, 0)),
            pl.BlockSpec((None, SEQ, 512), lambda b, i: (SEG_DK, b, 0)),
            pl.BlockSpec((None, 512, SEQ), lambda b, i: (VT_DIFF, 0, b)),
            pl.BlockSpec((4, DIFF_HEAD_DIM), lambda b, i: (0, 0)),
            pl.BlockSpec((1, 512), lambda b, i: (0, 0)),
        ],
        out_specs=pl.BlockSpec((TQ, 512), lambda b, i: (b * nq + i, 0)),
        out_shape=jax.ShapeDtypeStruct((TOKENS, 512), BF16),
        scratch_shapes=[pltpu.VMEM((2 * DIFF_HEADS, 1, TQ), F32),
                        pltpu.VMEM((2 * DIFF_HEADS, LANES + ONES_ROWS, TQ), F32)],
        compiler_params=_params("parallel", "arbitrary"),
        name="diff_attn",
    )(proj, proj, vt, lam, g)


def _spread(compact, select, row_group, col_group):
    full = _dot(compact, select)
    return jnp.where(row_group == col_group, full, 0.0).astype(BF16)


def _s5_kernel(u_ref, kc_ref, abre_ref, abim_ref, acre_ref, acim_ref, are_ref, aim_ref, y_ref, *state):
    nslab = SSM_SCOLS // LANES
    sre, sim = state[:nslab], state[nslab:]

    def iota(shape, dim):
        return lax.broadcasted_iota(jnp.int32, shape, dim)

    i_tn, j_cols = iota((SSM_TN, 1), 0), iota((1, SSM_COLS), 1)
    sel_tn = jnp.where(jnp.logical_and((i_tn >> 4) == (j_cols >> 7), (i_tn & 15) == (j_cols & 15)),
                       1.0, 0.0).astype(BF16)
    i_p, j_sc = iota((SSM_STATE, 1), 0), iota((1, SSM_SCOLS), 1)
    sel_p = jnp.where(i_p == (j_sc & (SSM_STATE - 1)), 1.0, 0.0).astype(BF16)
    grp_rows_cols = (iota((SSM_COLS, 1), 0) >> 4) & (SSM_GBLK - 1)
    grp_cols_cols = (j_cols >> 4) & (SSM_GBLK - 1)
    grp_rows_state = iota((SSM_SCOLS, 1), 0) >> 6
    grp_cols_state = j_sc >> 6
    kc = kc_ref[...]
    lane_tn = iota(kc.shape, 1)
    am = jnp.concatenate(
        [kc] + [jnp.where(lane_tn >= s * SSM_GROUP, pltpu.roll(kc, s * SSM_GROUP, 1), 0.0)
                for s in range(1, SSM_CHUNK)], axis=0).astype(BF16)
    m = _spread(am, sel_tn, grp_rows_cols, grp_cols_cols)
    bre = _spread(abre_ref[...], sel_p, grp_rows_cols, grp_cols_state)
    bim = _spread(abim_ref[...], sel_p, grp_rows_cols, grp_cols_state)
    cre = _spread(acre_ref[...], sel_tn, grp_rows_state, grp_cols_cols)
    cim = _spread(acim_ref[...], sel_tn, grp_rows_state, grp_cols_cols)

    u = jnp.concatenate([u_ref[s] for s in range(SSM_CHUNK)], axis=1)
    for k in range(0, nslab, 2):
        cols = slice(k * LANES, (k + 2) * LANES)
        for dst, b in ((sre, bre), (sim, bim)):
            part = _dot(u, b[:, cols])
            dst[k][...] = part[:, :LANES]
            dst[k + 1][...] = part[:, LANES:]
    ar = [are_ref[:, k * LANES:(k + 1) * LANES] for k in range(nslab)]
    ai = [aim_ref[:, k * LANES:(k + 1) * LANES] for k in range(nslab)]

    def body(c, carry):
        rows = pl.ds(c, BATCH, stride=SSM_NCHUNK)
        nxt = []
        for k in range(nslab):
            xr, xi = carry[k]
            sr = sre[k][rows, :]
            si = sim[k][rows, :]
            sre[k][rows, :] = xr
            sim[k][rows, :] = xi
            nxt.append((ar[k] * xr - ai[k] * xi + sr, ar[k] * xi + ai[k] * xr + si))
        return tuple(nxt)

    zero = jnp.zeros((BATCH, LANES), F32)
    lax.fori_loop(0, SSM_NCHUNK, body, tuple((zero, zero) for _ in range(nslab)))
    xr = jnp.concatenate([r[...] for r in sre], axis=1).astype(BF16)
    xi = jnp.concatenate([r[...] for r in sim], axis=1).astype(BF16)
    for c in range(SSM_COLS // (2 * LANES)):
        cols = slice(c * 2 * LANES, (c + 1) * 2 * LANES)
        y = _dot(u, m[:, cols]) + _dot(xr, cre[:, cols]) + _dot(xi, cim[:, cols])
        y_ref[2 * c] = y[:, :LANES].astype(BF16)
        y_ref[2 * c + 1] = y[:, LANES:].astype(BF16)


def _s5_scan(u_stack, mats, layer):
    am, abre, abim, acre, acim, are, aim = mats

    def spec(*shape):
        return pl.BlockSpec((None, None) + shape, lambda g: (layer, g) + (0,) * len(shape))

    io_spec = pl.BlockSpec((SSM_CHUNK, SSM_ROWS, LANES), lambda g: (0, 0, g))
    return pl.pallas_call(
        _s5_kernel,
        grid=(SSM_NGBLK,),
        in_specs=[io_spec, spec(LANES, SSM_TN),
                  spec(SSM_COLS, SSM_STATE), spec(SSM_COLS, SSM_STATE),
                  spec(SSM_SCOLS, SSM_TN), spec(SSM_SCOLS, SSM_TN),
                  spec(1, SSM_SCOLS), spec(1, SSM_SCOLS)],
        out_specs=io_spec,
        out_shape=jax.ShapeDtypeStruct((SSM_CHUNK, SSM_ROWS, SSM_WIDTH), BF16),
        scratch_shapes=[pltpu.VMEM((SSM_ROWS, LANES), F32) for _ in range(2 * SSM_SCOLS // LANES)],
        compiler_params=_params("parallel"),
        name="s5_scan",
    )(u_stack, am, abre, abim, acre, acim, are, aim)


def _s5_matrices(lam_re, lam_im, log_dt, b_re, b_im, c_re, c_im):
    hi = lax.Precision.HIGHEST
    L, NB, GB = SSM_CHUNK, SSM_NGBLK, SSM_GBLK
    lr = lam_re.astype(F32)
    li = lam_im.astype(F32)
    dt = jnp.exp(log_dt.astype(F32))[:, None]
    mag = jnp.exp(lr * dt)
    ang = li * dt
    ar = mag * jnp.cos(ang)
    ai = mag * jnp.sin(ang)
    er = ar - 1.0
    ei = ai
    den = lr * lr + li * li
    qr = (er * lr + ei * li) / den
    qi = (ei * lr - er * li) / den
    br = b_re.astype(F32)
    bi = b_im.astype(F32)
    bbr = qr[:, :, None] * br - qi[:, :, None] * bi
    bbi = qr[:, :, None] * bi + qi[:, :, None] * br
    k = jnp.arange(L + 1, dtype=F32)[:, None, None]
    pmag = jnp.exp(lr[None] * dt[None] * k)
    pr = pmag * jnp.cos(ang[None] * k)
    pi = pmag * jnp.sin(ang[None] * k)
    cr = c_re.astype(F32)
    ci = c_im.astype(F32)
    e_r = cr[None] * pr[:, :, None, :] - ci[None] * pi[:, :, None, :]
    e_i = cr[None] * pi[:, :, None, :] + ci[None] * pr[:, :, None, :]
    kern = jnp.einsum('kgnq,gqm->kgnm', jnp.concatenate([e_r[:L], -e_i[:L]], axis=-1),
                      jnp.concatenate([bbr, bbi], axis=1), precision=hi)
    kc = jnp.transpose(kern.reshape(L, NB, GB, SSM_GROUP, SSM_GROUP), (1, 2, 4, 0, 3))
    kc = kc.reshape(NB, LANES, SSM_TN)
    krev = (L - 1) - jnp.arange(L, dtype=F32)[:, None, None]
    rmag = jnp.exp(lr[None] * dt[None] * krev)
    rev_r = rmag * jnp.cos(ang[None] * krev)
    rev_i = rmag * jnp.sin(ang[None] * krev)
    sb_r = rev_r[:, :, :, None] * bbr[None] - rev_i[:, :, :, None] * bbi[None]
    sb_i = rev_r[:, :, :, None] * bbi[None] + rev_i[:, :, :, None] * bbr[None]

    def state_in(t):
        t = t.reshape(L, NB, GB, SSM_STATE, SSM_GROUP)
        return jnp.transpose(t, (1, 0, 2, 4, 3)).reshape(NB, SSM_COLS, SSM_STATE)

    def state_out(t):
        t = t.reshape(L, NB, GB, SSM_GROUP, SSM_STATE)
        return jnp.transpose(t, (1, 2, 4, 0, 3)).reshape(NB, SSM_SCOLS, SSM_TN)

    are = pr[L].reshape(NB, 1, SSM_SCOLS)
    aim = pi[L].reshape(NB, 1, SSM_SCOLS)
    return (kc, state_in(sb_r).astype(BF16), state_in(sb_i).astype(BF16),
            state_out(e_r[1:]).astype(BF16), (-state_out(e_i[1:])).astype(BF16), are, aim)


def _s5_post_kernel(y_ref, u_ref, d_ref, w_ref, o_ref, tmp_ref):
    nrow = y_ref.shape[1]
    ycat = jnp.concatenate([y_ref[s] for s in range(SSM_CHUNK)], axis=0).astype(F32)
    ucat = jnp.concatenate([u_ref[s] for s in range(SSM_CHUNK)], axis=0).astype(F32)
    y = jax.nn.gelu(ycat + d_ref[...] * ucat, approximate=True)
    z = _dot(y.astype(BF16), w_ref[...])
    out = z[:, :SSM_WIDTH] * jax.nn.sigmoid(z[:, SSM_WIDTH:])
    for s in range(SSM_CHUNK):
        for c in range(SSM_WIDTH // LANES):
            tmp_ref[c, pl.ds(s, nrow, stride=SSM_CHUNK), :] = (
                out[s * nrow:(s + 1) * nrow, c * LANES:(c + 1) * LANES])
    for c in range(SSM_WIDTH // LANES):
        o_ref[:, c * LANES:(c + 1) * LANES] = tmp_ref[c].astype(BF16)


def _s5_post(y_stack, u_stack, d, w_glu, layer):
    tm = TM_POST
    stack_spec = pl.BlockSpec((SSM_CHUNK, tm // SSM_CHUNK, SSM_WIDTH), lambda i: (0, i, 0))
    return pl.pallas_call(
        _s5_post_kernel,
        grid=(TOKENS // tm,),
        in_specs=[
            stack_spec,
            stack_spec,
            pl.BlockSpec((1, SSM_WIDTH), lambda i: (0, 0)),
            pl.BlockSpec((None, SSM_WIDTH, 2 * SSM_WIDTH), lambda i: (layer, 0, 0)),
        ],
        out_specs=pl.BlockSpec((tm, SSM_WIDTH), lambda i: (i, 0)),
        out_shape=jax.ShapeDtypeStruct((TOKENS, SSM_WIDTH), BF16),
        scratch_shapes=[pltpu.VMEM((SSM_WIDTH // LANES, tm, LANES), F32)],
        compiler_params=_params("parallel"),
        name="s5_post",
    )(y_stack, u_stack, d, w_glu)


def _route(x, w_ref, b_ref):
    xh = x.astype(BF16)
    xl = (x - xh.astype(F32)).astype(BF16)
    hw = _dot(xh, w_ref[...])
    logits = hw[:, :LANES] + (hw[:, LANES:] + _dot(xl, w_ref[:, :LANES])) + b_ref[...]
    lane = lax.broadcasted_iota(jnp.int32, logits.shape, 1).astype(F32)
    big = jnp.float32(LANES)
    neg = jnp.float32(-jnp.inf)

    def first_argmax(v, vmax):
        return jnp.min(jnp.where(v == vmax, lane, big), axis=1, keepdims=True)

    gl = jnp.where(lane < N_EXPERT_GROUPS, logits, neg)
    ge = jnp.exp(gl - jnp.max(gl, axis=1, keepdims=True))
    pg = ge / jnp.sum(ge, axis=1, keepdims=True)
    gp = jnp.max(pg, axis=1, keepdims=True)
    gi = first_argmax(pg, gp)
    base = N_EXPERT_GROUPS + EXPERTS_PER_GROUP * gi
    sel = jnp.logical_and(lane >= base, lane < base + EXPERTS_PER_GROUP)
    el = jnp.where(sel, logits, neg)
    ev1 = jnp.max(el, axis=1, keepdims=True)
    i1 = first_argmax(el, ev1)
    el2 = jnp.where(lane == i1, neg, el)
    ev2 = jnp.max(el2, axis=1, keepdims=True)
    i2 = first_argmax(el2, ev2)
    e2 = jnp.exp(ev2 - ev1)
    den = 1.0 + e2
    w1 = gp * (1.0 / den)
    w2 = gp * (e2 / den)
    return (jnp.where(lane == i1, w1, 0.0) + jnp.where(lane == i2, w2, 0.0)
            + jnp.where(lane == 0.0, gi, 0.0))


def _merge_kernel(x_ref, ys_ref, yd_ref, yf_ref, wgl_ref, wbr_ref, wout_ref, g_ref, b_ref, o_ref):
    x = x_ref[...]
    xb = x.astype(BF16)
    merged = None
    for n, y_ref in enumerate((ys_ref, yd_ref, yf_ref)):
        gate = jax.nn.sigmoid(_dot(xb, wgl_ref[:, n * D_MODEL:(n + 1) * D_MODEL]))
        term = gate * _dot(y_ref[...], wbr_ref[n])
        merged = term if merged is None else merged + term
    mix = _dot(merged.astype(BF16), wout_ref[...])
    o_ref[...] = _layer_norm(ALPHA * x + mix, g_ref[...], b_ref[...])


def _merge(x, ys, yd, yf, wgl, wbr, wout, g, b, layer):
    tm = TM_MERGE
    row = lambda i: (i, 0)
    const2 = lambda i: (0, 0)
    per_layer = lambda i: (layer, 0, 0)
    return pl.pallas_call(
        _merge_kernel,
        grid=(TOKENS // tm,),
        in_specs=[
            pl.BlockSpec((tm, D_MODEL), row),
            pl.BlockSpec((tm, 512), row),
            pl.BlockSpec((tm, 512), row),
            pl.BlockSpec((tm, 512), row),
            pl.BlockSpec((None, D_MODEL, N_BRANCH * D_MODEL), per_layer),
            pl.BlockSpec((None, N_BRANCH, BRANCH_WIDTH, D_MODEL), lambda i: (layer, 0, 0, 0)),
            pl.BlockSpec((None, D_MODEL, D_MODEL), per_layer),
            pl.BlockSpec((1, D_MODEL), const2),
            pl.BlockSpec((1, D_MODEL), const2),
        ],
        out_specs=pl.BlockSpec((tm, D_MODEL), row),
        out_shape=jax.ShapeDtypeStruct((TOKENS, D_MODEL), F32),
        compiler_params=_params("parallel"),
        name="merge_ln",
    )(x, ys, yd, yf, wgl, wbr, wout, g, b)


MOE_SUB = 128


def _one_hot(cond):
    return jnp.where(cond, 1.0, 0.0).astype(BF16)


def _moe_sort(x_ref, gates, xs_ref, gs_ref, ys_ref, pos_ref, off_ref):
    tm = x_ref.shape[0]
    lane = lax.broadcasted_iota(jnp.int32, (1, LANES), 1)
    t_col = lax.broadcasted_iota(jnp.int32, (tm, 1), 0)
    t_row = lax.broadcasted_iota(jnp.int32, (1, tm), 1)
    member = jnp.where(gates[:, 0:1] == lane.astype(F32), 1.0, 0.0)
    member_b = member.astype(BF16)
    counts = jnp.sum(member, axis=0, keepdims=True)
    cnt = [jnp.sum(jnp.where(lane == k, counts, 0.0), axis=1, keepdims=True)
           for k in range(N_EXPERT_GROUPS - 1)]
    start = [cnt[0], cnt[0] + cnt[1], cnt[0] + cnt[1] + cnt[2]]
    off_ref[0] = 0
    for k in range(N_EXPERT_GROUPS - 1):
        off_ref[k + 1] = start[k][0, 0].astype(jnp.int32)
    off_ref[N_EXPERT_GROUPS] = tm

    def offsets(index):
        return sum(jnp.where(index > k, cnt[k], 0.0) for k in range(N_EXPERT_GROUPS - 1))

    earlier = _one_hot(t_row < t_col)
    rank_col = _dot(earlier, member_b)
    pos_col = jnp.sum(member * (offsets(lane) + rank_col), axis=1, keepdims=True)
    pos_ref[...] = jnp.broadcast_to(pos_col, (tm, LANES))
    pos_hi = jnp.floor(pos_col * (1.0 / 32.0))
    pos_lo = pos_col - 32.0 * pos_hi
    digits = jnp.where(lane == 0, pos_hi, jnp.where(lane == 1, pos_lo, 0.0)).astype(BF16)
    pick = _one_hot(lax.broadcasted_iota(jnp.int32, (8, LANES), 0)
                    == lax.broadcasted_iota(jnp.int32, (8, LANES), 1))
    digits_t = _dot_nt(pick, digits)
    pos_row = 32.0 * digits_t[0:1, :] + digits_t[1:2, :]
    perm = _one_hot(t_col.astype(F32) == pos_row)
    xs_ref[...] = _dot(perm, x_ref[...].astype(BF16)).astype(BF16)
    g_hi = gates.astype(BF16)
    rest = gates - g_hi.astype(F32)
    g_mid = rest.astype(BF16)
    g_lo = (rest - g_mid.astype(F32)).astype(BF16)
    moved = _dot(perm, jnp.concatenate([g_hi, g_mid, g_lo], axis=1))
    gs_ref[...] = moved[:, :LANES] + moved[:, LANES:2 * LANES] + moved[:, 2 * LANES:]
    ys_ref[...] = jnp.zeros(ys_ref.shape, F32)


def _moe_kernel(x_ref, wr_ref, br_ref, wg_ref, wu_ref, wd_ref, g_ref, b_ref, o_ref,
                xs_ref, gs_ref, ys_ref, pos_ref, off_ref):
    grp = pl.program_id(1)
    tm = x_ref.shape[0]

    @pl.when(grp == 0)
    def _():
        _moe_sort(x_ref, _route(x_ref[...], wr_ref, br_ref), xs_ref, gs_ref, ys_ref, pos_ref, off_ref)

    first = N_EXPERT_GROUPS + EXPERTS_PER_GROUP * grp
    shift = MOE_SUB.bit_length() - 1

    def sub_tile(k, carry):
        rows = pl.ds(pl.multiple_of(k * MOE_SUB, MOE_SUB), MOE_SUB)
        xb = xs_ref[rows, :]
        gates = gs_ref[rows, :]
        lane = lax.broadcasted_iota(jnp.int32, gates.shape, 1)
        hidden = []
        for e in range(EXPERTS_PER_GROUP):
            gate = jnp.sum(jnp.where(lane == first + e, gates, 0.0), axis=1, keepdims=True)
            he = jax.nn.silu(_dot(xb, wg_ref[e])) * _dot(xb, wu_ref[e]) * gate
            hidden.append(he.astype(BF16))
        ys_ref[rows, :] += _dot(jnp.concatenate(hidden, axis=1), wd_ref[...])
        return carry

    lo = off_ref[grp] >> shift
    hi = (off_ref[grp + 1] + (MOE_SUB - 1)) >> shift
    lax.fori_loop(lo, hi, sub_tile, 0)

    @pl.when(grp == N_EXPERT_GROUPS - 1)
    def _():
        slot = lax.broadcasted_iota(jnp.int32, (1, tm), 1).astype(F32)
        unsort = _one_hot(pos_ref[:, 0:1] == slot)
        y = _dot(unsort, ys_ref[...].astype(BF16))
        o_ref[...] = _layer_norm(ALPHA * x_ref[...] + y, g_ref[...], b_ref[...])


def _moe(x, w_r, b_r, wg, wu, wd_grouped, g, b, layer):
    tm = TM_MOE
    return pl.pallas_call(
        _moe_kernel,
        grid=(TOKENS // tm, N_EXPERT_GROUPS),
        in_specs=[
            pl.BlockSpec((tm, D_MODEL), lambda i, e: (i, 0)),
            pl.BlockSpec((None, D_MODEL, 2 * LANES), lambda i, e: (layer, 0, 0)),
            pl.BlockSpec((None, 1, LANES), lambda i, e: (layer, 0, 0)),
            pl.BlockSpec((None, EXPERTS_PER_GROUP, D_MODEL, D_EXPERT), lambda i, e: (layer, e, 0, 0)),
            pl.BlockSpec((None, EXPERTS_PER_GROUP, D_MODEL, D_EXPERT), lambda i, e: (layer, e, 0, 0)),
            pl.BlockSpec((None, None, EXPERTS_PER_GROUP * D_EXPERT, D_MODEL), lambda i, e: (layer, e, 0, 0)),
            pl.BlockSpec((1, D_MODEL), lambda i, e: (0, 0)),
            pl.BlockSpec((1, D_MODEL), lambda i, e: (0, 0)),
        ],
        out_specs=pl.BlockSpec((tm, D_MODEL), lambda i, e: (i, 0)),
        out_shape=jax.ShapeDtypeStruct((TOKENS, D_MODEL), F32),
        scratch_shapes=[pltpu.VMEM((tm, D_MODEL), BF16),
                        pltpu.VMEM((tm, LANES), F32),
                        pltpu.VMEM((tm, D_MODEL), F32),
                        pltpu.VMEM((tm, LANES), F32),
                        pltpu.SMEM((8,), jnp.int32)],
        compiler_params=_params("parallel", "arbitrary"),
        name="moe_ln",
    )(x, w_r, b_r, wg, wu, wd_grouped, g, b)


def _rope_tables():
    pos = jnp.arange(SEQ, dtype=F32)
    inv_freq = ROPE_THETA ** (-jnp.arange(0, DIFF_HEAD_DIM, 2, dtype=F32) / DIFF_HEAD_DIM)
    ang = pos[:, None] * inv_freq[None, :]
    emb = jnp.concatenate([ang, ang], axis=-1)
    cos = jnp.cos(emb)
    sin = jnp.sin(emb)
    sign = jnp.where(jnp.arange(DIFF_HEAD_DIM) < DIFF_HEAD_DIM // 2, -1.0, 1.0).astype(F32)
    return jnp.tile(cos, (1, 2)), jnp.tile(sin * sign, (1, 2))


def _split_w_in(w_in):
    qscale = DIFF_HEAD_DIM ** -0.5 * LOG2E

    def seg(k):
        return w_in[:, :, k * 512:(k + 1) * 512]

    w_tok = jnp.stack([seg(0), seg(1) * qscale, seg(2), seg(4) * qscale, seg(5)], axis=1).astype(BF16)
    w_vt = jnp.stack([jnp.swapaxes(seg(3), 1, 2), jnp.swapaxes(seg(6), 1, 2)], axis=1).astype(BF16)
    off = 7 * 512
    wff_t = jnp.swapaxes(w_in[:, :, off:off + FOX_HEADS], 1, 2).astype(BF16)
    wgl = w_in[:, :, off + FOX_HEADS:].astype(BF16)
    return w_tok, w_vt, wff_t, wgl


def kernel(x, w_in, w_branch, w_out, ssm_lambda_re, ssm_lambda_im, ssm_log_dt, ssm_b_re, ssm_b_im,
           ssm_c_re, ssm_c_im, ssm_d, ssm_w_glu, diff_lambda, diff_norm_g, fox_f_bias, ln1_g, ln1_b,
           moe_w_group, moe_b_group, moe_w_expert, moe_b_expert, moe_w_gate, moe_w_up, moe_w_down,
           ln2_g, ln2_b):
    cos, sin = _rope_tables()
    w_tok, w_vt, wff_t, wgl = _split_w_in(w_in)
    s5_mats = jax.vmap(_s5_matrices)(ssm_lambda_re, ssm_lambda_im, ssm_log_dt, ssm_b_re, ssm_b_im,
                                     ssm_c_re, ssm_c_im)
    w_glu = ssm_w_glu.astype(BF16)
    w_br = w_branch.astype(BF16)
    w_o = w_out.astype(BF16)
    w_gate, w_up = moe_w_gate.astype(BF16), moe_w_up.astype(BF16)
    w_down = moe_w_down.astype(BF16).reshape(DEPTH, N_EXPERT_GROUPS, EXPERTS_PER_GROUP * D_EXPERT, D_MODEL)
    w_r = jnp.concatenate([moe_w_group, moe_w_expert], axis=2).astype(F32)
    w_r = jnp.pad(w_r, ((0, 0), (0, 0), (0, LANES - w_r.shape[2])))
    w_r_hi = w_r.astype(BF16)
    w_r = jnp.concatenate([w_r_hi, (w_r - w_r_hi.astype(F32)).astype(BF16)], axis=2)
    b_r = jnp.concatenate([moe_b_group, moe_b_expert], axis=1).astype(F32)
    b_r = jnp.pad(b_r, ((0, 0), (0, LANES - b_r.shape[1]))).reshape(DEPTH, 1, LANES)

    h = x.reshape(TOKENS, D_MODEL)
    for l in range(DEPTH):
        lam_init = 0.8 - 0.6 * math.exp(-0.3 * l)
        proj, u_stack, vt, fft = _inproj(h, w_tok, w_vt, wff_t, cos, sin, l)

        y_stack = _s5_scan(u_stack, s5_mats, l)
        y_ssm = _s5_post(y_stack, u_stack, ssm_d[l].reshape(1, SSM_WIDTH).astype(F32), w_glu, l)

        y_diff = _diff(proj, vt, diff_lambda[l].astype(F32),
                       diff_norm_g[l].reshape(1, 512).astype(F32), lam_init)
        c = _fgate(fft, fox_f_bias[l].reshape(FOX_HEADS, 1).astype(F32))
        y_fox = _fox(proj, vt, c)

        h = _merge(h, y_ssm, y_diff, y_fox, wgl, w_br, w_o,
                   ln1_g[l].reshape(1, D_MODEL), ln1_b[l].reshape(1, D_MODEL), l)

        h = _moe(h, w_r, b_r, w_gate, w_up, w_down,
                 ln2_g[l].reshape(1, D_MODEL), ln2_b[l].reshape(1, D_MODEL), l)
    return h.reshape(BATCH, SEQ, D_MODEL)
```
